```python
import math
import jax
import jax.numpy as jnp
from jax import lax
import numpy as np

D_MODEL = 1024
BATCH = 4
SEQ = 8192
DEPTH = 2

HEAD_DIM = 64
A_Q_HEADS = 8
A_KV_HEADS = 2
A_RADIUS = 128
B_HEADS = 4
B_V_DIM = 2 * HEAD_DIM
C_PATTERNS = ((128, 1), (512, 4), (2048, 16))
C_HEADS = 8
D_HEADS = 8
D_Q_RANK = 384
D_KV_RANK = 256
D_NOPE = 64
D_ROPE = 32
D_V = 64
ROPE_THETA = 10000.0
N_BRANCH = 4
BRANCH_WIDTH = 512
D_FF = 2816
CONV_WIDTH = 3
Q_BLOCK = 128
EPS = 1e-6
NEG = -1e30
IN_SPLITS = (
    A_Q_HEADS * HEAD_DIM,
    A_KV_HEADS * HEAD_DIM,
    A_KV_HEADS * HEAD_DIM,
    2 * B_HEADS * HEAD_DIM,
    2 * B_HEADS * HEAD_DIM,
    B_HEADS * B_V_DIM,
    3 * len(C_PATTERNS) * C_HEADS * HEAD_DIM,
    D_Q_RANK,
    D_KV_RANK,
    D_ROPE,
    N_BRANCH * D_MODEL,
)
N_IN = sum(IN_SPLITS)

kernel_name = "hybrid_gated_parallel_attn_encoder"


def _split_points(sizes):
    pts, acc = [], 0
    for n in sizes[:-1]:
        acc += n
        pts.append(acc)
    return pts


def rms_norm(x, g):
    xf = x.astype(jnp.float32)
    y = xf * lax.rsqrt(jnp.mean(xf * xf, axis=-1, keepdims=True) + EPS) * g.astype(jnp.float32)
    return y.astype(x.dtype)


def alibi_slopes(n):
    return jnp.asarray(2.0 ** (-8.0 * np.arange(1, n + 1) / n), jnp.float32)


def rope(t, pos):
    half = t.shape[-1] // 2
    inv = ROPE_THETA ** (-jnp.arange(half, dtype=jnp.float32) / half)
    ang = pos.astype(jnp.float32)[:, :, None, None] * inv
    c, sn = jnp.cos(ang), jnp.sin(ang)
    t1 = t[..., :half].astype(jnp.float32)
    t2 = t[..., half:].astype(jnp.float32)
    return jnp.concatenate([t1 * c - t2 * sn, t1 * sn + t2 * c], axis=-1).astype(t.dtype)


def banded_attention(q, k, v, pos, radius, slopes, sink=None):
    n, length, kh, g, d = q.shape
    blk = radius
    nb = -(-length // blk)
    pad = nb * blk - length

    def padl(t):
        return jnp.pad(t, [(0, 0), (0, pad)] + [(0, 0)] * (t.ndim - 2))

    def windows(t):
        tb = t.reshape((n, nb, blk) + t.shape[2:])
        tp = jnp.pad(tb, [(0, 0), (1, 1)] + [(0, 0)] * (tb.ndim - 2))
        return jnp.concatenate([tp[:, :-2], tp[:, 1:-1], tp[:, 2:]], axis=2)

    valid = jnp.broadcast_to(jnp.arange(nb * blk) < length, (n, nb * blk))
    qb = padl(q).reshape(n, nb, blk, kh, g, d)
    kw, vw = windows(padl(k)), windows(padl(v))
    posf = padl(pos).astype(jnp.float32)
    pw, vwin = windows(posf), windows(valid)
    s = jnp.einsum('nbqkgd,nbskd->nbkgqs', qb, kw, preferred_element_type=jnp.float32) * (d ** -0.5)
    dist = jnp.abs(posf.reshape(n, nb, blk)[..., :, None] - pw[..., None, :])
    s = s - slopes[None, None, :, :, None, None] * dist[:, :, None, None]
    rel = (jnp.arange(3 * blk) - blk)[None, :] - jnp.arange(blk)[:, None]
    mask = (jnp.abs(rel) <= radius)[None, None] & vwin[:, :, None, :]
    s = jnp.where(mask[:, :, None, None], s, NEG)
    m = jnp.max(s, axis=-1)
    if sink is not None:
        sk = sink.astype(jnp.float32)[None, None, :, :, None]
        m = jnp.maximum(m, sk)
    p = jnp.exp(s - m[..., None])
    den = jnp.sum(p, axis=-1)
    if sink is not None:
        den = den + jnp.exp(sk - m)
    o = jnp.einsum('nbkgqs,nbskv->nbqkgv', p.astype(v.dtype), vw, preferred_element_type=jnp.float32)
    o = o / jnp.transpose(den, (0, 1, 4, 2, 3))[..., None]
    lse = jnp.transpose(m + jnp.log(den), (0, 1, 4, 2, 3))
    o = o.reshape(n, nb * blk, kh, g, -1)[:, :length]
    lse = lse.reshape(n, nb * blk, kh, g)[:, :length]
    return o.astype(q.dtype), lse


def windowed_gqa(a_q, a_k, a_v, pos, q_norm, k_norm, sink):
    b, s, _ = a_q.shape
    grp = A_Q_HEADS // A_KV_HEADS
    q = rms_norm(a_q.reshape(b, s, A_KV_HEADS, grp, HEAD_DIM), q_norm)
    k = rms_norm(a_k.reshape(b, s, A_KV_HEADS, HEAD_DIM), k_norm)
    v = a_v.reshape(b, s, A_KV_HEADS, HEAD_DIM)
    slopes = alibi_slopes(A_Q_HEADS).reshape(A_KV_HEADS, grp)
    o, _ = banded_attention(q, k, v, pos, A_RADIUS, slopes, sink.reshape(A_KV_HEADS, grp))
    return o.reshape(b, s, A_Q_HEADS * HEAD_DIM)


def differential_attention(b_q, b_k, b_v, pos, q_norm, k_norm, lam_q1, lam_k1, lam_q2, lam_k2, subln_g, lam_init):
    b, s, _ = b_q.shape
    q = rms_norm(b_q.reshape(b, s, 2, B_HEADS, HEAD_DIM), q_norm)
    k = rms_norm(b_k.reshape(b, s, 2, B_HEADS, HEAD_DIM), k_norm)
    v = b_v.reshape(b, s, B_HEADS, B_V_DIM)
    lam = (jnp.exp(jnp.sum(lam_q1 * lam_k1, -1).astype(jnp.float32))
           - jnp.exp(jnp.sum(lam_q2 * lam_k2, -1).astype(jnp.float32)) + lam_init)
    slopes = alibi_slopes(B_HEADS)
    nb = s // Q_BLOCK
    posf = pos.astype(jnp.float32)
    qb = jnp.moveaxis(q.reshape(b, nb, Q_BLOCK, 2, B_HEADS, HEAD_DIM), 1, 0)
    pb = jnp.moveaxis(posf.reshape(b, nb, Q_BLOCK), 1, 0)
    scale = HEAD_DIM ** -0.5

    def block(args):
        qblk, pblk = args
        sc = jnp.einsum('bqmhd,bsmhd->bmhqs', qblk, k, preferred_element_type=jnp.float32) * scale
        dist = jnp.abs(pblk[:, :, None] - posf[:, None, :])
        sc = sc - slopes[None, None, :, None, None] * dist[:, None, None]
        a = jax.nn.softmax(sc, axis=-1)
        a = a[:, 0] - lam[None, :, None, None] * a[:, 1]
        return jnp.einsum('bhqs,bshv->bqhv', a.astype(v.dtype), v, preferred_element_type=jnp.float32)

    o = lax.map(block, (qb, pb))
    o = jnp.moveaxis(o, 0, 1).reshape(b, s, B_HEADS, B_V_DIM)
    o = rms_norm(o, subln_g) * (1.0 - lam_init)
    return o.reshape(b, s, B_HEADS * B_V_DIM).astype(b_v.dtype)


def dilated_attention(c_qkv, pos, q_norm, k_norm):
    b, s, _ = c_qkv.shape
    ng = len(C_PATTERNS)
    per = c_qkv.reshape(b, s, ng, 3, C_HEADS, HEAD_DIM)
    slopes = alibi_slopes(ng * C_HEADS).reshape(ng, C_HEADS)
    outs, lses = [], []
    for j, (window, r) in enumerate(C_PATTERNS):
        def strided(t):
            t = t.reshape((b, s // r, r) + t.shape[2:])
            t = jnp.moveaxis(t, 2, 1)
            return t.reshape((b * r, s // r) + t.shape[3:])

        def unstrided(t):
            t = t.reshape((b, r, s // r) + t.shape[2:])
            t = jnp.moveaxis(t, 1, 2)
            return t.reshape((b, s) + t.shape[3:])

        q = rms_norm(per[:, :, j, 0], q_norm)
        k = rms_norm(per[:, :, j, 1], k_norm)
        v = per[:, :, j, 2]
        o, lse = banded_attention(strided(q)[:, :, :, None], strided(k), strided(v), strided(pos),
                                  window // (2 * r), slopes[j][:, None])
        outs.append(unstrided(o[:, :, :, 0]))
        lses.append(unstrided(lse[:, :, :, 0]))
    w = jax.nn.softmax(jnp.stack(lses, 0), axis=0)
    o = jnp.sum(w[..., None] * jnp.stack(outs, 0).astype(jnp.float32), axis=0)
    return o.reshape(b, s, C_HEADS * HEAD_DIM).astype(c_qkv.dtype)


def latent_attention(d_cq, d_ckv, d_kr, pos, cq_norm, ckv_norm, w_uq, w_ukv, q_norm, k_norm):
    b, s, _ = d_cq.shape
    q = (rms_norm(d_cq, cq_norm) @ w_uq).reshape(b, s, D_HEADS, D_NOPE + D_ROPE)
    kv = (rms_norm(d_ckv, ckv_norm) @ w_ukv).reshape(b, s, D_HEADS, D_NOPE + D_V)
    k = jnp.concatenate([kv[..., :D_NOPE], jnp.broadcast_to(d_kr[:, :, None, :], (b, s, D_HEADS, D_ROPE))], axis=-1)
    v = kv[..., D_NOPE:]
    q = rms_norm(q, q_norm)
    k = rms_norm(k, k_norm)
    q = jnp.concatenate([q[..., :D_NOPE], rope(q[..., D_NOPE:], pos)], axis=-1)
    k = jnp.concatenate([k[..., :D_NOPE], rope(k[..., D_NOPE:], pos)], axis=-1)
    nb = s // Q_BLOCK
    qb = jnp.moveaxis(q.reshape(b, nb, Q_BLOCK, D_HEADS, D_NOPE + D_ROPE), 1, 0)
    scale = (D_NOPE + D_ROPE) ** -0.5

    def block(qblk):
        sc = jnp.einsum('bqhd,bshd->bhqs', qblk, k, preferred_element_type=jnp.float32) * scale
        a = jax.nn.softmax(sc, axis=-1)
        return jnp.einsum('bhqs,bshv->bqhv', a.astype(v.dtype), v, preferred_element_type=jnp.float32)

    o = jnp.moveaxis(lax.map(block, qb), 0, 1)
    return o.reshape(b, s, D_HEADS * D_V).astype(d_cq.dtype)


def depthwise_conv(u, w, bias):
    c = u.shape[-1]
    y = lax.conv_general_dilated(u, w[:, None, :].astype(u.dtype), window_strides=(1,), padding='SAME',
                                 dimension_numbers=('NWC', 'WIO', 'NWC'), feature_group_count=c)
    return y + bias.astype(u.dtype)


def hybrid_layer(x, pos, layer_idx, norm1_g, w_in, a_q_norm, a_k_norm, a_sink, b_q_norm, b_k_norm,
                 b_lam_q1, b_lam_k1, b_lam_q2, b_lam_k2, b_subln_g, c_q_norm, c_k_norm, d_cq_norm,
                 d_ckv_norm, d_w_uq, d_w_ukv, d_q_norm, d_k_norm, w_branch, w_o, norm2_g, ffn_w_up,
                 ffn_conv_w, ffn_conv_b, ffn_w_down):
    h = rms_norm(x, norm1_g)
    (a_q, a_k, a_v, b_q, b_k, b_v, c_qkv, d_cq, d_ckv, d_kr, gates) = jnp.split(
        h @ w_in, _split_points(IN_SPLITS), axis=-1)
    lam_init = 0.8 - 0.6 * math.exp(-0.3 * layer_idx)
    oa = windowed_gqa(a_q, a_k, a_v, pos, a_q_norm, a_k_norm, a_sink)
    ob = differential_attention(b_q, b_k, b_v, pos, b_q_norm, b_k_norm, b_lam_q1, b_lam_k1,
                                b_lam_q2, b_lam_k2, b_subln_g, lam_init)
    oc = dilated_attention(c_qkv, pos, c_q_norm, c_k_norm)
    od = latent_attention(d_cq, d_ckv, d_kr, pos, d_cq_norm, d_ckv_norm, d_w_uq, d_w_ukv, d_q_norm, d_k_norm)
    merged = jnp.zeros_like(x)
    for i, o in enumerate((oa, ob, oc, od)):
        gate = jax.nn.sigmoid(gates[..., i * D_MODEL:(i + 1) * D_MODEL])
        merged = merged + gate * (o @ w_branch[i])
    x = x + merged @ w_o
    u = depthwise_conv(rms_norm(x, norm2_g) @ ffn_w_up, ffn_conv_w, ffn_conv_b)
    g, up = jnp.split(u, 2, axis=-1)
    return x + (jax.nn.silu(g) * up) @ ffn_w_down


def setup_inputs(seed: int = 0) -> dict:
    key = jax.random.key(seed)
    ks = iter(jax.random.split(key, 40))
    f32 = jnp.float32
    L = DEPTH

    def nrm(shape, scale):
        return jax.random.normal(next(ks), shape, f32) * scale

    def gain(shape):
        return 1.0 + nrm(shape, 0.05)

    x = jax.random.normal(next(ks), (BATCH, SEQ, D_MODEL), f32)
    positions = (jnp.arange(SEQ, dtype=jnp.int32)[None, :]
                 + jax.random.randint(next(ks), (BATCH, 1), 0, 4096, dtype=jnp.int32))
    return {
        "x": x,
        "positions": positions,
        "norm1_g": gain((L, D_MODEL)),
        "w_in": nrm((L, D_MODEL, N_IN), D_MODEL ** -0.5),
        "a_q_norm": gain((L, HEAD_DIM)),
        "a_k_norm": gain((L, HEAD_DIM)),
        "a_sink": nrm((L, A_Q_HEADS), 0.5),
        "b_q_norm": gain((L, HEAD_DIM)),
        "b_k_norm": gain((L, HEAD_DIM)),
        "b_lam_q1": nrm((L, B_HEADS, HEAD_DIM), 0.1),
        "b_lam_k1": nrm((L, B_HEADS, HEAD_DIM), 0.1),
        "b_lam_q2": nrm((L, B_HEADS, HEAD_DIM), 0.1),
        "b_lam_k2": nrm((L, B_HEADS, HEAD_DIM), 0.1),
        "b_subln_g": gain((L, B_V_DIM)),
        "c_q_norm": gain((L, HEAD_DIM)),
        "c_k_norm": gain((L, HEAD_DIM)),
        "d_cq_norm": gain((L, D_Q_RANK)),
        "d_ckv_norm": gain((L, D_KV_RANK)),
        "d_w_uq": nrm((L, D_Q_RANK, D_HEADS * (D_NOPE + D_ROPE)), D_Q_RANK ** -0.5),
        "d_w_ukv": nrm((L, D_KV_RANK, D_HEADS * (D_NOPE + D_V)), D_KV_RANK ** -0.5),
        "d_q_norm": gain((L, D_NOPE + D_ROPE)),
        "d_k_norm": gain((L, D_NOPE + D_ROPE)),
        "w_branch": nrm((L, N_BRANCH, BRANCH_WIDTH, D_MODEL), BRANCH_WIDTH ** -0.5),
        "w_o": nrm((L, D_MODEL, D_MODEL), D_MODEL ** -0.5),
        "norm2_g": gain((L, D_MODEL)),
        "ffn_w_up": nrm((L, D_MODEL, 2 * D_FF), D_MODEL ** -0.5),
        "ffn_conv_w": nrm((L, CONV_WIDTH, 2 * D_FF), CONV_WIDTH ** -0.5),
        "ffn_conv_b": nrm((L, 2 * D_FF), 0.01),
        "ffn_w_down": nrm((L, D_FF, D_MODEL), D_FF ** -0.5),
    }


def reference(x, positions, norm1_g, w_in, a_q_norm, a_k_norm, a_sink, b_q_norm, b_k_norm,
              b_lam_q1, b_lam_k1, b_lam_q2, b_lam_k2, b_subln_g, c_q_norm, c_k_norm, d_cq_norm,
              d_ckv_norm, d_w_uq, d_w_ukv, d_q_norm, d_k_norm, w_branch, w_o, norm2_g, ffn_w_up,
              ffn_conv_w, ffn_conv_b, ffn_w_down):
    for l in range(DEPTH):
        x = hybrid_layer(x, positions, l, norm1_g[l], w_in[l], a_q_norm[l], a_k_norm[l], a_sink[l],
                         b_q_norm[l], b_k_norm[l], b_lam_q1[l], b_lam_k1[l], b_lam_q2[l], b_lam_k2[l],
                         b_subln_g[l], c_q_norm[l], c_k_norm[l], d_cq_norm[l], d_ckv_norm[l], d_w_uq[l],
                         d_w_ukv[l], d_q_norm[l], d_k_norm[l], w_branch[l], w_o[l], norm2_g[l],
                         ffn_w_up[l], ffn_conv_w[l], ffn_conv_b[l], ffn_w_down[l])
    return x
```

```python
import functools
import math

import jax
import jax.numpy as jnp
import numpy as np
from jax import lax
from jax.experimental import pallas as pl
from jax.experimental.pallas import tpu as pltpu

F32 = jnp.float32
BF16 = jnp.bfloat16

LANES = 128
HEAD_DIM = 64
A_Q_HEADS = 8
A_RADIUS = 128
B_HEADS = 4
C_PATTERNS = ((128, 1), (512, 4), (2048, 16))
C_HEADS = 8
D_HEADS = 8
D_Q_RANK = 384
D_KV_RANK = 256
D_NOPE = 64
D_ROPE = 32
D_V = 64
ROPE_THETA = 10000.0
N_BRANCH = 4
BRANCH_WIDTH = 512
EPS = 1e-6
NEG = -1e30
VMEM_LIMIT_BYTES = 56 * 1024 * 1024

NT_DIMS = (((1,), (1,)), ((), ()))


def _tile(n, pref):
    return pref if n % pref == 0 else n


def _params(*sem):
    return pltpu.CompilerParams(dimension_semantics=sem, vmem_limit_bytes=VMEM_LIMIT_BYTES)


def _in_proj_kernel(x_ref, g_ref, w_ref, cg_ref, nf_ref, ones_ref, y_ref, h_ref, hn_scr, *, tn):
    @pl.when(pl.program_id(1) == 0)
    def _():
        x = x_ref[...]
        ms = jnp.mean(x * x, axis=-1, keepdims=True)
        hn = (x * lax.rsqrt(ms + EPS) * g_ref[...]).astype(BF16)
        hn_scr[...] = hn
        h_ref[...] = hn

    y = jnp.dot(hn_scr[...], w_ref[...], preferred_element_type=F32)
    y2 = (y * y).astype(BF16)
    seg = jnp.concatenate(
        [jnp.dot(y2[:, c * 256:(c + 1) * 256], ones_ref[...], preferred_element_type=F32)
         for c in range(tn // 256)], axis=1)
    rs = lax.rsqrt(seg * (1.0 / HEAD_DIM) + EPS)
    scale = jnp.where(nf_ref[...] > 0.0, rs, 1.0) * cg_ref[...]
    y_ref[...] = (y * scale).astype(y_ref.dtype)


def _in_proj(x2d, g, w, colgain, normflag):
    t, d = x2d.shape
    n = w.shape[1]
    tm = _tile(t, 1024)
    tn = _tile(n, 768)
    seg_ones = jnp.asarray(np.kron(np.eye(256 // HEAD_DIM), np.ones((HEAD_DIM, HEAD_DIM))), BF16)
    return pl.pallas_call(
        functools.partial(_in_proj_kernel, tn=tn),
        grid=(t // tm, n // tn),
        in_specs=[
            pl.BlockSpec((tm, d), lambda i, j: (i, 0)),
            pl.BlockSpec((1, d), lambda i, j: (0, 0)),
            pl.BlockSpec((d, tn), lambda i, j: (0, j)),
            pl.BlockSpec((1, tn), lambda i, j: (0, j)),
            pl.BlockSpec((1, tn), lambda i, j: (0, j)),
            pl.BlockSpec((256, 256), lambda i, j: (0, 0)),
        ],
        out_specs=[
            pl.BlockSpec((tm, tn), lambda i, j: (i, j)),
            pl.BlockSpec((tm, d), lambda i, j: (i, 0)),
        ],
        out_shape=[jax.ShapeDtypeStruct((t, n), BF16), jax.ShapeDtypeStruct((t, d), BF16)],
        scratch_shapes=[pltpu.VMEM((tm, d), BF16)],
        compiler_params=_params("parallel", "arbitrary"),
        name="in_proj",
    )(x2d, g, w, colgain, normflag, seg_ones)


def _rope_slab(t, c_tab, s1_tab, s2_tab):
    return t * c_tab + pltpu.roll(t, LANES - D_ROPE // 2, 1) * s1_tab + pltpu.roll(t, D_ROPE // 2, 1) * s2_tab


def _d_proj_kernel(h_ref, wd_ref, gcq_ref, gckv_ref, wuq_ref, wuk_ref, wuv_ref, gq_ref, gk_ref,
                   pos_ref, inv_ref, q_ref, k_ref, v_ref):
    y = jnp.dot(h_ref[...], wd_ref[...], preferred_element_type=F32)
    cq = y[:, :D_Q_RANK]
    ckv = y[:, D_Q_RANK:D_Q_RANK + D_KV_RANK]
    kr = y[:, D_Q_RANK + D_KV_RANK:]
    cqn = (cq * lax.rsqrt(jnp.mean(cq * cq, axis=-1, keepdims=True) + EPS) * gcq_ref[...]).astype(BF16)
    ckvn = (ckv * lax.rsqrt(jnp.mean(ckv * ckv, axis=-1, keepdims=True) + EPS) * gckv_ref[...]).astype(BF16)
    q = jnp.dot(cqn, wuq_ref[...], preferred_element_type=F32)
    kn = jnp.dot(ckvn, wuk_ref[...], preferred_element_type=F32)
    v = jnp.dot(ckvn, wuv_ref[...], preferred_element_type=F32)

    lane = lax.broadcasted_iota(jnp.int32, (1, LANES), 1)
    ang = pos_ref[...] * inv_ref[...]
    cs, sn = jnp.cos(ang), jnp.sin(ang)
    half = D_ROPE // 2
    c_tab = jnp.where((lane >= D_NOPE) & (lane < D_NOPE + D_ROPE), cs, 1.0)
    s1_tab = jnp.where((lane >= D_NOPE) & (lane < D_NOPE + half), -sn, 0.0)
    s2_tab = jnp.where((lane >= D_NOPE + half) & (lane < D_NOPE + D_ROPE), sn, 0.0)
    inv_w = 1.0 / (D_NOPE + D_ROPE)
    for hd in range(D_HEADS):
        sl = slice(hd * LANES, (hd + 1) * LANES)
        qs = q[:, sl]
        qs = qs * lax.rsqrt(jnp.sum(qs * qs, axis=-1, keepdims=True) * inv_w + EPS) * gq_ref[...]
        q_ref[:, sl] = _rope_slab(qs, c_tab, s1_tab, s2_tab).astype(q_ref.dtype)
        ks = kn[:, sl] + kr
        ks = ks * lax.rsqrt(jnp.sum(ks * ks, axis=-1, keepdims=True) * inv_w + EPS) * gk_ref[...]
        k_ref[:, sl] = _rope_slab(ks, c_tab, s1_tab, s2_tab).astype(k_ref.dtype)
        v_ref[:, sl] = jnp.where(lane == D_V, 1.0, v[:, sl]).astype(v_ref.dtype)


def _d_proj(h2d, wd, gcq, gckv, wuq, wuk, wuv, gq, gk, pos_col, inv_tab):
    t, d = h2d.shape
    tm = _tile(t, 512)
    nd = D_HEADS * LANES
    full = lambda a: pl.BlockSpec(a.shape, lambda i: (0,) * a.ndim)
    out = jax.ShapeDtypeStruct((t, nd), BF16)
    return pl.pallas_call(
        _d_proj_kernel,
        grid=(t // tm,),
        in_specs=[pl.BlockSpec((tm, d), lambda i: (i, 0)), full(wd), full(gcq), full(gckv), full(wuq),
                  full(wuk), full(wuv), full(gq), full(gk), pl.BlockSpec((tm, 1), lambda i: (i, 0)),
                  full(inv_tab)],
        out_specs=[pl.BlockSpec((tm, nd), lambda i: (i, 0))] * 3,
        out_shape=[out, out, out],
        compiler_params=_params("parallel"),
        name="d_proj",
    )(h2d, wd, gcq, gckv, wuq, wuk, wuv, gq, gk, pos_col, inv_tab)


def _band_kernel(slope_ref, sink_ref, q_ref, kp_ref, kc_ref, kn_ref, vp_ref, vc_ref, vn_ref,
                 posq_ref, posw_ref, *out_refs, radius, tq, length, has_sink, want_lse):
    i = pl.program_id(1)
    t = pl.program_id(2)
    w = tq + 2 * radius
    q = q_ref[0]
    kw = jnp.concatenate([kp_ref[0], kc_ref[0], kn_ref[0]], axis=0)
    vw = jnp.concatenate([vp_ref[0], vc_ref[0], vn_ref[0]], axis=0)
    r_io = lax.broadcasted_iota(jnp.int32, (tq, w), 0)
    c_io = lax.broadcasted_iota(jnp.int32, (tq, w), 1)
    jabs = i * tq - radius + c_io
    mask = (jnp.abs(c_io - radius - r_io) <= radius) & (jabs >= 0) & (jabs < length)
    dist = jnp.abs(posq_ref[0] - posw_ref[0, 0])
    lane = lax.broadcasted_iota(jnp.int32, (tq, LANES), 1)
    outs, lses = [], []
    for e in range(2):
        in_half = (lane >= HEAD_DIM) if e else (lane < HEAD_DIM)
        qe = jnp.where(in_half, q, jnp.zeros_like(q))
        s = lax.dot_general(qe, kw, NT_DIMS, preferred_element_type=F32)
        s = s - slope_ref[t, e] * dist
        s = jnp.where(mask, s, NEG)
        m = jnp.max(s, axis=-1, keepdims=True)
        if has_sink:
            sk = sink_ref[t, e]
            m = jnp.maximum(m, sk)
        p = jnp.exp(s - m)
        den = jnp.sum(p, axis=-1, keepdims=True)
        if has_sink:
            den = den + jnp.exp(sk - m)
        o = jnp.dot(p.astype(BF16), vw, preferred_element_type=F32) / den
        outs.append(o)
        lses.append(m + jnp.log(den))
    out_refs[0][0] = jnp.where(lane < HEAD_DIM, outs[0], outs[1]).astype(out_refs[0].dtype)
    if want_lse:
        out_refs[1][0] = jnp.where(lane < HEAD_DIM, lses[0], lses[1])


def _band_attention(src, pos, radius, slopes, sink, q_slab0, k_slab0, v_slab0, kv_per_q, want_lse):
    bn, length, _ = src.shape
    nslab = 4
    tq = _tile(length, 256)
    nb = length // tq
    rb = tq // radius
    nrb = length // radius
    w = tq + 2 * radius
    idx = jnp.clip(jnp.arange(nb)[:, None] * tq - radius + jnp.arange(w)[None, :], 0, length - 1)
    posf = pos.astype(F32)
    posw = posf[:, idx][:, :, None, :]
    posq = posf[:, :, None]
    has_sink = sink is not None
    if sink is None:
        sink = jnp.zeros((nslab, 2), F32)

    def halo(slab0, which):
        if which == 0:
            return pl.BlockSpec((1, radius, LANES),
                                lambda b, i, t: (b, jnp.maximum(i * rb - 1, 0), slab0 + t * kv_per_q))
        if which == 1:
            return pl.BlockSpec((1, tq, LANES), lambda b, i, t: (b, i, slab0 + t * kv_per_q))
        return pl.BlockSpec((1, radius, LANES),
                            lambda b, i, t: (b, jnp.minimum((i + 1) * rb, nrb - 1), slab0 + t * kv_per_q))

    smem = pl.BlockSpec(memory_space=pltpu.SMEM)
    out_specs = [pl.BlockSpec((1, tq, LANES), lambda b, i, t: (b, i, t))]
    out_shape = [jax.ShapeDtypeStruct((bn, length, nslab * LANES), BF16)]
    if want_lse:
        out_specs.append(pl.BlockSpec((1, tq, LANES), lambda b, i, t: (b, i, t)))
        out_shape.append(jax.ShapeDtypeStruct((bn, length, nslab * LANES), F32))
    res = pl.pallas_call(
        functools.partial(_band_kernel, radius=radius, tq=tq, length=length, has_sink=has_sink,
                          want_lse=want_lse),
        grid=(bn, nb, nslab),
        in_specs=[smem, smem,
                  pl.BlockSpec((1, tq, LANES), lambda b, i, t: (b, i, q_slab0 + t)),
                  halo(k_slab0, 0), halo(k_slab0, 1), halo(k_slab0, 2),
                  halo(v_slab0, 0), halo(v_slab0, 1), halo(v_slab0, 2),
                  pl.BlockSpec((1, tq, 1), lambda b, i, t: (b, i, 0)),
                  pl.BlockSpec((1, 1, 1, w), lambda b, i, t: (b, i, 0, 0))],
        out_specs=out_specs,
        out_shape=out_shape,
        compiler_params=_params("parallel", "parallel", "arbitrary"),
        name="band_attn",
    )(slopes, sink, src, src, src, src, src, src, src, posq, posw)
    return res


def _c_merge_kernel(o0_ref, o1_ref, o2_ref, l0_ref, l1_ref, l2_ref, out_ref):
    l0, l1, l2 = l0_ref[...], l1_ref[...], l2_ref[...]
    m = jnp.maximum(jnp.maximum(l0, l1), l2)
    w0, w1, w2 = jnp.exp(l0 - m), jnp.exp(l1 - m), jnp.exp(l2 - m)
    num = w0 * o0_ref[...].astype(F32) + w1 * o1_ref[...].astype(F32) + w2 * o2_ref[...].astype(F32)
    out_ref[...] = (num / (w0 + w1 + w2)).astype(out_ref.dtype)


def _c_merge(outs, lses):
    t, n = outs[0].shape
    tm = _tile(t, 1024)
    spec = pl.BlockSpec((tm, n), lambda i: (i, 0))
    return pl.pallas_call(
        _c_merge_kernel,
        grid=(t // tm,),
        in_specs=[spec] * 6,
        out_specs=spec,
        out_shape=jax.ShapeDtypeStruct((t, n), BF16),
        compiler_params=_params("parallel"),
        name="c_merge",
    )(*outs, *lses)


def _flash_update(u, q, k_c, v_c, bias, m_scr, l_scr, acc_scr):
    s = lax.dot_general(q, k_c, NT_DIMS, preferred_element_type=F32)
    if bias is not None:
        s = s - bias
    m_old = m_scr[u]
    m_new = jnp.maximum(m_old, jnp.max(s, axis=-1, keepdims=True))
    alpha = jnp.exp(m_old - m_new)
    p = jnp.exp(s - m_new)
    if l_scr is not None:
        l_scr[u] = alpha * l_scr[u] + jnp.sum(p, axis=-1, keepdims=True)
    acc_scr[u] = alpha * acc_scr[u] + jnp.dot(p.astype(BF16), v_c, preferred_element_type=F32)
    m_scr[u] = m_new


def _flash_b_kernel(slope_ref, q0_ref, q1_ref, k0_ref, k1_ref, va_ref, vb_ref, posq_ref, posk_ref,
                    lq1_ref, lk1_ref, lq2_ref, lk2_ref, subg_ref, o_ref, m_scr, l_scr, acc_scr,
                    *, tq, tk, seq, lam_init):
    hp = pl.program_id(1)
    m_scr[...] = jnp.full(m_scr.shape, NEG, F32)
    l_scr[...] = jnp.zeros(l_scr.shape, F32)
    acc_scr[...] = jnp.zeros(acc_scr.shape, F32)
    lane = lax.broadcasted_iota(jnp.int32, (tq, LANES), 1)
    q_maps = (q0_ref[0], q1_ref[0])
    k_refs = (k0_ref, k1_ref)
    v_refs = (va_ref, vb_ref)
    qs = [[jnp.where((lane >= HEAD_DIM) if e else (lane < HEAD_DIM), qm, jnp.zeros_like(qm))
           for e in range(2)] for qm in q_maps]
    posq = posq_ref[0]

    def body(j, carry):
        off = pl.multiple_of(j * tk, tk)
        dist = jnp.abs(posq - posk_ref[0, :, pl.ds(off, tk)])
        for e in range(2):
            bias = slope_ref[2 * hp + e] * dist
            v_c = v_refs[e][0, pl.ds(off, tk), :]
            for mp in range(2):
                k_c = k_refs[mp][0, pl.ds(off, tk), :]
                _flash_update(2 * mp + e, qs[mp][e], k_c, v_c, bias, m_scr, l_scr, acc_scr)
        return carry

    lax.fori_loop(0, seq // tk, body, 0)

    for e in range(2):
        hd = 2 * hp + e
        lam1 = jnp.sum(lq1_ref[pl.ds(hd, 1), :] * lk1_ref[pl.ds(hd, 1), :], axis=-1, keepdims=True)
        lam2 = jnp.sum(lq2_ref[pl.ds(hd, 1), :] * lk2_ref[pl.ds(hd, 1), :], axis=-1, keepdims=True)
        lam = jnp.exp(lam1) - jnp.exp(lam2) + lam_init
        a = acc_scr[e] / l_scr[e] - lam * (acc_scr[2 + e] / l_scr[2 + e])
        ms = jnp.mean(a * a, axis=-1, keepdims=True)
        out = a * lax.rsqrt(ms + EPS) * subg_ref[...] * (1.0 - lam_init)
        o_ref[0, :, e * LANES:(e + 1) * LANES] = out.astype(o_ref.dtype)


def _flash_b(y_main, pos, slopes, lq1, lk1, lq2, lk2, subg, lam_init, q_slab0, k_slab0, v_slab0):
    bsz, seq, _ = y_main.shape
    tq = _tile(seq, 512)
    tk = _tile(seq, 512)
    posf = pos.astype(F32)
    posq = posf[:, :, None]
    posk = posf[:, None, :]
    kv_spec = lambda slab_fn: pl.BlockSpec((1, seq, LANES), lambda b, hp, i: (b, 0, slab_fn(hp)))
    small = lambda a: pl.BlockSpec(a.shape, lambda b, hp, i: (0,) * a.ndim)
    return pl.pallas_call(
        functools.partial(_flash_b_kernel, tq=tq, tk=tk, seq=seq, lam_init=lam_init),
        grid=(bsz, B_HEADS // 2, seq // tq),
        in_specs=[pl.BlockSpec(memory_space=pltpu.SMEM),
                  pl.BlockSpec((1, tq, LANES), lambda b, hp, i: (b, i, q_slab0 + hp)),
                  pl.BlockSpec((1, tq, LANES), lambda b, hp, i: (b, i, q_slab0 + 2 + hp)),
                  kv_spec(lambda hp: k_slab0 + hp), kv_spec(lambda hp: k_slab0 + 2 + hp),
                  kv_spec(lambda hp: v_slab0 + 2 * hp), kv_spec(lambda hp: v_slab0 + 2 * hp + 1),
                  pl.BlockSpec((1, tq, 1), lambda b, hp, i: (b, i, 0)),
                  pl.BlockSpec((1, 1, seq), lambda b, hp, i: (b, 0, 0)),
                  small(lq1), small(lk1), small(lq2), small(lk2), small(subg)],
        out_specs=pl.BlockSpec((1, tq, 2 * LANES), lambda b, hp, i: (b, i, hp)),
        out_shape=jax.ShapeDtypeStruct((bsz, seq, B_HEADS * LANES), BF16),
        scratch_shapes=[pltpu.VMEM((4, tq, 1), F32), pltpu.VMEM((4, tq, 1), F32),
                        pltpu.VMEM((4, tq, LANES), F32)],
        compiler_params=_params("parallel", "parallel", "arbitrary"),
        name="flash_diff",
    )(slopes, y_main, y_main, y_main, y_main, y_main, y_main, posq, posk, lq1, lk1, lq2, lk2, subg)


def _flash_d_kernel(qa_ref, qb_ref, ka_ref, kb_ref, va_ref, vb_ref, o_ref, m_scr, acc_scr, *, tq, tk, seq):
    m_scr[...] = jnp.full(m_scr.shape, NEG, F32)
    acc_scr[...] = jnp.zeros(acc_scr.shape, F32)
    qs = (qa_ref[0], qb_ref[0])
    k_refs = (ka_ref, kb_ref)
    v_refs = (va_ref, vb_ref)

    def body(j, carry):
        off = pl.multiple_of(j * tk, tk)
        for e in range(2):
            _flash_update(e, qs[e], k_refs[e][0, pl.ds(off, tk), :], v_refs[e][0, pl.ds(off, tk), :],
                          None, m_scr, None, acc_scr)
        return carry

    lax.fori_loop(0, seq // tk, body, 0)
    lane = lax.broadcasted_iota(jnp.int32, (tq, LANES), 1)
    outs = []
    for e in range(2):
        acc = acc_scr[e]
        outs.append(acc / acc[:, D_V:D_V + 1])
    o_ref[0] = jnp.where(lane < D_V, outs[0], pltpu.roll(outs[1], D_V, 1)).astype(o_ref.dtype)


def _flash_d(qd, kd, vd):
    bsz, seq, _ = qd.shape
    tq = _tile(seq, 512)
    tk = _tile(seq, 512)
    q_spec = lambda e: pl.BlockSpec((1, tq, LANES), lambda b, hp, i: (b, i, 2 * hp + e))
    kv_spec = lambda e: pl.BlockSpec((1, seq, LANES), lambda b, hp, i: (b, 0, 2 * hp + e))
    return pl.pallas_call(
        functools.partial(_flash_d_kernel, tq=tq, tk=tk, seq=seq),
        grid=(bsz, D_HEADS // 2, seq // tq),
        in_specs=[q_spec(0), q_spec(1), kv_spec(0), kv_spec(1), kv_spec(0), kv_spec(1)],
        out_specs=pl.BlockSpec((1, tq, LANES), lambda b, hp, i: (b, i, hp)),
        out_shape=jax.ShapeDtypeStruct((bsz, seq, D_HEADS * D_V), BF16),
        scratch_shapes=[pltpu.VMEM((2, tq, 1), F32), pltpu.VMEM((2, tq, LANES), F32)],
        compiler_params=_params("parallel", "parallel", "arbitrary"),
        name="flash_latent",
    )(qd, qd, kd, kd, vd, vd)


def _merge_kernel(h_ref, x_ref, oa_ref, ob_ref, oc_ref, od_ref, wg_ref, wb_ref, wo_ref, out_ref, acc_scr):
    j = pl.program_id(1)

    @pl.when(j == 0)
    def _():
        acc_scr[...] = jnp.zeros(acc_scr.shape, F32)

    gate = jax.nn.sigmoid(jnp.dot(h_ref[...], wg_ref[...], preferred_element_type=F32))
    for br, o_ref in enumerate((oa_ref, ob_ref, oc_ref, od_ref)):
        @pl.when(j == br)
        def _(o_ref=o_ref):
            acc_scr[...] += gate * jnp.dot(o_ref[...], wb_ref[0], preferred_element_type=F32)

    @pl.when(j == N_BRANCH - 1)
    def _():
        out_ref[...] = x_ref[...] + jnp.dot(acc_scr[...].astype(BF16), wo_ref[...],
                                            preferred_element_type=F32)


def _merge(h2d, x2d, branch_outs, wg, wb, wo):
    t, d = x2d.shape
    tm = _tile(t, 512)
    bw = wb.shape[1]
    row = lambda width: pl.BlockSpec((tm, width), lambda i, j: (i, 0))
    return pl.pallas_call(
        _merge_kernel,
        grid=(t // tm, N_BRANCH),
        in_specs=[row(d), row(d), row(bw), row(bw), row(bw), row(bw),
                  pl.BlockSpec((d, d), lambda i, j: (0, j)),
                  pl.BlockSpec((1, bw, d), lambda i, j: (j, 0, 0)),
                  pl.BlockSpec((d, d), lambda i, j: (0, 0))],
        out_specs=row(d),
        out_shape=jax.ShapeDtypeStruct((t, d), F32),
        scratch_shapes=[pltpu.VMEM((tm, d), F32)],
        compiler_params=_params("parallel", "arbitrary"),
        name="gated_merge",
    )(h2d, x2d, *branch_outs, wg, wb, wo)


HALO = 8


def _ffn_kernel(x_ref, xp_ref, xn_ref, g_ref, wg_ref, wu_ref, cwg_ref, cwu_ref, cbg_ref, cbu_ref,
                wd_ref, out_ref, hn_scr, acc_scr, *, tm, tiles_per_seq):
    i = pl.program_id(0)
    j = pl.program_id(1)
    rows = tm + 2 * HALO

    @pl.when(j == 0)
    def _():
        xe = jnp.concatenate([xp_ref[...], x_ref[...], xn_ref[...]], axis=0)
        ms = jnp.mean(xe * xe, axis=-1, keepdims=True)
        hn = xe * lax.rsqrt(ms + EPS) * g_ref[...]
        r = lax.broadcasted_iota(jnp.int32, (rows, 1), 0)
        first = (i % tiles_per_seq) == 0
        last = (i % tiles_per_seq) == tiles_per_seq - 1
        dead = ((r == HALO - 1) & first) | ((r == HALO + tm) & last)
        hn_scr[...] = jnp.where(dead, 0.0, hn).astype(BF16)
        acc_scr[...] = jnp.zeros(acc_scr.shape, F32)

    def conv(w_ref, cw_ref, cb_ref):
        u = jnp.dot(hn_scr[...], w_ref[...], preferred_element_type=F32)
        y = (cw_ref[0:1, :] * pltpu.roll(u, 1, 0) + cw_ref[1:2, :] * u
             + cw_ref[2:3, :] * pltpu.roll(u, rows - 1, 0))
        return y[HALO:HALO + tm, :] + cb_ref[...]

    yg = conv(wg_ref, cwg_ref, cbg_ref)
    yu = conv(wu_ref, cwu_ref, cbu_ref)
    act = (jax.nn.silu(yg) * yu).astype(BF16)
    acc_scr[...] += jnp.dot(act, wd_ref[...], preferred_element_type=F32)

    @pl.when(j == pl.num_programs(1) - 1)
    def _():
        out_ref[...] = x_ref[...] + acc_scr[...]


def _ffn(x2d, seq, g, w_up, conv_w, conv_b, w_down):
    t, d = x2d.shape
    dff = w_down.shape[0]
    tm = _tile(seq, 512)
    tf = _tile(dff, 256)
    nf = dff // tf
    nhb = t // HALO
    rb = tm // HALO
    return pl.pallas_call(
        functools.partial(_ffn_kernel, tm=tm, tiles_per_seq=seq // tm),
        grid=(t // tm, nf),
        in_specs=[pl.BlockSpec((tm, d), lambda i, j: (i, 0)),
                  pl.BlockSpec((HALO, d), lambda i, j: (jnp.maximum(i * rb - 1, 0), 0)),
                  pl.BlockSpec((HALO, d), lambda i, j: (jnp.minimum((i + 1) * rb, nhb - 1), 0)),
                  pl.BlockSpec((1, d), lambda i, j: (0, 0)),
                  pl.BlockSpec((d, tf), lambda i, j: (0, j)),
                  pl.BlockSpec((d, tf), lambda i, j: (0, nf + j)),
                  pl.BlockSpec((3, tf), lambda i, j: (0, j)),
                  pl.BlockSpec((3, tf), lambda i, j: (0, nf + j)),
                  pl.BlockSpec((1, tf), lambda i, j: (0, j)),
                  pl.BlockSpec((1, tf), lambda i, j: (0, nf + j)),
                  pl.BlockSpec((tf, d), lambda i, j: (j, 0))],
        out_specs=pl.BlockSpec((tm, d), lambda i, j: (i, 0)),
        out_shape=jax.ShapeDtypeStruct((t, d), F32),
        scratch_shapes=[pltpu.VMEM((tm + 2 * HALO, d), BF16), pltpu.VMEM((tm, d), F32)],
        compiler_params=_params("parallel", "arbitrary"),
        name="conv_mlp",
    )(x2d, x2d, x2d, g, w_up, w_up, conv_w, conv_w, conv_b, conv_b, w_down)


def _alibi_slopes(n):
    return 2.0 ** (-8.0 * np.arange(1, n + 1) / n)


def _pair_split_perm():
    cols = []
    for t in range(A_Q_HEADS // 2):
        for e in range(2):
            hd = t + (A_Q_HEADS // 2) * e
            cols.extend(range(hd * HEAD_DIM, (hd + 1) * HEAD_DIM))
    return np.asarray(cols, np.int32)


def _layer(x, pos, layer_idx, norm1_g, w_in, a_q_norm, a_k_norm, a_sink, b_q_norm, b_k_norm, b_lam_q1,
           b_lam_k1, b_lam_q2, b_lam_k2, b_subln_g, c_q_norm, c_k_norm, d_cq_norm, d_ckv_norm, d_w_uq,
           d_w_ukv, d_q_norm, d_k_norm, w_branch, w_o, norm2_g, ffn_w_up, ffn_conv_w, ffn_conv_b,
           ffn_w_down):
    bsz, seq, d = x.shape
    t = bsz * seq
    x2d = x.reshape(t, d)
    ng = len(C_PATTERNS)
    qscale = HEAD_DIM ** -0.5

    n_a = A_Q_HEADS * HEAD_DIM + 2 * LANES
    n_b = 3 * 2 * B_HEADS * HEAD_DIM
    n_c = 3 * ng * C_HEADS * HEAD_DIM
    n_main = n_a + n_b + n_c
    perm = _pair_split_perm()
    w_main = jnp.concatenate([w_in[:, :A_Q_HEADS * HEAD_DIM][:, perm],
                              w_in[:, A_Q_HEADS * HEAD_DIM:n_main]], axis=1).astype(BF16)
    ones = lambda n: jnp.ones((n,), F32)
    zeros = lambda n: jnp.zeros((n,), F32)
    rep = lambda gvec, n: jnp.tile(gvec.astype(F32), n)
    gains = [rep(a_q_norm, 8) * qscale, rep(a_k_norm, 2), ones(128),
             rep(b_q_norm, 8) * qscale, rep(b_k_norm, 8), ones(512)]
    flags = [ones(512), ones(128), zeros(128), ones(512), ones(512), zeros(512)]
    for _ in range(ng):
        gains += [rep(c_q_norm, 8) * qscale, rep(c_k_norm, 8), ones(512)]
        flags += [ones(512), ones(512), zeros(512)]
    colgain = jnp.concatenate(gains)[None, :]
    normflag = jnp.concatenate(flags)[None, :]
    y_main, h2d = _in_proj(x2d, norm1_g[None, :].astype(F32), w_main, colgain, normflag)
    y3 = y_main.reshape(bsz, seq, n_main)

    sl_a = _alibi_slopes(A_Q_HEADS)
    half = A_Q_HEADS // 2
    slopes_a = jnp.asarray(np.stack([sl_a[:half], sl_a[half:]], axis=1), F32)
    sink_a = jnp.stack([a_sink[:half], a_sink[half:]], axis=1).astype(F32)
    (oa,) = _band_attention(y3, pos, A_RADIUS, slopes_a, sink_a, 0, 4, 5, 0, False)

    lam_init = 0.8 - 0.6 * math.exp(-0.3 * layer_idx)
    slopes_b = jnp.asarray(_alibi_slopes(B_HEADS), F32)
    sb = n_a // LANES
    ob = _flash_b(y3, pos, slopes_b, b_lam_q1.astype(F32), b_lam_k1.astype(F32), b_lam_q2.astype(F32),
                  b_lam_k2.astype(F32), b_subln_g[None, :].astype(F32), lam_init, sb, sb + 4, sb + 8)

    sl_c = _alibi_slopes(ng * C_HEADS).reshape(ng, C_HEADS // 2, 2)
    c_outs, c_lses = [], []
    for gi, (window, r) in enumerate(C_PATTERNS):
        radius = window // (2 * r)
        slab0 = (n_a + n_b) // LANES + gi * 12
        slopes_c = jnp.asarray(sl_c[gi], F32)
        if r == 1:
            o_g, lse_g = _band_attention(y3, pos, radius, slopes_c, None, slab0, slab0 + 4, slab0 + 8, 1, True)
        else:
            wc = 3 * C_HEADS * HEAD_DIM
            src = y3[:, :, slab0 * LANES:slab0 * LANES + wc]
            src = src.reshape(bsz, seq // r, r, wc).transpose(0, 2, 1, 3).reshape(bsz * r, seq // r, wc)
            pos_r = pos.reshape(bsz, seq // r, r).transpose(0, 2, 1).reshape(bsz * r, seq // r)
            o_g, lse_g = _band_attention(src, pos_r, radius, slopes_c, None, 0, 4, 8, 1, True)
            unstride = lambda a: a.reshape(bsz, r, seq // r, -1).transpose(0, 2, 1, 3).reshape(bsz, seq, -1)
            o_g, lse_g = unstride(o_g), unstride(lse_g)
        c_outs.append(o_g.reshape(t, -1))
        c_lses.append(lse_g.reshape(t, -1))
    oc = _c_merge(c_outs, c_lses)

    n0 = sum((512, 128, 128, 512, 512, 512, n_c))
    kr_cols = jnp.zeros((d, LANES), w_in.dtype).at[:, D_NOPE:D_NOPE + D_ROPE].set(
        w_in[:, n0 + D_Q_RANK + D_KV_RANK:n0 + D_Q_RANK + D_KV_RANK + D_ROPE])
    w_d = jnp.concatenate([w_in[:, n0:n0 + D_Q_RANK + D_KV_RANK], kr_cols], axis=1).astype(BF16)
    dq = D_NOPE + D_ROPE
    pad_heads = lambda wm, width: jnp.pad(wm.reshape(wm.shape[0], D_HEADS, width),
                                          ((0, 0), (0, 0), (0, LANES - width))).reshape(wm.shape[0], -1)
    wuq = pad_heads(d_w_uq, dq).astype(BF16)
    ukv = d_w_ukv.reshape(D_KV_RANK, D_HEADS, D_NOPE + D_V)
    wuk = pad_heads(ukv[:, :, :D_NOPE].reshape(D_KV_RANK, -1), D_NOPE).astype(BF16)
    wuv = pad_heads(ukv[:, :, D_NOPE:].reshape(D_KV_RANK, -1), D_V).astype(BF16)
    gq = jnp.pad(d_q_norm.astype(F32) * dq ** -0.5, (0, LANES - dq))[None, :]
    gk = jnp.pad(d_k_norm.astype(F32), (0, LANES - dq))[None, :]
    halfr = D_ROPE // 2
    inv = ROPE_THETA ** (-np.arange(halfr, dtype=np.float32) / halfr)
    inv_tab = np.zeros((1, LANES), np.float32)
    inv_tab[0, D_NOPE:D_NOPE + halfr] = inv
    inv_tab[0, D_NOPE + halfr:D_NOPE + D_ROPE] = inv
    pos_col = pos.astype(F32).reshape(t, 1)
    qd, kd, vd = _d_proj(h2d, w_d, d_cq_norm[None, :].astype(F32), d_ckv_norm[None, :].astype(F32),
                         wuq, wuk, wuv, gq, gk, pos_col, jnp.asarray(inv_tab))
    rs3 = lambda a: a.reshape(bsz, seq, -1)
    od = _flash_d(rs3(qd), rs3(kd), rs3(vd))

    n_gate0 = n0 + D_Q_RANK + D_KV_RANK + D_ROPE
    w_gate = w_in[:, n_gate0:].astype(BF16)
    wb = jnp.concatenate([w_branch[0][perm][None], w_branch[1:]], axis=0).astype(BF16)
    x1 = _merge(h2d, x2d, [oa.reshape(t, -1), ob.reshape(t, -1), oc, od.reshape(t, -1)],
                w_gate, wb, w_o.astype(BF16))

    x2 = _ffn(x1, seq, norm2_g[None, :].astype(F32), ffn_w_up.astype(BF16), ffn_conv_w.astype(F32),
              ffn_conv_b[None, :].astype(F32), ffn_w_down.astype(BF16))
    return x2.reshape(bsz, seq, d)


def kernel(x, positions, norm1_g, w_in, a_q_norm, a_k_norm, a_sink, b_q_norm, b_k_norm, b_lam_q1,
           b_lam_k1, b_lam_q2, b_lam_k2, b_subln_g, c_q_norm, c_k_norm, d_cq_norm, d_ckv_norm, d_w_uq,
           d_w_ukv, d_q_norm, d_k_norm, w_branch, w_o, norm2_g, ffn_w_up, ffn_conv_w, ffn_conv_b,
           ffn_w_down):
    depth = w_in.shape[0]
    per_layer = (norm1_g, w_in, a_q_norm, a_k_norm, a_sink, b_q_norm, b_k_norm, b_lam_q1, b_lam_k1,
                 b_lam_q2, b_lam_k2, b_subln_g, c_q_norm, c_k_norm, d_cq_norm, d_ckv_norm, d_w_uq,
                 d_w_ukv, d_q_norm, d_k_norm, w_branch, w_o, norm2_g, ffn_w_up, ffn_conv_w, ffn_conv_b,
                 ffn_w_down)
    for layer in range(depth):
        x = _layer(x, positions, layer, *[p[layer] for p in per_layer])
    return x
```

```python
import functools
import math

import jax
import jax.numpy as jnp
import numpy as np
from jax import lax
from jax.experimental import pallas as pl
from jax.experimental.pallas import tpu as pltpu

F32 = jnp.float32
BF16 = jnp.bfloat16

LANES = 128
HEAD_DIM = 64
A_Q_HEADS = 8
A_RADIUS = 128
B_HEADS = 4
C_PATTERNS = ((128, 1), (512, 4), (2048, 16))
C_HEADS = 8
D_HEADS = 8
D_Q_RANK = 384
D_KV_RANK = 256
D_NOPE = 64
D_ROPE = 32
D_V = 64
ROPE_THETA = 10000.0
N_BRANCH = 4
BRANCH_WIDTH = 512
EPS = 1e-6
NEG = -1e30
LOG2E = math.log2(math.e)
VMEM_LIMIT_BYTES = 56 * 1024 * 1024

NT_DIMS = (((1,), (1,)), ((), ()))


def _tile(n, pref):
    return pref if n % pref == 0 else n


def _params(*sem):
    return pltpu.CompilerParams(dimension_semantics=sem, vmem_limit_bytes=VMEM_LIMIT_BYTES)


def _in_proj_kernel(x_ref, g_ref, w_ref, cg_ref, nf_ref, ones_ref, y_ref, h_ref, hn_scr, *, tn):
    @pl.when(pl.program_id(1) == 0)
    def _():
        x = x_ref[...]
        ms = jnp.mean(x * x, axis=-1, keepdims=True)
        hn = (x * lax.rsqrt(ms + EPS) * g_ref[...]).astype(BF16)
        hn_scr[...] = hn
        h_ref[...] = hn

    y = jnp.dot(hn_scr[...], w_ref[...], preferred_element_type=F32)
    y2 = (y * y).astype(BF16)
    seg = jnp.concatenate(
        [jnp.dot(y2[:, c * 256:(c + 1) * 256], ones_ref[...], preferred_element_type=F32)
         for c in range(tn // 256)], axis=1)
    rs = lax.rsqrt(seg * (1.0 / HEAD_DIM) + EPS)
    scale = jnp.where(nf_ref[...] > 0.0, rs, 1.0) * cg_ref[...]
    y_ref[...] = (y * scale).astype(y_ref.dtype)


def _in_proj(x2d, g, w, colgain, normflag):
    t, d = x2d.shape
    n = w.shape[1]
    tm = _tile(t, 1024)
    tn = _tile(n, 768)
    seg_ones = jnp.asarray(np.kron(np.eye(256 // HEAD_DIM), np.ones((HEAD_DIM, HEAD_DIM))), BF16)
    return pl.pallas_call(
        functools.partial(_in_proj_kernel, tn=tn),
        grid=(t // tm, n // tn),
        in_specs=[
            pl.BlockSpec((tm, d), lambda i, j: (i, 0)),
            pl.BlockSpec((1, d), lambda i, j: (0, 0)),
            pl.BlockSpec((d, tn), lambda i, j: (0, j)),
            pl.BlockSpec((1, tn), lambda i, j: (0, j)),
            pl.BlockSpec((1, tn), lambda i, j: (0, j)),
            pl.BlockSpec((256, 256), lambda i, j: (0, 0)),
        ],
        out_specs=[
            pl.BlockSpec((tm, tn), lambda i, j: (i, j)),
            pl.BlockSpec((tm, d), lambda i, j: (i, 0)),
        ],
        out_shape=[jax.ShapeDtypeStruct((t, n), BF16), jax.ShapeDtypeStruct((t, d), BF16)],
        scratch_shapes=[pltpu.VMEM((tm, d), BF16)],
        compiler_params=_params("parallel", "arbitrary"),
        name="in_proj",
    )(x2d, g, w, colgain, normflag, seg_ones)


def _rope_slab(t, c_tab, s1_tab, s2_tab):
    return t * c_tab + pltpu.roll(t, LANES - D_ROPE // 2, 1) * s1_tab + pltpu.roll(t, D_ROPE // 2, 1) * s2_tab


def _d_proj_kernel(h_ref, wd_ref, gcq_ref, gckv_ref, wuq_ref, wuk_ref, wuv_ref, gq_ref, gk_ref,
                   pos_ref, inv_ref, q_ref, k_ref, v_ref):
    y = jnp.dot(h_ref[...], wd_ref[...], preferred_element_type=F32)
    cq = y[:, :D_Q_RANK]
    ckv = y[:, D_Q_RANK:D_Q_RANK + D_KV_RANK]
    kr = y[:, D_Q_RANK + D_KV_RANK:]
    cqn = (cq * lax.rsqrt(jnp.mean(cq * cq, axis=-1, keepdims=True) + EPS) * gcq_ref[...]).astype(BF16)
    ckvn = (ckv * lax.rsqrt(jnp.mean(ckv * ckv, axis=-1, keepdims=True) + EPS) * gckv_ref[...]).astype(BF16)
    q = jnp.dot(cqn, wuq_ref[...], preferred_element_type=F32)
    kn = jnp.dot(ckvn, wuk_ref[...], preferred_element_type=F32)
    v = jnp.dot(ckvn, wuv_ref[...], preferred_element_type=F32)

    lane = lax.broadcasted_iota(jnp.int32, (1, LANES), 1)
    ang = pos_ref[...] * inv_ref[...]
    cs, sn = jnp.cos(ang), jnp.sin(ang)
    half = D_ROPE // 2
    c_tab = jnp.where((lane >= D_NOPE) & (lane < D_NOPE + D_ROPE), cs, 1.0)
    s1_tab = jnp.where((lane >= D_NOPE) & (lane < D_NOPE + half), -sn, 0.0)
    s2_tab = jnp.where((lane >= D_NOPE + half) & (lane < D_NOPE + D_ROPE), sn, 0.0)
    inv_w = 1.0 / (D_NOPE + D_ROPE)
    for hd in range(D_HEADS):
        sl = slice(hd * LANES, (hd + 1) * LANES)
        qs = q[:, sl]
        qs = qs * lax.rsqrt(jnp.sum(qs * qs, axis=-1, keepdims=True) * inv_w + EPS) * gq_ref[...]
        q_ref[:, sl] = _rope_slab(qs, c_tab, s1_tab, s2_tab).astype(q_ref.dtype)
        ks = kn[:, sl] + kr
        ks = ks * lax.rsqrt(jnp.sum(ks * ks, axis=-1, keepdims=True) * inv_w + EPS) * gk_ref[...]
        k_ref[:, sl] = _rope_slab(ks, c_tab, s1_tab, s2_tab).astype(k_ref.dtype)
        v_ref[:, sl] = jnp.where(lane == D_V, 1.0, v[:, sl]).astype(v_ref.dtype)


def _d_proj(h2d, wd, gcq, gckv, wuq, wuk, wuv, gq, gk, pos_col, inv_tab):
    t, d = h2d.shape
    tm = _tile(t, 512)
    nd = D_HEADS * LANES
    full = lambda a: pl.BlockSpec(a.shape, lambda i: (0,) * a.ndim)
    out = jax.ShapeDtypeStruct((t, nd), BF16)
    return pl.pallas_call(
        _d_proj_kernel,
        grid=(t // tm,),
        in_specs=[pl.BlockSpec((tm, d), lambda i: (i, 0)), full(wd), full(gcq), full(gckv), full(wuq),
                  full(wuk), full(wuv), full(gq), full(gk), pl.BlockSpec((tm, 1), lambda i: (i, 0)),
                  full(inv_tab)],
        out_specs=[pl.BlockSpec((tm, nd), lambda i: (i, 0))] * 3,
        out_shape=[out, out, out],
        compiler_params=_params("parallel"),
        name="d_proj",
    )(h2d, wd, gcq, gckv, wuq, wuk, wuv, gq, gk, pos_col, inv_tab)


def _band_kernel(slope_ref, sink_ref, q_ref, kp_ref, kc_ref, kn_ref, vp_ref, vc_ref, vn_ref,
                 posq_ref, posw_ref, *out_refs, radius, tq, length, has_sink, want_lse):
    i = pl.program_id(1)
    t = pl.program_id(2)
    w = tq + 2 * radius
    q = q_ref[0]
    kw = jnp.concatenate([kp_ref[0], kc_ref[0], kn_ref[0]], axis=0)
    vw = jnp.concatenate([vp_ref[0], vc_ref[0], vn_ref[0]], axis=0)
    r_io = lax.broadcasted_iota(jnp.int32, (tq, w), 0)
    c_io = lax.broadcasted_iota(jnp.int32, (tq, w), 1)
    jabs = i * tq - radius + c_io
    mask = (jnp.abs(c_io - radius - r_io) <= radius) & (jabs >= 0) & (jabs < length)
    dist = jnp.abs(posq_ref[0] - posw_ref[0, 0])
    lane = lax.broadcasted_iota(jnp.int32, (tq, LANES), 1)
    outs, lses = [], []
    for e in range(2):
        in_half = (lane >= HEAD_DIM) if e else (lane < HEAD_DIM)
        qe = jnp.where(in_half, q, jnp.zeros_like(q))
        s = lax.dot_general(qe, kw, NT_DIMS, preferred_element_type=F32)
        s = s - slope_ref[t, e] * dist
        s = jnp.where(mask, s, NEG)
        m = jnp.max(s, axis=-1, keepdims=True)
        if has_sink:
            sk = sink_ref[t, e]
            m = jnp.maximum(m, sk)
        p = jnp.exp(s - m)
        den = jnp.sum(p, axis=-1, keepdims=True)
        if has_sink:
            den = den + jnp.exp(sk - m)
        o = jnp.dot(p.astype(BF16), vw, preferred_element_type=F32) / den
        outs.append(o)
        lses.append(m + jnp.log(den))
    out_refs[0][0] = jnp.where(lane < HEAD_DIM, outs[0], outs[1]).astype(out_refs[0].dtype)
    if want_lse:
        out_refs[1][0] = jnp.where(lane < HEAD_DIM, lses[0], lses[1])


def _band_attention(src, pos, radius, slopes, sink, q_slab0, k_slab0, v_slab0, kv_per_q, want_lse):
    bn, length, _ = src.shape
    nslab = 4
    tq = _tile(length, 256)
    nb = length // tq
    rb = tq // radius
    nrb = length // radius
    w = tq + 2 * radius
    idx = jnp.clip(jnp.arange(nb)[:, None] * tq - radius + jnp.arange(w)[None, :], 0, length - 1)
    posf = pos.astype(F32)
    posw = posf[:, idx][:, :, None, :]
    posq = posf[:, :, None]
    has_sink = sink is not None
    if sink is None:
        sink = jnp.zeros((nslab, 2), F32)

    def halo(slab0, which):
        if which == 0:
            return pl.BlockSpec((1, radius, LANES),
                                lambda b, i, t: (b, jnp.maximum(i * rb - 1, 0), slab0 + t * kv_per_q))
        if which == 1:
            return pl.BlockSpec((1, tq, LANES), lambda b, i, t: (b, i, slab0 + t * kv_per_q))
        return pl.BlockSpec((1, radius, LANES),
                            lambda b, i, t: (b, jnp.minimum((i + 1) * rb, nrb - 1), slab0 + t * kv_per_q))

    smem = pl.BlockSpec(memory_space=pltpu.SMEM)
    out_specs = [pl.BlockSpec((1, tq, LANES), lambda b, i, t: (b, i, t))]
    out_shape = [jax.ShapeDtypeStruct((bn, length, nslab * LANES), BF16)]
    if want_lse:
        out_specs.append(pl.BlockSpec((1, tq, LANES), lambda b, i, t: (b, i, t)))
        out_shape.append(jax.ShapeDtypeStruct((bn, length, nslab * LANES), F32))
    res = pl.pallas_call(
        functools.partial(_band_kernel, radius=radius, tq=tq, length=length, has_sink=has_sink,
                          want_lse=want_lse),
        grid=(bn, nb, nslab),
        in_specs=[smem, smem,
                  pl.BlockSpec((1, tq, LANES), lambda b, i, t: (b, i, q_slab0 + t)),
                  halo(k_slab0, 0), halo(k_slab0, 1), halo(k_slab0, 2),
                  halo(v_slab0, 0), halo(v_slab0, 1), halo(v_slab0, 2),
                  pl.BlockSpec((1, tq, 1), lambda b, i, t: (b, i, 0)),
                  pl.BlockSpec((1, 1, 1, w), lambda b, i, t: (b, i, 0, 0))],
        out_specs=out_specs,
        out_shape=out_shape,
        compiler_params=_params("parallel", "parallel", "arbitrary"),
        name="band_attn",
    )(slopes, sink, src, src, src, src, src, src, src, posq, posw)
    return res


def _c_merge_kernel(o0_ref, o1_ref, o2_ref, l0_ref, l1_ref, l2_ref, out_ref):
    l0, l1, l2 = l0_ref[...], l1_ref[...], l2_ref[...]
    m = jnp.maximum(jnp.maximum(l0, l1), l2)
    w0, w1, w2 = jnp.exp(l0 - m), jnp.exp(l1 - m), jnp.exp(l2 - m)
    num = w0 * o0_ref[...].astype(F32) + w1 * o1_ref[...].astype(F32) + w2 * o2_ref[...].astype(F32)
    out_ref[...] = (num / (w0 + w1 + w2)).astype(out_ref.dtype)


def _c_merge(outs, lses):
    t, n = outs[0].shape
    tm = _tile(t, 1024)
    spec = pl.BlockSpec((tm, n), lambda i: (i, 0))
    return pl.pallas_call(
        _c_merge_kernel,
        grid=(t // tm,),
        in_specs=[spec] * 6,
        out_specs=spec,
        out_shape=jax.ShapeDtypeStruct((t, n), BF16),
        compiler_params=_params("parallel"),
        name="c_merge",
    )(*outs, *lses)


def _flash_step(units, m_refs, acc_refs):
    scores = []
    for q, k_c, _, bias in units:
        s = lax.dot_general(k_c, q, NT_DIMS, preferred_element_type=F32)
        scores.append(s if bias is None else s - bias)
    for u, (s, (_, _, vt_c, _)) in enumerate(zip(scores, units)):
        m_old = m_refs[u][...]
        m_new = jnp.maximum(m_old, jnp.max(s, axis=0, keepdims=True))
        alpha = jnp.exp2(m_old - m_new)
        p = jnp.exp2(s - m_new).astype(BF16)
        acc_refs[u][...] = alpha * acc_refs[u][...] + jnp.dot(vt_c, p, preferred_element_type=F32)
        m_refs[u][...] = m_new


def _flash_init(m_refs, acc_refs):
    for m_ref, acc_ref in zip(m_refs, acc_refs):
        m_ref[...] = jnp.full(m_ref.shape, NEG, F32)
        acc_ref[...] = jnp.zeros(acc_ref.shape, F32)


B_VT_ROWS = LANES + 16


def _flash_b_kernel(slope_ref, q0_ref, q1_ref, k0_ref, k1_ref, vta_ref, vtb_ref, posq_ref, posk_ref,
                    lq1_ref, lk1_ref, lq2_ref, lk2_ref, subg_ref, o_ref, *scratch, tq, tk, seq, lam_init):
    hp = pl.program_id(1)
    m_refs, acc_refs = scratch[:4], scratch[4:]
    _flash_init(m_refs, acc_refs)
    lane = lax.broadcasted_iota(jnp.int32, (tq, LANES), 1)
    q_maps = (q0_ref[0], q1_ref[0])
    k_refs = (k0_ref, k1_ref)
    vt_refs = (vta_ref, vtb_ref)
    qs = [[jnp.where((lane >= HEAD_DIM) if e else (lane < HEAD_DIM), qm, jnp.zeros_like(qm))
           for e in range(2)] for qm in q_maps]
    posq = posq_ref[0]

    def body(j, carry):
        off = pl.multiple_of(j * tk, tk)
        posk = pltpu.repeat(posk_ref[0, pl.ds(off, tk), :], tq // LANES, axis=1)
        dist = jnp.abs(posk - posq)
        bias = [slope_ref[2 * hp + e] * dist for e in range(2)]
        k_cs = [k_refs[mp][0, pl.ds(off, tk), :] for mp in range(2)]
        vt_cs = [vt_refs[e][0, 0, :, pl.ds(off, tk)] for e in range(2)]
        _flash_step([(qs[mp][e], k_cs[mp], vt_cs[e], bias[e]) for mp in range(2) for e in range(2)],
                    m_refs, acc_refs)
        return carry

    lax.fori_loop(0, seq // tk, body, 0)

    def normalised(u):
        acc = acc_refs[u][...]
        return acc[:LANES] / acc[LANES:LANES + 1]

    for e in range(2):
        hd = 2 * hp + e
        lam1 = jnp.sum(lq1_ref[pl.ds(hd, 1), :] * lk1_ref[pl.ds(hd, 1), :], axis=-1, keepdims=True)
        lam2 = jnp.sum(lq2_ref[pl.ds(hd, 1), :] * lk2_ref[pl.ds(hd, 1), :], axis=-1, keepdims=True)
        lam = jnp.exp(lam1) - jnp.exp(lam2) + lam_init
        a = (normalised(e) - lam * normalised(2 + e)).T
        ms = jnp.mean(a * a, axis=-1, keepdims=True)
        out = a * lax.rsqrt(ms + EPS) * subg_ref[...] * (1.0 - lam_init)
        o_ref[0, :, e * LANES:(e + 1) * LANES] = out.astype(o_ref.dtype)


def _flash_b(y_main, pos, slopes, lq1, lk1, lq2, lk2, subg, lam_init, q_slab0, k_slab0, v_slab0):
    bsz, seq, _ = y_main.shape
    tq = _tile(seq, 512)
    tk = _tile(seq, 512)
    posf = pos.astype(F32)
    posq = posf[:, None, :]
    posk = jnp.broadcast_to(posf[:, :, None], (bsz, seq, LANES))
    v = y_main[:, :, v_slab0 * LANES:(v_slab0 + B_HEADS) * LANES].reshape(bsz, seq, B_HEADS, LANES)
    vt = jnp.concatenate([v.transpose(0, 2, 3, 1), jnp.ones((bsz, B_HEADS, 1, seq), BF16),
                          jnp.zeros((bsz, B_HEADS, B_VT_ROWS - LANES - 1, seq), BF16)], axis=2)
    k_spec = lambda slab_fn: pl.BlockSpec((1, seq, LANES), lambda b, hp, i: (b, 0, slab_fn(hp)))
    vt_spec = lambda e: pl.BlockSpec((1, 1, B_VT_ROWS, seq), lambda b, hp, i: (b, 2 * hp + e, 0, 0))
    small = lambda a: pl.BlockSpec(a.shape, lambda b, hp, i: (0,) * a.ndim)
    return pl.pallas_call(
        functools.partial(_flash_b_kernel, tq=tq, tk=tk, seq=seq, lam_init=lam_init),
        grid=(bsz, B_HEADS // 2, seq // tq),
        in_specs=[pl.BlockSpec(memory_space=pltpu.SMEM),
                  pl.BlockSpec((1, tq, LANES), lambda b, hp, i: (b, i, q_slab0 + hp)),
                  pl.BlockSpec((1, tq, LANES), lambda b, hp, i: (b, i, q_slab0 + 2 + hp)),
                  k_spec(lambda hp: k_slab0 + hp), k_spec(lambda hp: k_slab0 + 2 + hp),
                  vt_spec(0), vt_spec(1),
                  pl.BlockSpec((1, 1, tq), lambda b, hp, i: (b, 0, i)),
                  pl.BlockSpec((1, seq, LANES), lambda b, hp, i: (b, 0, 0)),
                  small(lq1), small(lk1), small(lq2), small(lk2), small(subg)],
        out_specs=pl.BlockSpec((1, tq, 2 * LANES), lambda b, hp, i: (b, i, hp)),
        out_shape=jax.ShapeDtypeStruct((bsz, seq, B_HEADS * LANES), BF16),
        scratch_shapes=[pltpu.VMEM((1, tq), F32)] * 4 + [pltpu.VMEM((B_VT_ROWS, tq), F32)] * 4,
        compiler_params=_params("parallel", "parallel", "arbitrary"),
        name="flash_diff",
    )(slopes, y_main, y_main, y_main, y_main, vt, vt, posq, posk, lq1, lk1, lq2, lk2, subg)


D_GROUP = 4


def _flash_d_kernel(*refs, tq, tk, seq):
    n = D_GROUP
    q_refs, k_refs, vt_refs = refs[:n], refs[n:2 * n], refs[2 * n:3 * n]
    o_ref = refs[3 * n]
    scratch = refs[3 * n + 1:]
    m_refs, acc_refs = scratch[:n], scratch[n:]
    _flash_init(m_refs, acc_refs)
    qs = [q_ref[0] for q_ref in q_refs]

    def body(j, carry):
        off = pl.multiple_of(j * tk, tk)
        _flash_step([(qs[e], k_refs[e][0, pl.ds(off, tk), :], vt_refs[e][0, :, pl.ds(off, tk)], None)
                     for e in range(n)], m_refs, acc_refs)
        return carry

    lax.fori_loop(0, seq // tk, body, 0)
    outs = []
    for e in range(n):
        acc = acc_refs[e][...]
        outs.append(acc[:D_V] / acc[D_V:D_V + 1])
    o_ref[0] = jnp.concatenate(outs, axis=0).T.astype(o_ref.dtype)


def _flash_d(qd, kd, vd):
    bsz, seq, _ = qd.shape
    tq = _tile(seq, 512)
    tk = _tile(seq, 512)
    n = D_GROUP
    vt = vd.transpose(0, 2, 1)
    q_spec = lambda e: pl.BlockSpec((1, tq, LANES), lambda b, hg, i: (b, i, n * hg + e))
    k_spec = lambda e: pl.BlockSpec((1, seq, LANES), lambda b, hg, i: (b, 0, n * hg + e))
    vt_spec = lambda e: pl.BlockSpec((1, LANES, seq), lambda b, hg, i: (b, n * hg + e, 0))
    return pl.pallas_call(
        functools.partial(_flash_d_kernel, tq=tq, tk=tk, seq=seq),
        grid=(bsz, D_HEADS // n, seq // tq),
        in_specs=[q_spec(e) for e in range(n)] + [k_spec(e) for e in range(n)]
                 + [vt_spec(e) for e in range(n)],
        out_specs=pl.BlockSpec((1, tq, n * D_V), lambda b, hg, i: (b, i, hg)),
        out_shape=jax.ShapeDtypeStruct((bsz, seq, D_HEADS * D_V), BF16),
        scratch_shapes=[pltpu.VMEM((1, tq), F32)] * n + [pltpu.VMEM((LANES, tq), F32)] * n,
        compiler_params=_params("parallel", "parallel", "arbitrary"),
        name="flash_latent",
    )(*([qd] * n + [kd] * n + [vt] * n))


def _merge_kernel(h_ref, x_ref, oa_ref, ob_ref, oc_ref, od_ref, wg_ref, wb_ref, wo_ref, out_ref, acc_scr):
    j = pl.program_id(1)

    @pl.when(j == 0)
    def _():
        acc_scr[...] = jnp.zeros(acc_scr.shape, F32)

    gate = jax.nn.sigmoid(jnp.dot(h_ref[...], wg_ref[...], preferred_element_type=F32))
    for br, o_ref in enumerate((oa_ref, ob_ref, oc_ref, od_ref)):
        @pl.when(j == br)
        def _(o_ref=o_ref):
            acc_scr[...] += gate * jnp.dot(o_ref[...], wb_ref[0], preferred_element_type=F32)

    @pl.when(j == N_BRANCH - 1)
    def _():
        out_ref[...] = x_ref[...] + jnp.dot(acc_scr[...].astype(BF16), wo_ref[...],
                                            preferred_element_type=F32)


def _merge(h2d, x2d, branch_outs, wg, wb, wo):
    t, d = x2d.shape
    tm = _tile(t, 512)
    bw = wb.shape[1]
    row = lambda width: pl.BlockSpec((tm, width), lambda i, j: (i, 0))
    return pl.pallas_call(
        _merge_kernel,
        grid=(t // tm, N_BRANCH),
        in_specs=[row(d), row(d), row(bw), row(bw), row(bw), row(bw),
                  pl.BlockSpec((d, d), lambda i, j: (0, j)),
                  pl.BlockSpec((1, bw, d), lambda i, j: (j, 0, 0)),
                  pl.BlockSpec((d, d), lambda i, j: (0, 0))],
        out_specs=row(d),
        out_shape=jax.ShapeDtypeStruct((t, d), F32),
        scratch_shapes=[pltpu.VMEM((tm, d), F32)],
        compiler_params=_params("parallel", "arbitrary"),
        name="gated_merge",
    )(h2d, x2d, *branch_outs, wg, wb, wo)


HALO = 8


def _ffn_kernel(x_ref, xp_ref, xn_ref, g_ref, wg_ref, wu_ref, cwg_ref, cwu_ref, cbg_ref, cbu_ref,
                wd_ref, out_ref, hn_scr, acc_scr, *, tm, tiles_per_seq):
    i = pl.program_id(0)
    j = pl.program_id(1)
    rows = tm + 2 * HALO

    @pl.when(j == 0)
    def _():
        xe = jnp.concatenate([xp_ref[...], x_ref[...], xn_ref[...]], axis=0)
        ms = jnp.mean(xe * xe, axis=-1, keepdims=True)
        hn = xe * lax.rsqrt(ms + EPS) * g_ref[...]
        r = lax.broadcasted_iota(jnp.int32, (rows, 1), 0)
        first = (i % tiles_per_seq) == 0
        last = (i % tiles_per_seq) == tiles_per_seq - 1
        dead = ((r == HALO - 1) & first) | ((r == HALO + tm) & last)
        hn_scr[...] = jnp.where(dead, 0.0, hn).astype(BF16)
        acc_scr[...] = jnp.zeros(acc_scr.shape, F32)

    def conv(w_ref, cw_ref, cb_ref):
        u = jnp.dot(hn_scr[...], w_ref[...], preferred_element_type=F32)
        y = (cw_ref[0:1, :] * pltpu.roll(u, 1, 0) + cw_ref[1:2, :] * u
             + cw_ref[2:3, :] * pltpu.roll(u, rows - 1, 0))
        return y[HALO:HALO + tm, :] + cb_ref[...]

    yg = conv(wg_ref, cwg_ref, cbg_ref)
    yu = conv(wu_ref, cwu_ref, cbu_ref)
    act = (jax.nn.silu(yg) * yu).astype(BF16)
    acc_scr[...] += jnp.dot(act, wd_ref[...], preferred_element_type=F32)

    @pl.when(j == pl.num_programs(1) - 1)
    def _():
        out_ref[...] = x_ref[...] + acc_scr[...]


def _ffn(x2d, seq, g, w_up, conv_w, conv_b, w_down):
    t, d = x2d.shape
    dff = w_down.shape[0]
    tm = _tile(seq, 512)
    tf = _tile(dff, 256)
    nf = dff // tf
    nhb = t // HALO
    rb = tm // HALO
    return pl.pallas_call(
        functools.partial(_ffn_kernel, tm=tm, tiles_per_seq=seq // tm),
        grid=(t // tm, nf),
        in_specs=[pl.BlockSpec((tm, d), lambda i, j: (i, 0)),
                  pl.BlockSpec((HALO, d), lambda i, j: (jnp.maximum(i * rb - 1, 0), 0)),
                  pl.BlockSpec((HALO, d), lambda i, j: (jnp.minimum((i + 1) * rb, nhb - 1), 0)),
                  pl.BlockSpec((1, d), lambda i, j: (0, 0)),
                  pl.BlockSpec((d, tf), lambda i, j: (0, j)),
                  pl.BlockSpec((d, tf), lambda i, j: (0, nf + j)),
                  pl.BlockSpec((3, tf), lambda i, j: (0, j)),
                  pl.BlockSpec((3, tf), lambda i, j: (0, nf + j)),
                  pl.BlockSpec((1, tf), lambda i, j: (0, j)),
                  pl.BlockSpec((1, tf), lambda i, j: (0, nf + j)),
                  pl.BlockSpec((tf, d), lambda i, j: (j, 0))],
        out_specs=pl.BlockSpec((tm, d), lambda i, j: (i, 0)),
        out_shape=jax.ShapeDtypeStruct((t, d), F32),
        scratch_shapes=[pltpu.VMEM((tm + 2 * HALO, d), BF16), pltpu.VMEM((tm, d), F32)],
        compiler_params=_params("parallel", "arbitrary"),
        name="conv_mlp",
    )(x2d, x2d, x2d, g, w_up, w_up, conv_w, conv_w, conv_b, conv_b, w_down)


def _alibi_slopes(n):
    return 2.0 ** (-8.0 * np.arange(1, n + 1) / n)


def _pair_split_perm():
    cols = []
    for t in range(A_Q_HEADS // 2):
        for e in range(2):
            hd = t + (A_Q_HEADS // 2) * e
            cols.extend(range(hd * HEAD_DIM, (hd + 1) * HEAD_DIM))
    return np.asarray(cols, np.int32)


def _layer(x, pos, layer_idx, norm1_g, w_in, a_q_norm, a_k_norm, a_sink, b_q_norm, b_k_norm, b_lam_q1,
           b_lam_k1, b_lam_q2, b_lam_k2, b_subln_g, c_q_norm, c_k_norm, d_cq_norm, d_ckv_norm, d_w_uq,
           d_w_ukv, d_q_norm, d_k_norm, w_branch, w_o, norm2_g, ffn_w_up, ffn_conv_w, ffn_conv_b,
           ffn_w_down):
    bsz, seq, d = x.shape
    t = bsz * seq
    x2d = x.reshape(t, d)
    ng = len(C_PATTERNS)
    qscale = HEAD_DIM ** -0.5

    n_a = A_Q_HEADS * HEAD_DIM + 2 * LANES
    n_b = 3 * 2 * B_HEADS * HEAD_DIM
    n_c = 3 * ng * C_HEADS * HEAD_DIM
    n_main = n_a + n_b + n_c
    perm = _pair_split_perm()
    w_main = jnp.concatenate([w_in[:, :A_Q_HEADS * HEAD_DIM][:, perm],
                              w_in[:, A_Q_HEADS * HEAD_DIM:n_main]], axis=1).astype(BF16)
    ones = lambda n: jnp.ones((n,), F32)
    zeros = lambda n: jnp.zeros((n,), F32)
    rep = lambda gvec, n: jnp.tile(gvec.astype(F32), n)
    gains = [rep(a_q_norm, 8) * qscale, rep(a_k_norm, 2), ones(128),
             rep(b_q_norm, 8) * (qscale * LOG2E), rep(b_k_norm, 8), ones(512)]
    flags = [ones(512), ones(128), zeros(128), ones(512), ones(512), zeros(512)]
    for _ in range(ng):
        gains += [rep(c_q_norm, 8) * qscale, rep(c_k_norm, 8), ones(512)]
        flags += [ones(512), ones(512), zeros(512)]
    colgain = jnp.concatenate(gains)[None, :]
    normflag = jnp.concatenate(flags)[None, :]
    y_main, h2d = _in_proj(x2d, norm1_g[None, :].astype(F32), w_main, colgain, normflag)
    y3 = y_main.reshape(bsz, seq, n_main)

    sl_a = _alibi_slopes(A_Q_HEADS)
    half = A_Q_HEADS // 2
    slopes_a = jnp.asarray(np.stack([sl_a[:half], sl_a[half:]], axis=1), F32)
    sink_a = jnp.stack([a_sink[:half], a_sink[half:]], axis=1).astype(F32)
    (oa,) = _band_attention(y3, pos, A_RADIUS, slopes_a, sink_a, 0, 4, 5, 0, False)

    lam_init = 0.8 - 0.6 * math.exp(-0.3 * layer_idx)
    slopes_b = jnp.asarray(_alibi_slopes(B_HEADS) * LOG2E, F32)
    sb = n_a // LANES
    ob = _flash_b(y3, pos, slopes_b, b_lam_q1.astype(F32), b_lam_k1.astype(F32), b_lam_q2.astype(F32),
                  b_lam_k2.astype(F32), b_subln_g[None, :].astype(F32), lam_init, sb, sb + 4, sb + 8)

    sl_c = _alibi_slopes(ng * C_HEADS).reshape(ng, C_HEADS // 2, 2)
    c_outs, c_lses = [], []
    for gi, (window, r) in enumerate(C_PATTERNS):
        radius = window // (2 * r)
        slab0 = (n_a + n_b) // LANES + gi * 12
        slopes_c = jnp.asarray(sl_c[gi], F32)
        if r == 1:
            o_g, lse_g = _band_attention(y3, pos, radius, slopes_c, None, slab0, slab0 + 4, slab0 + 8, 1, True)
        else:
            wc = 3 * C_HEADS * HEAD_DIM
            src = y3[:, :, slab0 * LANES:slab0 * LANES + wc]
            src = src.reshape(bsz, seq // r, r, wc).transpose(0, 2, 1, 3).reshape(bsz * r, seq // r, wc)
            pos_r = pos.reshape(bsz, seq // r, r).transpose(0, 2, 1).reshape(bsz * r, seq // r)
            o_g, lse_g = _band_attention(src, pos_r, radius, slopes_c, None, 0, 4, 8, 1, True)
            unstride = lambda a: a.reshape(bsz, r, seq // r, -1).transpose(0, 2, 1, 3).reshape(bsz, seq, -1)
            o_g, lse_g = unstride(o_g), unstride(lse_g)
        c_outs.append(o_g.reshape(t, -1))
        c_lses.append(lse_g.reshape(t, -1))
    oc = _c_merge(c_outs, c_lses)

    n0 = sum((512, 128, 128, 512, 512, 512, n_c))
    kr_cols = jnp.zeros((d, LANES), w_in.dtype).at[:, D_NOPE:D_NOPE + D_ROPE].set(
        w_in[:, n0 + D_Q_RANK + D_KV_RANK:n0 + D_Q_RANK + D_KV_RANK + D_ROPE])
    w_d = jnp.concatenate([w_in[:, n0:n0 + D_Q_RANK + D_KV_RANK], kr_cols], axis=1).astype(BF16)
    dq = D_NOPE + D_ROPE
    pad_heads = lambda wm, width: jnp.pad(wm.reshape(wm.shape[0], D_HEADS, width),
                                          ((0, 0), (0, 0), (0, LANES - width))).reshape(wm.shape[0], -1)
    wuq = pad_heads(d_w_uq, dq).astype(BF16)
    ukv = d_w_ukv.reshape(D_KV_RANK, D_HEADS, D_NOPE + D_V)
    wuk = pad_heads(ukv[:, :, :D_NOPE].reshape(D_KV_RANK, -1), D_NOPE).astype(BF16)
    wuv = pad_heads(ukv[:, :, D_NOPE:].reshape(D_KV_RANK, -1), D_V).astype(BF16)
    gq = jnp.pad(d_q_norm.astype(F32) * (dq ** -0.5 * LOG2E), (0, LANES - dq))[None, :]
    gk = jnp.pad(d_k_norm.astype(F32), (0, LANES - dq))[None, :]
    halfr = D_ROPE // 2
    inv = ROPE_THETA ** (-np.arange(halfr, dtype=np.float32) / halfr)
    inv_tab = np.zeros((1, LANES), np.float32)
    inv_tab[0, D_NOPE:D_NOPE + halfr] = inv
    inv_tab[0, D_NOPE + halfr:D_NOPE + D_ROPE] = inv
    pos_col = pos.astype(F32).reshape(t, 1)
    qd, kd, vd = _d_proj(h2d, w_d, d_cq_norm[None, :].astype(F32), d_ckv_norm[None, :].astype(F32),
                         wuq, wuk, wuv, gq, gk, pos_col, jnp.asarray(inv_tab))
    rs3 = lambda a: a.reshape(bsz, seq, -1)
    od = _flash_d(rs3(qd), rs3(kd), rs3(vd))

    n_gate0 = n0 + D_Q_RANK + D_KV_RANK + D_ROPE
    w_gate = w_in[:, n_gate0:].astype(BF16)
    wb = jnp.concatenate([w_branch[0][perm][None], w_branch[1:]], axis=0).astype(BF16)
    x1 = _merge(h2d, x2d, [oa.reshape(t, -1), ob.reshape(t, -1), oc, od.reshape(t, -1)],
                w_gate, wb, w_o.astype(BF16))

    x2 = _ffn(x1, seq, norm2_g[None, :].astype(F32), ffn_w_up.astype(BF16), ffn_conv_w.astype(F32),
              ffn_conv_b[None, :].astype(F32), ffn_w_down.astype(BF16))
    return x2.reshape(bsz, seq, d)


def kernel(x, positions, norm1_g, w_in, a_q_norm, a_k_norm, a_sink, b_q_norm, b_k_norm, b_lam_q1,
           b_lam_k1, b_lam_q2, b_lam_k2, b_subln_g, c_q_norm, c_k_norm, d_cq_norm, d_ckv_norm, d_w_uq,
           d_w_ukv, d_q_norm, d_k_norm, w_branch, w_o, norm2_g, ffn_w_up, ffn_conv_w, ffn_conv_b,
           ffn_w_down):
    depth = w_in.shape[0]
    per_layer = (norm1_g, w_in, a_q_norm, a_k_norm, a_sink, b_q_norm, b_k_norm, b_lam_q1, b_lam_k1,
                 b_lam_q2, b_lam_k2, b_subln_g, c_q_norm, c_k_norm, d_cq_norm, d_ckv_norm, d_w_uq,
                 d_w_ukv, d_q_norm, d_k_norm, w_branch, w_o, norm2_g, ffn_w_up, ffn_conv_w, ffn_conv_b,
                 ffn_w_down)
    for layer in range(depth):
        x = _layer(x, positions, layer, *[p[layer] for p in per_layer])
    return x
```

```python
import functools
import math

import jax
import jax.numpy as jnp
import numpy as np
from jax import lax
from jax.experimental import pallas as pl
from jax.experimental.pallas import tpu as pltpu

F32 = jnp.float32
BF16 = jnp.bfloat16

LANES = 128
HEAD_DIM = 64
A_Q_HEADS = 8
A_RADIUS = 128
B_HEADS = 4
C_PATTERNS = ((128, 1), (512, 4), (2048, 16))
C_HEADS = 8
D_HEADS = 8
D_Q_RANK = 384
D_KV_RANK = 256
D_NOPE = 64
D_ROPE = 32
D_V = 64
ROPE_THETA = 10000.0
N_BRANCH = 4
BRANCH_WIDTH = 512
EPS = 1e-6
NEG = -1e30
LOG2E = math.log2(math.e)
VMEM_LIMIT_BYTES = 56 * 1024 * 1024

NT_DIMS = (((1,), (1,)), ((), ()))
TN_DIMS = (((0,), (0,)), ((), ()))


def _tile(n, pref):
    return pref if n % pref == 0 else n


def _params(*sem):
    return pltpu.CompilerParams(dimension_semantics=sem, vmem_limit_bytes=VMEM_LIMIT_BYTES)


def _in_proj_kernel(x_ref, g_ref, w_ref, cg_ref, nf_ref, ones_ref, y_ref, h_ref, hn_scr, *, tn):
    @pl.when(pl.program_id(1) == 0)
    def _():
        x = x_ref[...]
        ms = jnp.mean(x * x, axis=-1, keepdims=True)
        hn = (x * lax.rsqrt(ms + EPS) * g_ref[...]).astype(BF16)
        hn_scr[...] = hn
        h_ref[...] = hn

    y = jnp.dot(hn_scr[...], w_ref[...], preferred_element_type=F32)
    y2 = (y * y).astype(BF16)
    seg = jnp.concatenate(
        [jnp.dot(y2[:, c * 256:(c + 1) * 256], ones_ref[...], preferred_element_type=F32)
         for c in range(tn // 256)], axis=1)
    rs = lax.rsqrt(seg * (1.0 / HEAD_DIM) + EPS)
    scale = jnp.where(nf_ref[...] > 0.0, rs, 1.0) * cg_ref[...]
    y_ref[...] = (y * scale).astype(y_ref.dtype)


def _in_proj(x2d, g, w, colgain, normflag):
    t, d = x2d.shape
    n = w.shape[1]
    tm = _tile(t, 1024)
    tn = _tile(n, 1024)
    seg_ones = jnp.asarray(np.kron(np.eye(256 // HEAD_DIM), np.ones((HEAD_DIM, HEAD_DIM))), BF16)
    return pl.pallas_call(
        functools.partial(_in_proj_kernel, tn=tn),
        grid=(t // tm, n // tn),
        in_specs=[
            pl.BlockSpec((tm, d), lambda i, j: (i, 0)),
            pl.BlockSpec((1, d), lambda i, j: (0, 0)),
            pl.BlockSpec((d, tn), lambda i, j: (0, j)),
            pl.BlockSpec((1, tn), lambda i, j: (0, j)),
            pl.BlockSpec((1, tn), lambda i, j: (0, j)),
            pl.BlockSpec((256, 256), lambda i, j: (0, 0)),
        ],
        out_specs=[
            pl.BlockSpec((tm, tn), lambda i, j: (i, j)),
            pl.BlockSpec((tm, d), lambda i, j: (i, 0)),
        ],
        out_shape=[jax.ShapeDtypeStruct((t, n), BF16), jax.ShapeDtypeStruct((t, d), BF16)],
        scratch_shapes=[pltpu.VMEM((tm, d), BF16)],
        compiler_params=_params("parallel", "arbitrary"),
        name="in_proj",
    )(x2d, g, w, colgain, normflag, seg_ones)


def _rope_slab(t, c_tab, s1_tab, s2_tab):
    return t * c_tab + pltpu.roll(t, LANES - D_ROPE // 2, 1) * s1_tab + pltpu.roll(t, D_ROPE // 2, 1) * s2_tab


def _d_proj_kernel(h_ref, wd_ref, gcq_ref, gckv_ref, wuq_ref, wuk_ref, wuv_ref, gq_ref, gk_ref,
                   pos_ref, inv_ref, q_ref, k_ref, v_ref):
    y = jnp.dot(h_ref[...], wd_ref[...], preferred_element_type=F32)
    cq = y[:, :D_Q_RANK]
    ckv = y[:, D_Q_RANK:D_Q_RANK + D_KV_RANK]
    kr = y[:, D_Q_RANK + D_KV_RANK:]
    cqn = (cq * lax.rsqrt(jnp.mean(cq * cq, axis=-1, keepdims=True) + EPS) * gcq_ref[...]).astype(BF16)
    ckvn = (ckv * lax.rsqrt(jnp.mean(ckv * ckv, axis=-1, keepdims=True) + EPS) * gckv_ref[...]).astype(BF16)
    q = jnp.dot(cqn, wuq_ref[...], preferred_element_type=F32)
    kn = jnp.dot(ckvn, wuk_ref[...], preferred_element_type=F32)
    v = jnp.dot(ckvn, wuv_ref[...], preferred_element_type=F32)

    lane = lax.broadcasted_iota(jnp.int32, (1, LANES), 1)
    ang = pos_ref[...] * inv_ref[...]
    cs, sn = jnp.cos(ang), jnp.sin(ang)
    half = D_ROPE // 2
    c_tab = jnp.where((lane >= D_NOPE) & (lane < D_NOPE + D_ROPE), cs, 1.0)
    s1_tab = jnp.where((lane >= D_NOPE) & (lane < D_NOPE + half), -sn, 0.0)
    s2_tab = jnp.where((lane >= D_NOPE + half) & (lane < D_NOPE + D_ROPE), sn, 0.0)
    inv_w = 1.0 / (D_NOPE + D_ROPE)
    for hd in range(D_HEADS):
        sl = slice(hd * LANES, (hd + 1) * LANES)
        qs = q[:, sl]
        qs = qs * lax.rsqrt(jnp.sum(qs * qs, axis=-1, keepdims=True) * inv_w + EPS) * gq_ref[...]
        q_ref[:, sl] = _rope_slab(qs, c_tab, s1_tab, s2_tab).astype(q_ref.dtype)
        ks = kn[:, sl] + kr
        ks = ks * lax.rsqrt(jnp.sum(ks * ks, axis=-1, keepdims=True) * inv_w + EPS) * gk_ref[...]
        k_ref[:, sl] = _rope_slab(ks, c_tab, s1_tab, s2_tab).astype(k_ref.dtype)
        v_ref[:, sl] = jnp.where(lane == D_V, 1.0, v[:, sl]).astype(v_ref.dtype)


def _d_proj(h2d, wd, gcq, gckv, wuq, wuk, wuv, gq, gk, pos_col, inv_tab):
    t, d = h2d.shape
    tm = _tile(t, 512)
    nd = D_HEADS * LANES
    full = lambda a: pl.BlockSpec(a.shape, lambda i: (0,) * a.ndim)
    out = jax.ShapeDtypeStruct((t, nd), BF16)
    return pl.pallas_call(
        _d_proj_kernel,
        grid=(t // tm,),
        in_specs=[pl.BlockSpec((tm, d), lambda i: (i, 0)), full(wd), full(gcq), full(gckv), full(wuq),
                  full(wuk), full(wuv), full(gq), full(gk), pl.BlockSpec((tm, 1), lambda i: (i, 0)),
                  full(inv_tab)],
        out_specs=[pl.BlockSpec((tm, nd), lambda i: (i, 0))] * 3,
        out_shape=[out, out, out],
        compiler_params=_params("parallel"),
        name="d_proj",
    )(h2d, wd, gcq, gckv, wuq, wuk, wuv, gq, gk, pos_col, inv_tab)


N_QSLAB = 4
BAND_GQ = 128


def _band_kernel(slope_ref, sink_ref, q_ref, kp_ref, kc_ref, kn_ref, vp_ref, vc_ref, vn_ref,
                 posq_ref, posw_ref, *out_refs, radius, tq, gq, length, nkv, has_sink, want_lse):
    i = pl.program_id(1)
    kw = jnp.concatenate([kp_ref[0], kc_ref[0], kn_ref[0]], axis=0)
    vw = jnp.concatenate([vp_ref[0], vc_ref[0], vn_ref[0]], axis=0)
    wg = gq + 2 * radius
    c_io = lax.broadcasted_iota(jnp.int32, (wg, gq), 0)
    r_io = lax.broadcasted_iota(jnp.int32, (wg, gq), 1)
    in_band = jnp.abs(c_io - radius - r_io) <= radius
    lane = lax.broadcasted_iota(jnp.int32, (gq, LANES), 1)
    row = lax.broadcasted_iota(jnp.int32, (LANES, gq), 0)
    for g in range(tq // gq):
        rows = slice(g * gq, g * gq + wg)
        cols = slice(g * gq, (g + 1) * gq)
        jabs = i * tq + g * gq - radius + c_io
        mask = in_band & (jabs >= 0) & (jabs < length)
        dist = jnp.abs(posw_ref[0, 0, rows, :] - posq_ref[0, :, cols])
        for t in range(N_QSLAB):
            q = q_ref[0, cols, t * LANES:(t + 1) * LANES]
            kt = t if nkv == N_QSLAB else 0
            ks = kw[rows, kt * LANES:(kt + 1) * LANES]
            vs = vw[rows, kt * LANES:(kt + 1) * LANES]
            outs, lses = [], []
            for e in range(2):
                in_half = (lane >= HEAD_DIM) if e else (lane < HEAD_DIM)
                qe = jnp.where(in_half, q, jnp.zeros_like(q))
                s = lax.dot_general(ks, qe, NT_DIMS, preferred_element_type=F32)
                s = jnp.where(mask, s - slope_ref[t, e] * dist, NEG)
                m = jnp.max(s, axis=0, keepdims=True)
                if has_sink:
                    sk = sink_ref[t, e]
                    m = jnp.maximum(m, sk)
                p = jnp.exp2(s - m)
                den = jnp.sum(p, axis=0, keepdims=True)
                if has_sink:
                    den = den + jnp.exp2(sk - m)
                ot = lax.dot_general(vs, p.astype(BF16), TN_DIMS, preferred_element_type=F32)
                outs.append(ot / den)
                lses.append((m + jnp.log2(den)) * (1.0 / LOG2E))
            sl = slice(t * LANES, (t + 1) * LANES)
            out_refs[0][0, cols, sl] = jnp.where(row < HEAD_DIM, outs[0], outs[1]).T.astype(out_refs[0].dtype)
            if want_lse:
                out_refs[1][0, cols, sl] = jnp.where(row < HEAD_DIM, lses[0], lses[1]).T


def _band_attention(src, pos, radius, slopes, sink, q_blk, k_slab0, v_slab0, nkv, want_lse):
    bn, length, _ = src.shape
    tq = _tile(length, 512)
    nb = length // tq
    rb = tq // radius
    nrb = length // radius
    w = tq + 2 * radius
    kvw = nkv * LANES
    idx = jnp.clip(jnp.arange(nb)[:, None] * tq - radius + jnp.arange(w)[None, :], 0, length - 1)
    posf = pos.astype(F32)
    posw = posf[:, idx][:, :, :, None]
    posq = posf[:, None, :]
    has_sink = sink is not None
    if sink is None:
        sink = jnp.zeros((N_QSLAB, 2), F32)

    def halo(slab0, which):
        blk = slab0 // nkv
        if which == 0:
            return pl.BlockSpec((1, radius, kvw), lambda b, i: (b, jnp.maximum(i * rb - 1, 0), blk))
        if which == 1:
            return pl.BlockSpec((1, tq, kvw), lambda b, i: (b, i, blk))
        return pl.BlockSpec((1, radius, kvw), lambda b, i: (b, jnp.minimum((i + 1) * rb, nrb - 1), blk))

    smem = pl.BlockSpec(memory_space=pltpu.SMEM)
    qw = N_QSLAB * LANES
    out_specs = [pl.BlockSpec((1, tq, qw), lambda b, i: (b, i, 0))]
    out_shape = [jax.ShapeDtypeStruct((bn, length, qw), BF16)]
    if want_lse:
        out_specs.append(pl.BlockSpec((1, tq, qw), lambda b, i: (b, i, 0)))
        out_shape.append(jax.ShapeDtypeStruct((bn, length, qw), F32))
    return pl.pallas_call(
        functools.partial(_band_kernel, radius=radius, tq=tq, gq=min(BAND_GQ, tq), length=length, nkv=nkv,
                          has_sink=has_sink, want_lse=want_lse),
        grid=(bn, nb),
        in_specs=[smem, smem,
                  pl.BlockSpec((1, tq, qw), lambda b, i: (b, i, q_blk)),
                  halo(k_slab0, 0), halo(k_slab0, 1), halo(k_slab0, 2),
                  halo(v_slab0, 0), halo(v_slab0, 1), halo(v_slab0, 2),
                  pl.BlockSpec((1, 1, tq), lambda b, i: (b, 0, i)),
                  pl.BlockSpec((1, 1, w, 1), lambda b, i: (b, i, 0, 0))],
        out_specs=out_specs,
        out_shape=out_shape,
        compiler_params=_params("parallel", "arbitrary"),
        name="band_attn",
    )(slopes, sink, src, src, src, src, src, src, src, posq, posw)


def _c_merge_kernel(o0_ref, o1_ref, o2_ref, l0_ref, l1_ref, l2_ref, out_ref):
    l0, l1, l2 = l0_ref[...], l1_ref[...], l2_ref[...]
    m = jnp.maximum(jnp.maximum(l0, l1), l2)
    w0, w1, w2 = jnp.exp(l0 - m), jnp.exp(l1 - m), jnp.exp(l2 - m)
    num = w0 * o0_ref[...].astype(F32) + w1 * o1_ref[...].astype(F32) + w2 * o2_ref[...].astype(F32)
    out_ref[...] = (num / (w0 + w1 + w2)).astype(out_ref.dtype)


def _c_merge(outs, lses):
    t, n = outs[0].shape
    tm = _tile(t, 1024)
    spec = pl.BlockSpec((tm, n), lambda i: (i, 0))
    return pl.pallas_call(
        _c_merge_kernel,
        grid=(t // tm,),
        in_specs=[spec] * 6,
        out_specs=spec,
        out_shape=jax.ShapeDtypeStruct((t, n), BF16),
        compiler_params=_params("parallel"),
        name="c_merge",
    )(*outs, *lses)


def _flash_loop(n_chunks, score_fn, value_fn, m_refs, acc_refs):
    def body(j, carry):
        scores = score_fn(j)
        for u, (s, vt_c) in enumerate(zip(scores, value_fn(j))):
            m_old = m_refs[u][...]
            m_new = jnp.maximum(m_old, jnp.max(s, axis=0, keepdims=True))
            alpha = jnp.exp2(m_old - m_new)
            p = jnp.exp2(s - m_new).astype(BF16)
            acc_refs[u][...] = alpha * acc_refs[u][...] + jnp.dot(vt_c, p, preferred_element_type=F32)
            m_refs[u][...] = m_new
        return carry

    lax.fori_loop(0, n_chunks, body, 0)


def _flash_init(m_refs, acc_refs):
    for m_ref, acc_ref in zip(m_refs, acc_refs):
        m_ref[...] = jnp.full(m_ref.shape, NEG, F32)
        acc_ref[...] = jnp.zeros(acc_ref.shape, F32)


B_VT_ROWS = LANES + 16


def _flash_b_kernel(slope_ref, qt_ref, k_ref, vt_ref, posq_ref, posk_ref, lq1_ref, lk1_ref, lq2_ref, lk2_ref,
                    subg_ref, o_ref, *scratch, tq, tk, seq, lam_init):
    n_units = 2 * B_HEADS
    m_refs, acc_refs = scratch[:n_units], scratch[n_units:]
    _flash_init(m_refs, acc_refs)
    row = lax.broadcasted_iota(jnp.int32, (LANES, tq), 0)
    slab_of = [2 * (u % 2) + (u // 2) // 2 for u in range(n_units)]
    qs = []
    for u in range(n_units):
        qm = qt_ref[0, slab_of[u]]
        qs.append(jnp.where((row >= HEAD_DIM) if (u // 2) % 2 else (row < HEAD_DIM), qm, jnp.zeros_like(qm)))
    posq = posq_ref[0]

    def score_fn(j):
        off = pl.multiple_of(j * tk, tk)
        posk = posk_ref[0, pl.ds(off, tk), :]
        dist = jnp.abs(jnp.concatenate([posk] * (tq // LANES), axis=1) - posq)
        k_c = k_ref[0, pl.ds(off, tk), :]
        scores = []
        for hd in range(B_HEADS):
            bias = slope_ref[hd] * dist
            for u in (2 * hd, 2 * hd + 1):
                sl = slab_of[u]
                scores.append(jnp.dot(k_c[:, sl * LANES:(sl + 1) * LANES], qs[u], preferred_element_type=F32)
                              - bias)
        return scores

    def value_fn(j):
        off = pl.multiple_of(j * tk, tk)
        vt_cs = [vt_ref[0, hd, :, pl.ds(off, tk)] for hd in range(B_HEADS)]
        return [vt_cs[u // 2] for u in range(n_units)]

    _flash_loop(seq // tk, score_fn, value_fn, m_refs, acc_refs)

    def normalised(u):
        acc = acc_refs[u][...]
        return acc[:LANES] / acc[LANES:LANES + 1]

    for hd in range(B_HEADS):
        u0, u1 = 2 * hd, 2 * hd + 1
        lam1 = jnp.sum(lq1_ref[hd:hd + 1, :] * lk1_ref[hd:hd + 1, :], axis=-1, keepdims=True)
        lam2 = jnp.sum(lq2_ref[hd:hd + 1, :] * lk2_ref[hd:hd + 1, :], axis=-1, keepdims=True)
        lam = jnp.exp(lam1) - jnp.exp(lam2) + lam_init
        a = (normalised(u0) - lam * normalised(u1)).T
        ms = jnp.mean(a * a, axis=-1, keepdims=True)
        out = a * lax.rsqrt(ms + EPS) * subg_ref[...] * (1.0 - lam_init)
        o_ref[0, :, hd * LANES:(hd + 1) * LANES] = out.astype(o_ref.dtype)


def _flash_b(y_main, pos, slopes, lq1, lk1, lq2, lk2, subg, lam_init, q_slab0, k_slab0, v_slab0):
    bsz, seq, _ = y_main.shape
    tq = _tile(seq, 512)
    tk = _tile(seq, 512)
    nq = B_HEADS
    posf = pos.astype(F32)
    posq = posf[:, None, :]
    posk = jnp.broadcast_to(posf[:, :, None], (bsz, seq, LANES))
    qt = y_main[:, :, q_slab0 * LANES:(q_slab0 + nq) * LANES].reshape(bsz, seq, nq, LANES)
    qt = qt.transpose(0, 2, 3, 1)
    v = y_main[:, :, v_slab0 * LANES:(v_slab0 + B_HEADS) * LANES].reshape(bsz, seq, B_HEADS, LANES)
    vt = jnp.concatenate([v.transpose(0, 2, 3, 1), jnp.ones((bsz, B_HEADS, 1, seq), BF16),
                          jnp.zeros((bsz, B_HEADS, B_VT_ROWS - LANES - 1, seq), BF16)], axis=2)
    once = pl.Buffered(1)
    small = lambda a: pl.BlockSpec(a.shape, lambda b, i: (0,) * a.ndim)
    n_units = 2 * nq
    return pl.pallas_call(
        functools.partial(_flash_b_kernel, tq=tq, tk=tk, seq=seq, lam_init=lam_init),
        grid=(bsz, seq // tq),
        in_specs=[pl.BlockSpec(memory_space=pltpu.SMEM),
                  pl.BlockSpec((1, nq, LANES, tq), lambda b, i: (b, 0, 0, i)),
                  pl.BlockSpec((1, seq, nq * LANES), lambda b, i: (b, 0, k_slab0 // nq), pipeline_mode=once),
                  pl.BlockSpec((1, B_HEADS, B_VT_ROWS, seq), lambda b, i: (b, 0, 0, 0), pipeline_mode=once),
                  pl.BlockSpec((1, 1, tq), lambda b, i: (b, 0, i)),
                  pl.BlockSpec((1, seq, LANES), lambda b, i: (b, 0, 0), pipeline_mode=once),
                  small(lq1), small(lk1), small(lq2), small(lk2), small(subg)],
        out_specs=pl.BlockSpec((1, tq, B_HEADS * LANES), lambda b, i: (b, i, 0)),
        out_shape=jax.ShapeDtypeStruct((bsz, seq, B_HEADS * LANES), BF16),
        scratch_shapes=[pltpu.VMEM((1, tq), F32)] * n_units + [pltpu.VMEM((B_VT_ROWS, tq), F32)] * n_units,
        compiler_params=_params("parallel", "arbitrary"),
        name="flash_diff",
    )(slopes, qt, y_main, vt, posq, posk, lq1, lk1, lq2, lk2, subg)


D_VT_ROWS = D_V + 16
D_GROUP = 8


def _flash_d_kernel(*refs, tq, tk, seq):
    n = D_GROUP
    q_refs, k_refs, vt_refs = refs[:n], refs[n:2 * n], refs[2 * n:3 * n]
    o_ref = refs[3 * n]
    scratch = refs[3 * n + 1:]
    m_refs, acc_refs = scratch[:n], scratch[n:]
    _flash_init(m_refs, acc_refs)
    qs = [q_ref[0, 0] for q_ref in q_refs]

    def score_fn(j):
        off = pl.multiple_of(j * tk, tk)
        return [jnp.dot(k_refs[e][0, pl.ds(off, tk), :], qs[e], preferred_element_type=F32) for e in range(n)]

    def value_fn(j):
        off = pl.multiple_of(j * tk, tk)
        return [vt_refs[e][0, 0, :, pl.ds(off, tk)] for e in range(n)]

    _flash_loop(seq // tk, score_fn, value_fn, m_refs, acc_refs)
    outs = []
    for e in range(n):
        acc = acc_refs[e][...]
        outs.append(acc[:D_V] / acc[D_V:D_V + 1])
    o_ref[0] = jnp.concatenate(outs, axis=0).T.astype(o_ref.dtype)


def _flash_d(qd, kd, vd):
    bsz, seq, _ = qd.shape
    tq = _tile(seq, 512)
    tk = _tile(seq, 512)
    n = D_GROUP
    once = pl.Buffered(1)
    qt = qd.reshape(bsz, seq, D_HEADS, LANES).transpose(0, 2, 3, 1)
    vt = vd.reshape(bsz, seq, D_HEADS, LANES)[:, :, :, :D_VT_ROWS].transpose(0, 2, 3, 1)
    q_spec = lambda e: pl.BlockSpec((1, 1, LANES, tq), lambda b, hg, i: (b, n * hg + e, 0, i))
    k_spec = lambda e: pl.BlockSpec((1, seq, LANES), lambda b, hg, i: (b, 0, n * hg + e), pipeline_mode=once)
    vt_spec = lambda e: pl.BlockSpec((1, 1, D_VT_ROWS, seq), lambda b, hg, i: (b, n * hg + e, 0, 0),
                                     pipeline_mode=once)
    return pl.pallas_call(
        functools.partial(_flash_d_kernel, tq=tq, tk=tk, seq=seq),
        grid=(bsz, D_HEADS // n, seq // tq),
        in_specs=[q_spec(e) for e in range(n)] + [k_spec(e) for e in range(n)]
                 + [vt_spec(e) for e in range(n)],
        out_specs=pl.BlockSpec((1, tq, n * D_V), lambda b, hg, i: (b, i, hg)),
        out_shape=jax.ShapeDtypeStruct((bsz, seq, D_HEADS * D_V), BF16),
        scratch_shapes=[pltpu.VMEM((1, tq), F32)] * n + [pltpu.VMEM((D_VT_ROWS, tq), F32)] * n,
        compiler_params=_params("parallel", "parallel", "arbitrary"),
        name="flash_latent",
    )(*([qt] * n + [kd] * n + [vt] * n))


def _merge_kernel(h_ref, x_ref, oa_ref, ob_ref, oc_ref, od_ref, wg_ref, wb_ref, wo_ref, out_ref, acc_scr):
    j = pl.program_id(1)

    @pl.when(j == 0)
    def _():
        acc_scr[...] = jnp.zeros(acc_scr.shape, F32)

    gate = jax.nn.sigmoid(jnp.dot(h_ref[...], wg_ref[...], preferred_element_type=F32))
    for br, o_ref in enumerate((oa_ref, ob_ref, oc_ref, od_ref)):
        @pl.when(j == br)
        def _(o_ref=o_ref):
            acc_scr[...] += gate * jnp.dot(o_ref[...], wb_ref[0], preferred_element_type=F32)

    @pl.when(j == N_BRANCH - 1)
    def _():
        out_ref[...] = x_ref[...] + jnp.dot(acc_scr[...].astype(BF16), wo_ref[...],
                                            preferred_element_type=F32)


def _merge(h2d, x2d, branch_outs, wg, wb, wo):
    t, d = x2d.shape
    tm = _tile(t, 512)
    bw = wb.shape[1]
    row = lambda width: pl.BlockSpec((tm, width), lambda i, j: (i, 0))
    return pl.pallas_call(
        _merge_kernel,
        grid=(t // tm, N_BRANCH),
        in_specs=[row(d), row(d), row(bw), row(bw), row(bw), row(bw),
                  pl.BlockSpec((d, d), lambda i, j: (0, j)),
                  pl.BlockSpec((1, bw, d), lambda i, j: (j, 0, 0)),
                  pl.BlockSpec((d, d), lambda i, j: (0, 0))],
        out_specs=row(d),
        out_shape=jax.ShapeDtypeStruct((t, d), F32),
        scratch_shapes=[pltpu.VMEM((tm, d), F32)],
        compiler_params=_params("parallel", "arbitrary"),
        name="gated_merge",
    )(h2d, x2d, *branch_outs, wg, wb, wo)


HALO = 8


def _ffn_kernel(x_ref, xp_ref, xn_ref, g_ref, wg_ref, wu_ref, cwg_ref, cwu_ref, cbg_ref, cbu_ref,
                wd_ref, out_ref, hn_scr, acc_scr, u_scr, *, tm, tiles_per_seq):
    i = pl.program_id(0)
    j = pl.program_id(1)
    rows = tm + 2 * HALO

    @pl.when(j == 0)
    def _():
        xe = jnp.concatenate([xp_ref[...], x_ref[...], xn_ref[...]], axis=0)
        ms = jnp.mean(xe * xe, axis=-1, keepdims=True)
        hn = xe * lax.rsqrt(ms + EPS) * g_ref[...]
        r = lax.broadcasted_iota(jnp.int32, (rows, 1), 0)
        first = (i % tiles_per_seq) == 0
        last = (i % tiles_per_seq) == tiles_per_seq - 1
        dead = ((r == HALO - 1) & first) | ((r == HALO + tm) & last)
        hn_scr[...] = jnp.where(dead, 0.0, hn).astype(BF16)
        acc_scr[...] = jnp.zeros(acc_scr.shape, F32)

    def conv(w_ref, cw_ref, cb_ref):
        u_scr[...] = jnp.dot(hn_scr[...], w_ref[...], preferred_element_type=F32)
        return (cw_ref[0:1, :] * u_scr[HALO - 1:HALO - 1 + tm, :] + cw_ref[1:2, :] * u_scr[HALO:HALO + tm, :]
                + cw_ref[2:3, :] * u_scr[HALO + 1:HALO + 1 + tm, :] + cb_ref[...])

    yg = conv(wg_ref, cwg_ref, cbg_ref)
    yu = conv(wu_ref, cwu_ref, cbu_ref)
    act = (jax.nn.silu(yg) * yu).astype(BF16)
    acc_scr[...] += jnp.dot(act, wd_ref[...], preferred_element_type=F32)

    @pl.when(j == pl.num_programs(1) - 1)
    def _():
        out_ref[...] = x_ref[...] + acc_scr[...]


def _ffn(x2d, seq, g, w_up, conv_w, conv_b, w_down):
    t, d = x2d.shape
    dff = w_down.shape[0]
    tm = _tile(seq, 512)
    tf = _tile(dff, 1408)
    nf = dff // tf
    nhb = t // HALO
    rb = tm // HALO
    return pl.pallas_call(
        functools.partial(_ffn_kernel, tm=tm, tiles_per_seq=seq // tm),
        grid=(t // tm, nf),
        in_specs=[pl.BlockSpec((tm, d), lambda i, j: (i, 0)),
                  pl.BlockSpec((HALO, d), lambda i, j: (jnp.maximum(i * rb - 1, 0), 0)),
                  pl.BlockSpec((HALO, d), lambda i, j: (jnp.minimum((i + 1) * rb, nhb - 1), 0)),
                  pl.BlockSpec((1, d), lambda i, j: (0, 0)),
                  pl.BlockSpec((d, tf), lambda i, j: (0, j)),
                  pl.BlockSpec((d, tf), lambda i, j: (0, nf + j)),
                  pl.BlockSpec((3, tf), lambda i, j: (0, j)),
                  pl.BlockSpec((3, tf), lambda i, j: (0, nf + j)),
                  pl.BlockSpec((1, tf), lambda i, j: (0, j)),
                  pl.BlockSpec((1, tf), lambda i, j: (0, nf + j)),
                  pl.BlockSpec((tf, d), lambda i, j: (j, 0))],
        out_specs=pl.BlockSpec((tm, d), lambda i, j: (i, 0)),
        out_shape=jax.ShapeDtypeStruct((t, d), F32),
        scratch_shapes=[pltpu.VMEM((tm + 2 * HALO, d), BF16), pltpu.VMEM((tm, d), F32),
                        pltpu.VMEM((tm + 2 * HALO, tf), F32)],
        compiler_params=_params("parallel", "arbitrary"),
        name="conv_mlp",
    )(x2d, x2d, x2d, g, w_up, w_up, conv_w, conv_w, conv_b, conv_b, w_down)


def _alibi_slopes(n):
    return 2.0 ** (-8.0 * np.arange(1, n + 1) / n)


def _pair_split_perm():
    cols = []
    for t in range(A_Q_HEADS // 2):
        for e in range(2):
            hd = t + (A_Q_HEADS // 2) * e
            cols.extend(range(hd * HEAD_DIM, (hd + 1) * HEAD_DIM))
    return np.asarray(cols, np.int32)


def _layer(x, pos, layer_idx, norm1_g, w_in, a_q_norm, a_k_norm, a_sink, b_q_norm, b_k_norm, b_lam_q1,
           b_lam_k1, b_lam_q2, b_lam_k2, b_subln_g, c_q_norm, c_k_norm, d_cq_norm, d_ckv_norm, d_w_uq,
           d_w_ukv, d_q_norm, d_k_norm, w_branch, w_o, norm2_g, ffn_w_up, ffn_conv_w, ffn_conv_b,
           ffn_w_down):
    bsz, seq, d = x.shape
    t = bsz * seq
    x2d = x.reshape(t, d)
    ng = len(C_PATTERNS)
    qscale = HEAD_DIM ** -0.5

    n_a_in = A_Q_HEADS * HEAD_DIM + 2 * LANES
    n_a = n_a_in + 2 * LANES
    n_b = 3 * 2 * B_HEADS * HEAD_DIM
    n_c = 3 * ng * C_HEADS * HEAD_DIM
    n_main = n_a + n_b + n_c
    perm = _pair_split_perm()
    w_main = jnp.concatenate([w_in[:, :A_Q_HEADS * HEAD_DIM][:, perm],
                              w_in[:, A_Q_HEADS * HEAD_DIM:n_a_in],
                              jnp.zeros((d, n_a - n_a_in), w_in.dtype),
                              w_in[:, n_a_in:n_a_in + n_b + n_c]], axis=1).astype(BF16)
    ones = lambda n: jnp.ones((n,), F32)
    zeros = lambda n: jnp.zeros((n,), F32)
    rep = lambda gvec, n: jnp.tile(gvec.astype(F32), n)
    gains = [rep(a_q_norm, 8) * (qscale * LOG2E), rep(a_k_norm, 2), ones(128 + n_a - n_a_in),
             rep(b_q_norm, 8) * (qscale * LOG2E), rep(b_k_norm, 8), ones(512)]
    flags = [ones(512), ones(128), zeros(128 + n_a - n_a_in), ones(512), ones(512), zeros(512)]
    for _ in range(ng):
        gains += [rep(c_q_norm, 8) * (qscale * LOG2E), rep(c_k_norm, 8), ones(512)]
        flags += [ones(512), ones(512), zeros(512)]
    colgain = jnp.concatenate(gains)[None, :]
    normflag = jnp.concatenate(flags)[None, :]
    y_main, h2d = _in_proj(x2d, norm1_g[None, :].astype(F32), w_main, colgain, normflag)
    y3 = y_main.reshape(bsz, seq, n_main)

    sl_a = _alibi_slopes(A_Q_HEADS)
    half = A_Q_HEADS // 2
    slopes_a = jnp.asarray(np.stack([sl_a[:half], sl_a[half:]], axis=1) * LOG2E, F32)
    sink_a = jnp.stack([a_sink[:half], a_sink[half:]], axis=1).astype(F32) * LOG2E
    (oa,) = _band_attention(y3, pos, A_RADIUS, slopes_a, sink_a, 0, 4, 5, 1, False)

    lam_init = 0.8 - 0.6 * math.exp(-0.3 * layer_idx)
    slopes_b = jnp.asarray(_alibi_slopes(B_HEADS) * LOG2E, F32)
    sb = n_a // LANES
    ob = _flash_b(y3, pos, slopes_b, b_lam_q1.astype(F32), b_lam_k1.astype(F32), b_lam_q2.astype(F32),
                  b_lam_k2.astype(F32), b_subln_g[None, :].astype(F32), lam_init, sb, sb + 4, sb + 8)

    sl_c = _alibi_slopes(ng * C_HEADS).reshape(ng, C_HEADS // 2, 2)
    c_outs, c_lses = [], []
    for gi, (window, r) in enumerate(C_PATTERNS):
        radius = window // (2 * r)
        slab0 = (n_a + n_b) // LANES + gi * 12
        slopes_c = jnp.asarray(sl_c[gi] * LOG2E, F32)
        if r == 1:
            o_g, lse_g = _band_attention(y3, pos, radius, slopes_c, None, slab0 // 4, slab0 + 4, slab0 + 8, 4, True)
        else:
            wc = 3 * C_HEADS * HEAD_DIM
            src = y3[:, :, slab0 * LANES:slab0 * LANES + wc]
            src = src.reshape(bsz, seq // r, r, wc).transpose(0, 2, 1, 3).reshape(bsz * r, seq // r, wc)
            pos_r = pos.reshape(bsz, seq // r, r).transpose(0, 2, 1).reshape(bsz * r, seq // r)
            o_g, lse_g = _band_attention(src, pos_r, radius, slopes_c, None, 0, 4, 8, 4, True)
            unstride = lambda a: a.reshape(bsz, r, seq // r, -1).transpose(0, 2, 1, 3).reshape(bsz, seq, -1)
            o_g, lse_g = unstride(o_g), unstride(lse_g)
        c_outs.append(o_g.reshape(t, -1))
        c_lses.append(lse_g.reshape(t, -1))
    oc = _c_merge(c_outs, c_lses)

    n0 = n_a_in + n_b + n_c
    kr_cols = jnp.zeros((d, LANES), w_in.dtype).at[:, D_NOPE:D_NOPE + D_ROPE].set(
        w_in[:, n0 + D_Q_RANK + D_KV_RANK:n0 + D_Q_RANK + D_KV_RANK + D_ROPE])
    w_d = jnp.concatenate([w_in[:, n0:n0 + D_Q_RANK + D_KV_RANK], kr_cols], axis=1).astype(BF16)
    dq = D_NOPE + D_ROPE
    pad_heads = lambda wm, width: jnp.pad(wm.reshape(wm.shape[0], D_HEADS, width),
                                          ((0, 0), (0, 0), (0, LANES - width))).reshape(wm.shape[0], -1)
    wuq = pad_heads(d_w_uq, dq).astype(BF16)
    ukv = d_w_ukv.reshape(D_KV_RANK, D_HEADS, D_NOPE + D_V)
    wuk = pad_heads(ukv[:, :, :D_NOPE].reshape(D_KV_RANK, -1), D_NOPE).astype(BF16)
    wuv = pad_heads(ukv[:, :, D_NOPE:].reshape(D_KV_RANK, -1), D_V).astype(BF16)
    gq = jnp.pad(d_q_norm.astype(F32) * (dq ** -0.5 * LOG2E), (0, LANES - dq))[None, :]
    gk = jnp.pad(d_k_norm.astype(F32), (0, LANES - dq))[None, :]
    halfr = D_ROPE // 2
    inv = ROPE_THETA ** (-np.arange(halfr, dtype=np.float32) / halfr)
    inv_tab = np.zeros((1, LANES), np.float32)
    inv_tab[0, D_NOPE:D_NOPE + halfr] = inv
    inv_tab[0, D_NOPE + halfr:D_NOPE + D_ROPE] = inv
    pos_col = pos.astype(F32).reshape(t, 1)
    qd, kd, vd = _d_proj(h2d, w_d, d_cq_norm[None, :].astype(F32), d_ckv_norm[None, :].astype(F32),
                         wuq, wuk, wuv, gq, gk, pos_col, jnp.asarray(inv_tab))
    rs3 = lambda a: a.reshape(bsz, seq, -1)
    od = _flash_d(rs3(qd), rs3(kd), rs3(vd))

    n_gate0 = n0 + D_Q_RANK + D_KV_RANK + D_ROPE
    w_gate = w_in[:, n_gate0:].astype(BF16)
    wb = jnp.concatenate([w_branch[0][perm][None], w_branch[1:]], axis=0).astype(BF16)
    x1 = _merge(h2d, x2d, [oa.reshape(t, -1), ob.reshape(t, -1), oc, od.reshape(t, -1)],
                w_gate, wb, w_o.astype(BF16))

    x2 = _ffn(x1, seq, norm2_g[None, :].astype(F32), ffn_w_up.astype(BF16), ffn_conv_w.astype(F32),
              ffn_conv_b[None, :].astype(F32), ffn_w_down.astype(BF16))
    return x2.reshape(bsz, seq, d)


def kernel(x, positions, norm1_g, w_in, a_q_norm, a_k_norm, a_sink, b_q_norm, b_k_norm, b_lam_q1,
           b_lam_k1, b_lam_q2, b_lam_k2, b_subln_g, c_q_norm, c_k_norm, d_cq_norm, d_ckv_norm, d_w_uq,
           d_w_ukv, d_q_norm, d_k_norm, w_branch, w_o, norm2_g, ffn_w_up, ffn_conv_w, ffn_conv_b,
           ffn_w_down):
    depth = w_in.shape[0]
    per_layer = (norm1_g, w_in, a_q_norm, a_k_norm, a_sink, b_q_norm, b_k_norm, b_lam_q1, b_lam_k1,
                 b_lam_q2, b_lam_k2, b_subln_g, c_q_norm, c_k_norm, d_cq_norm, d_ckv_norm, d_w_uq,
                 d_w_ukv, d_q_norm, d_k_norm, w_branch, w_o, norm2_g, ffn_w_up, ffn_conv_w, ffn_conv_b,
                 ffn_w_down)
    for layer in range(depth):
        x = _layer(x, positions, layer, *[p[layer] for p in per_layer])
    return x
```

```python
import functools
import math

import jax
import jax.numpy as jnp
import numpy as np
from jax import lax
from jax.experimental import pallas as pl
from jax.experimental.pallas import tpu as pltpu

F32 = jnp.float32
BF16 = jnp.bfloat16

LANES = 128
HEAD_DIM = 64
A_Q_HEADS = 8
A_RADIUS = 128
B_HEADS = 4
C_PATTERNS = ((128, 1), (512, 4), (2048, 16))
C_HEADS = 8
D_HEADS = 8
D_Q_RANK = 384
D_KV_RANK = 256
D_NOPE = 64
D_ROPE = 32
D_V = 64
ROPE_THETA = 10000.0
N_BRANCH = 4
BRANCH_WIDTH = 512
EPS = 1e-6
NEG = -1e30
LOG2E = math.log2(math.e)
VMEM_LIMIT_BYTES = 56 * 1024 * 1024

NT_DIMS = (((1,), (1,)), ((), ()))
TN_DIMS = (((0,), (0,)), ((), ()))


def _tile(n, pref):
    return pref if n % pref == 0 else n


def _params(*sem):
    return pltpu.CompilerParams(dimension_semantics=sem, vmem_limit_bytes=VMEM_LIMIT_BYTES)


SEG_CHUNK = 256
B_VT_ROWS = LANES + 16


def _seg_ones():
    return jnp.asarray(np.kron(np.eye(SEG_CHUNK // HEAD_DIM), np.ones((HEAD_DIM, HEAD_DIM))), BF16)


def _head_normed(y, nf_ref, cg_ref, ones_ref):
    y2 = (y * y).astype(BF16)
    seg = jnp.concatenate(
        [jnp.dot(y2[:, c * SEG_CHUNK:(c + 1) * SEG_CHUNK], ones_ref[...], preferred_element_type=F32)
         for c in range(y.shape[1] // SEG_CHUNK)], axis=1)
    rs = lax.rsqrt(seg * (1.0 / HEAD_DIM) + EPS)
    return y * (jnp.where(nf_ref[...] > 0.0, rs, 1.0) * cg_ref[...])


def _in_proj_kernel(x_ref, g_ref, w_ref, cg_ref, nf_ref, ones_ref, y_ref, h_ref, qt_ref, vt_ref, hn_scr,
                    *, qt_tile, vt_tile):
    j = pl.program_id(1)

    @pl.when(j == 0)
    def _():
        x = x_ref[...]
        ms = jnp.mean(x * x, axis=-1, keepdims=True)
        hn = (x * lax.rsqrt(ms + EPS) * g_ref[...]).astype(BF16)
        hn_scr[...] = hn
        h_ref[...] = hn

    y = _head_normed(jnp.dot(hn_scr[...], w_ref[...], preferred_element_type=F32), nf_ref, cg_ref, ones_ref)
    y_ref[...] = y.astype(y_ref.dtype)

    @pl.when(j == qt_tile)
    def _():
        for s in range(B_HEADS):
            qt_ref[0, s] = y[:, s * LANES:(s + 1) * LANES].T.astype(qt_ref.dtype)

    @pl.when(j == vt_tile)
    def _():
        tm = y.shape[0]
        ones_row = (lax.broadcasted_iota(jnp.int32, (B_VT_ROWS - LANES, tm), 0) == 0).astype(vt_ref.dtype)
        for s in range(B_HEADS):
            vt_ref[0, s, :LANES, :] = y[:, s * LANES:(s + 1) * LANES].T.astype(vt_ref.dtype)
            vt_ref[0, s, LANES:, :] = ones_row


def _in_proj(x2d, g, w, colgain, normflag, bsz, q_col, v_col):
    t, d = x2d.shape
    n = w.shape[1]
    seq = t // bsz
    tm = _tile(seq, 1024)
    tn = _tile(n, 1024)
    assert q_col % tn == 0 and v_col % tn == 0
    nper = seq // tm
    return pl.pallas_call(
        functools.partial(_in_proj_kernel, qt_tile=q_col // tn, vt_tile=v_col // tn),
        grid=(t // tm, n // tn),
        in_specs=[
            pl.BlockSpec((tm, d), lambda i, j: (i, 0)),
            pl.BlockSpec((1, d), lambda i, j: (0, 0)),
            pl.BlockSpec((d, tn), lambda i, j: (0, j)),
            pl.BlockSpec((1, tn), lambda i, j: (0, j)),
            pl.BlockSpec((1, tn), lambda i, j: (0, j)),
            pl.BlockSpec((SEG_CHUNK, SEG_CHUNK), lambda i, j: (0, 0)),
        ],
        out_specs=[
            pl.BlockSpec((tm, tn), lambda i, j: (i, j)),
            pl.BlockSpec((tm, d), lambda i, j: (i, 0)),
            pl.BlockSpec((1, B_HEADS, LANES, tm), lambda i, j: (i // nper, 0, 0, i % nper)),
            pl.BlockSpec((1, B_HEADS, B_VT_ROWS, tm), lambda i, j: (i // nper, 0, 0, i % nper)),
        ],
        out_shape=[jax.ShapeDtypeStruct((t, n), BF16), jax.ShapeDtypeStruct((t, d), BF16),
                   jax.ShapeDtypeStruct((bsz, B_HEADS, LANES, seq), BF16),
                   jax.ShapeDtypeStruct((bsz, B_HEADS, B_VT_ROWS, seq), BF16)],
        scratch_shapes=[pltpu.VMEM((tm, d), BF16)],
        compiler_params=_params("parallel", "arbitrary"),
        name="in_proj",
    )(x2d, g, w, colgain, normflag, _seg_ones())


def _class_proj_kernel(h_ref, w_ref, cg_ref, nf_ref, ones_ref, y_ref):
    y = jnp.dot(h_ref[0], w_ref[...], preferred_element_type=F32)
    y_ref[0, 0] = _head_normed(y, nf_ref, cg_ref, ones_ref).astype(y_ref.dtype)


def _class_proj(h2d, w, colgain, normflag, bsz, r):
    t, d = h2d.shape
    n = w.shape[1]
    length = t // bsz // r
    tm = _tile(length, 1024)
    tn = n
    h3 = h2d.reshape(bsz, length, r * d)
    return pl.pallas_call(
        _class_proj_kernel,
        grid=(bsz, r, length // tm, n // tn),
        in_specs=[
            pl.BlockSpec((1, tm, d), lambda b, c, i, j: (b, i, c)),
            pl.BlockSpec((d, tn), lambda b, c, i, j: (0, j)),
            pl.BlockSpec((1, tn), lambda b, c, i, j: (0, j)),
            pl.BlockSpec((1, tn), lambda b, c, i, j: (0, j)),
            pl.BlockSpec((SEG_CHUNK, SEG_CHUNK), lambda b, c, i, j: (0, 0)),
        ],
        out_specs=pl.BlockSpec((1, 1, tm, tn), lambda b, c, i, j: (b, c, i, j)),
        out_shape=jax.ShapeDtypeStruct((bsz, r, length, n), BF16),
        compiler_params=_params("parallel", "parallel", "parallel", "arbitrary"),
        name="class_proj",
    )(h3, w, colgain, normflag, _seg_ones())


def _rope_slab(t, c_tab, s1_tab, s2_tab):
    return t * c_tab + pltpu.roll(t, LANES - D_ROPE // 2, 1) * s1_tab + pltpu.roll(t, D_ROPE // 2, 1) * s2_tab


def _d_proj_kernel(h_ref, wd_ref, gcq_ref, gckv_ref, wuq_ref, wuk_ref, wuv_ref, gq_ref, gk_ref,
                   pos_ref, inv_ref, qt_ref, k_ref, vt_ref):
    y = jnp.dot(h_ref[...], wd_ref[...], preferred_element_type=F32)
    cq = y[:, :D_Q_RANK]
    ckv = y[:, D_Q_RANK:D_Q_RANK + D_KV_RANK]
    kr = y[:, D_Q_RANK + D_KV_RANK:]
    cqn = (cq * lax.rsqrt(jnp.mean(cq * cq, axis=-1, keepdims=True) + EPS) * gcq_ref[...]).astype(BF16)
    ckvn = (ckv * lax.rsqrt(jnp.mean(ckv * ckv, axis=-1, keepdims=True) + EPS) * gckv_ref[...]).astype(BF16)
    q = jnp.dot(cqn, wuq_ref[...], preferred_element_type=F32)
    kn = jnp.dot(ckvn, wuk_ref[...], preferred_element_type=F32)
    v = jnp.dot(ckvn, wuv_ref[...], preferred_element_type=F32)

    lane = lax.broadcasted_iota(jnp.int32, (1, LANES), 1)
    ang = pos_ref[...] * inv_ref[...]
    cs, sn = jnp.cos(ang), jnp.sin(ang)
    half = D_ROPE // 2
    c_tab = jnp.where((lane >= D_NOPE) & (lane < D_NOPE + D_ROPE), cs, 1.0)
    s1_tab = jnp.where((lane >= D_NOPE) & (lane < D_NOPE + half), -sn, 0.0)
    s2_tab = jnp.where((lane >= D_NOPE + half) & (lane < D_NOPE + D_ROPE), sn, 0.0)
    inv_w = 1.0 / (D_NOPE + D_ROPE)
    for hd in range(D_HEADS):
        sl = slice(hd * LANES, (hd + 1) * LANES)
        qs = q[:, sl]
        qs = qs * lax.rsqrt(jnp.sum(qs * qs, axis=-1, keepdims=True) * inv_w + EPS) * gq_ref[...]
        qt_ref[0, hd] = _rope_slab(qs, c_tab, s1_tab, s2_tab).T.astype(qt_ref.dtype)
        ks = kn[:, sl] + kr
        ks = ks * lax.rsqrt(jnp.sum(ks * ks, axis=-1, keepdims=True) * inv_w + EPS) * gk_ref[...]
        k_ref[:, sl] = _rope_slab(ks, c_tab, s1_tab, s2_tab).astype(k_ref.dtype)
        vt_ref[0, hd] = jnp.where(lane == D_V, 1.0, v[:, sl]).T[:D_VT_ROWS].astype(vt_ref.dtype)


def _d_proj(h2d, wd, gcq, gckv, wuq, wuk, wuv, gq, gk, pos_col, inv_tab, bsz):
    t, d = h2d.shape
    seq = t // bsz
    tm = _tile(seq, 512)
    nper = seq // tm
    nd = D_HEADS * LANES
    full = lambda a: pl.BlockSpec(a.shape, lambda i: (0,) * a.ndim)
    return pl.pallas_call(
        _d_proj_kernel,
        grid=(t // tm,),
        in_specs=[pl.BlockSpec((tm, d), lambda i: (i, 0)), full(wd), full(gcq), full(gckv), full(wuq),
                  full(wuk), full(wuv), full(gq), full(gk), pl.BlockSpec((tm, 1), lambda i: (i, 0)),
                  full(inv_tab)],
        out_specs=[pl.BlockSpec((1, D_HEADS, LANES, tm), lambda i: (i // nper, 0, 0, i % nper)),
                   pl.BlockSpec((tm, nd), lambda i: (i, 0)),
                   pl.BlockSpec((1, D_HEADS, D_VT_ROWS, tm), lambda i: (i // nper, 0, 0, i % nper))],
        out_shape=[jax.ShapeDtypeStruct((bsz, D_HEADS, LANES, seq), BF16),
                   jax.ShapeDtypeStruct((t, nd), BF16),
                   jax.ShapeDtypeStruct((bsz, D_HEADS, D_VT_ROWS, seq), BF16)],
        compiler_params=_params("parallel"),
        name="d_proj",
    )(h2d, wd, gcq, gckv, wuq, wuk, wuv, gq, gk, pos_col, inv_tab)


N_QSLAB = 4
BAND_GQ = 128


def _band_kernel(slope_ref, sink_ref, q_ref, kp_ref, kc_ref, kn_ref, vp_ref, vc_ref, vn_ref,
                 posq_ref, posw_ref, *out_refs, radius, tq, gq, length, nkv, has_sink, want_lse):
    i = pl.program_id(1)
    kw = jnp.concatenate([kp_ref[0], kc_ref[0], kn_ref[0]], axis=0)
    vw = jnp.concatenate([vp_ref[0], vc_ref[0], vn_ref[0]], axis=0)
    wg = gq + 2 * radius
    c_io = lax.broadcasted_iota(jnp.int32, (wg, gq), 0)
    r_io = lax.broadcasted_iota(jnp.int32, (wg, gq), 1)
    in_band = jnp.abs(c_io - radius - r_io) <= radius
    lane = lax.broadcasted_iota(jnp.int32, (gq, LANES), 1)
    row = lax.broadcasted_iota(jnp.int32, (LANES, gq), 0)
    for g in range(tq // gq):
        rows = slice(g * gq, g * gq + wg)
        cols = slice(g * gq, (g + 1) * gq)
        jabs = i * tq + g * gq - radius + c_io
        mask = in_band & (jabs >= 0) & (jabs < length)
        dist = jnp.abs(posw_ref[0, 0, rows, :] - posq_ref[0, :, cols])
        for t in range(N_QSLAB):
            q = q_ref[0, cols, t * LANES:(t + 1) * LANES]
            kt = t if nkv == N_QSLAB else 0
            ks = kw[rows, kt * LANES:(kt + 1) * LANES]
            vs = vw[rows, kt * LANES:(kt + 1) * LANES]
            outs, lses = [], []
            for e in range(2):
                in_half = (lane >= HEAD_DIM) if e else (lane < HEAD_DIM)
                qe = jnp.where(in_half, q, jnp.zeros_like(q))
                s = lax.dot_general(ks, qe, NT_DIMS, preferred_element_type=F32)
                s = jnp.where(mask, s - slope_ref[t, e] * dist, NEG)
                m = jnp.max(s, axis=0, keepdims=True)
                if has_sink:
                    sk = sink_ref[t, e]
                    m = jnp.maximum(m, sk)
                p = jnp.exp2(s - m)
                den = jnp.sum(p, axis=0, keepdims=True)
                if has_sink:
                    den = den + jnp.exp2(sk - m)
                ot = lax.dot_general(vs, p.astype(BF16), TN_DIMS, preferred_element_type=F32)
                outs.append(ot / den)
                lses.append((m + jnp.log2(den)) * (1.0 / LOG2E))
            sl = slice(t * LANES, (t + 1) * LANES)
            out_refs[0][0, cols, sl] = jnp.where(row < HEAD_DIM, outs[0], outs[1]).T.astype(out_refs[0].dtype)
            if want_lse:
                out_refs[1][0, cols, sl] = jnp.where(row < HEAD_DIM, lses[0], lses[1]).T


def _band_attention(src, pos, radius, slopes, sink, q_blk, k_slab0, v_slab0, nkv, want_lse):
    bn, length, _ = src.shape
    tq = _tile(length, 512)
    nb = length // tq
    rb = tq // radius
    nrb = length // radius
    w = tq + 2 * radius
    kvw = nkv * LANES
    idx = jnp.clip(jnp.arange(nb)[:, None] * tq - radius + jnp.arange(w)[None, :], 0, length - 1)
    posf = pos.astype(F32)
    posw = posf[:, idx][:, :, :, None]
    posq = posf[:, None, :]
    has_sink = sink is not None
    if sink is None:
        sink = jnp.zeros((N_QSLAB, 2), F32)

    def halo(slab0, which):
        blk = slab0 // nkv
        if which == 0:
            return pl.BlockSpec((1, radius, kvw), lambda b, i: (b, jnp.maximum(i * rb - 1, 0), blk))
        if which == 1:
            return pl.BlockSpec((1, tq, kvw), lambda b, i: (b, i, blk))
        return pl.BlockSpec((1, radius, kvw), lambda b, i: (b, jnp.minimum((i + 1) * rb, nrb - 1), blk))

    smem = pl.BlockSpec(memory_space=pltpu.SMEM)
    qw = N_QSLAB * LANES
    out_specs = [pl.BlockSpec((1, tq, qw), lambda b, i: (b, i, 0))]
    out_shape = [jax.ShapeDtypeStruct((bn, length, qw), BF16)]
    if want_lse:
        out_specs.append(pl.BlockSpec((1, tq, qw), lambda b, i: (b, i, 0)))
        out_shape.append(jax.ShapeDtypeStruct((bn, length, qw), F32))
    return pl.pallas_call(
        functools.partial(_band_kernel, radius=radius, tq=tq, gq=min(BAND_GQ, tq), length=length, nkv=nkv,
                          has_sink=has_sink, want_lse=want_lse),
        grid=(bn, nb),
        in_specs=[smem, smem,
                  pl.BlockSpec((1, tq, qw), lambda b, i: (b, i, q_blk)),
                  halo(k_slab0, 0), halo(k_slab0, 1), halo(k_slab0, 2),
                  halo(v_slab0, 0), halo(v_slab0, 1), halo(v_slab0, 2),
                  pl.BlockSpec((1, 1, tq), lambda b, i: (b, 0, i)),
                  pl.BlockSpec((1, 1, w, 1), lambda b, i: (b, i, 0, 0))],
        out_specs=out_specs,
        out_shape=out_shape,
        compiler_params=_params("parallel", "arbitrary"),
        name="band_attn",
    )(slopes, sink, src, src, src, src, src, src, src, posq, posw)


def _c_merge_kernel(*refs, dilations, tm):
    ng = len(dilations)
    o_refs, l_refs, out_ref = refs[:ng], refs[ng:2 * ng], refs[2 * ng]
    o_scr, l_scr = refs[2 * ng + 1:]

    def natural(ref, scr, r):
        if r == 1:
            return ref[0].astype(F32)
        nslab = scr.shape[0]
        for c in range(r):
            blk = ref[0, c].astype(F32)
            for s in range(nslab):
                scr[s, pl.ds(c, tm // r, stride=r), :] = blk[:, s * LANES:(s + 1) * LANES]
        return jnp.concatenate([scr[s] for s in range(nslab)], axis=1)

    ls = [natural(l_refs[g], l_scr, r) for g, r in enumerate(dilations)]
    m = functools.reduce(jnp.maximum, ls)
    ws = [jnp.exp(l - m) for l in ls]
    num = sum(w * natural(o_refs[g], o_scr, r) for g, (w, r) in enumerate(zip(ws, dilations)))
    out_ref[0] = (num / sum(ws)).astype(out_ref.dtype)


def _c_merge(outs, lses, dilations, bsz, seq):
    n = outs[0].shape[-1]
    tm = _tile(seq, 512)
    views, specs = [], []
    for arrs in (outs, lses):
        for a, r in zip(arrs, dilations):
            if r == 1:
                views.append(a.reshape(bsz, seq, n))
                specs.append(pl.BlockSpec((1, tm, n), lambda b, i: (b, i, 0)))
            else:
                views.append(a.reshape(bsz, r, seq // r, n))
                specs.append(pl.BlockSpec((1, r, tm // r, n), lambda b, i: (b, 0, i, 0)))
    return pl.pallas_call(
        functools.partial(_c_merge_kernel, dilations=tuple(dilations), tm=tm),
        grid=(bsz, seq // tm),
        in_specs=specs,
        out_specs=pl.BlockSpec((1, tm, n), lambda b, i: (b, i, 0)),
        out_shape=jax.ShapeDtypeStruct((bsz, seq, n), BF16),
        scratch_shapes=[pltpu.VMEM((n // LANES, tm, LANES), F32), pltpu.VMEM((n // LANES, tm, LANES), F32)],
        compiler_params=_params("parallel", "parallel"),
        name="c_merge",
    )(*views)


def _flash_loop(n_chunks, score_fn, value_fn, m_refs, acc_refs):
    def body(j, carry):
        scores = score_fn(j)
        for u, (s, vt_c) in enumerate(zip(scores, value_fn(j))):
            m_old = m_refs[u][...]
            m_new = jnp.maximum(m_old, jnp.max(s, axis=0, keepdims=True))
            alpha = jnp.exp2(m_old - m_new)
            p = jnp.exp2(s - m_new).astype(BF16)
            acc_refs[u][...] = alpha * acc_refs[u][...] + jnp.dot(vt_c, p, preferred_element_type=F32)
            m_refs[u][...] = m_new
        return carry

    lax.fori_loop(0, n_chunks, body, 0)


def _flash_init(m_refs, acc_refs):
    for m_ref, acc_ref in zip(m_refs, acc_refs):
        m_ref[...] = jnp.full(m_ref.shape, NEG, F32)
        acc_ref[...] = jnp.zeros(acc_ref.shape, F32)


def _flash_b_kernel(slope_ref, qt_ref, k_ref, vt_ref, posq_ref, posk_ref, lq1_ref, lk1_ref, lq2_ref, lk2_ref,
                    subg_ref, o_ref, *scratch, tq, tk, seq, lam_init):
    n_units = 2 * B_HEADS
    m_refs, acc_refs = scratch[:n_units], scratch[n_units:]
    _flash_init(m_refs, acc_refs)
    row = lax.broadcasted_iota(jnp.int32, (LANES, tq), 0)
    slab_of = [2 * (u % 2) + (u // 2) // 2 for u in range(n_units)]
    qs = []
    for u in range(n_units):
        qm = qt_ref[0, slab_of[u]]
        qs.append(jnp.where((row >= HEAD_DIM) if (u // 2) % 2 else (row < HEAD_DIM), qm, jnp.zeros_like(qm)))
    posq = posq_ref[0]

    def score_fn(j):
        off = pl.multiple_of(j * tk, tk)
        posk = posk_ref[0, pl.ds(off, tk), :]
        dist = jnp.abs(jnp.concatenate([posk] * (tq // LANES), axis=1) - posq)
        k_c = k_ref[0, pl.ds(off, tk), :]
        scores = []
        for hd in range(B_HEADS):
            bias = slope_ref[hd] * dist
            for u in (2 * hd, 2 * hd + 1):
                sl = slab_of[u]
                scores.append(jnp.dot(k_c[:, sl * LANES:(sl + 1) * LANES], qs[u], preferred_element_type=F32)
                              - bias)
        return scores

    def value_fn(j):
        off = pl.multiple_of(j * tk, tk)
        vt_cs = [vt_ref[0, hd, :, pl.ds(off, tk)] for hd in range(B_HEADS)]
        return [vt_cs[u // 2] for u in range(n_units)]

    _flash_loop(seq // tk, score_fn, value_fn, m_refs, acc_refs)

    def normalised(u):
        acc = acc_refs[u][...]
        return acc[:LANES] / acc[LANES:LANES + 1]

    for hd in range(B_HEADS):
        u0, u1 = 2 * hd, 2 * hd + 1
        lam1 = jnp.sum(lq1_ref[hd:hd + 1, :] * lk1_ref[hd:hd + 1, :], axis=-1, keepdims=True)
        lam2 = jnp.sum(lq2_ref[hd:hd + 1, :] * lk2_ref[hd:hd + 1, :], axis=-1, keepdims=True)
        lam = jnp.exp(lam1) - jnp.exp(lam2) + lam_init
        a = (normalised(u0) - lam * normalised(u1)).T
        ms = jnp.mean(a * a, axis=-1, keepdims=True)
        out = a * lax.rsqrt(ms + EPS) * subg_ref[...] * (1.0 - lam_init)
        o_ref[0, :, hd * LANES:(hd + 1) * LANES] = out.astype(o_ref.dtype)


def _flash_b(y_main, qt, vt, pos, slopes, lq1, lk1, lq2, lk2, subg, lam_init, k_slab0):
    bsz, seq, _ = y_main.shape
    tq = _tile(seq, 512)
    tk = _tile(seq, 512)
    nq = B_HEADS
    posf = pos.astype(F32)
    posq = posf[:, None, :]
    posk = jnp.broadcast_to(posf[:, :, None], (bsz, seq, LANES))
    once = pl.Buffered(1)
    small = lambda a: pl.BlockSpec(a.shape, lambda b, i: (0,) * a.ndim)
    n_units = 2 * nq
    return pl.pallas_call(
        functools.partial(_flash_b_kernel, tq=tq, tk=tk, seq=seq, lam_init=lam_init),
        grid=(bsz, seq // tq),
        in_specs=[pl.BlockSpec(memory_space=pltpu.SMEM),
                  pl.BlockSpec((1, nq, LANES, tq), lambda b, i: (b, 0, 0, i)),
                  pl.BlockSpec((1, seq, nq * LANES), lambda b, i: (b, 0, k_slab0 // nq), pipeline_mode=once),
                  pl.BlockSpec((1, B_HEADS, B_VT_ROWS, seq), lambda b, i: (b, 0, 0, 0), pipeline_mode=once),
                  pl.BlockSpec((1, 1, tq), lambda b, i: (b, 0, i)),
                  pl.BlockSpec((1, seq, LANES), lambda b, i: (b, 0, 0), pipeline_mode=once),
                  small(lq1), small(lk1), small(lq2), small(lk2), small(subg)],
        out_specs=pl.BlockSpec((1, tq, B_HEADS * LANES), lambda b, i: (b, i, 0)),
        out_shape=jax.ShapeDtypeStruct((bsz, seq, B_HEADS * LANES), BF16),
        scratch_shapes=[pltpu.VMEM((1, tq), F32)] * n_units + [pltpu.VMEM((B_VT_ROWS, tq), F32)] * n_units,
        compiler_params=_params("parallel", "arbitrary"),
        name="flash_diff",
    )(slopes, qt, y_main, vt, posq, posk, lq1, lk1, lq2, lk2, subg)


D_VT_ROWS = D_V + 16
D_GROUP = 8


def _flash_d_kernel(*refs, tq, tk, seq):
    n = D_GROUP
    q_refs, k_refs, vt_refs = refs[:n], refs[n:2 * n], refs[2 * n:3 * n]
    o_ref = refs[3 * n]
    scratch = refs[3 * n + 1:]
    m_refs, acc_refs = scratch[:n], scratch[n:]
    _flash_init(m_refs, acc_refs)
    qs = [q_ref[0, 0] for q_ref in q_refs]

    def score_fn(j):
        off = pl.multiple_of(j * tk, tk)
        return [jnp.dot(k_refs[e][0, pl.ds(off, tk), :], qs[e], preferred_element_type=F32) for e in range(n)]

    def value_fn(j):
        off = pl.multiple_of(j * tk, tk)
        return [vt_refs[e][0, 0, :, pl.ds(off, tk)] for e in range(n)]

    _flash_loop(seq // tk, score_fn, value_fn, m_refs, acc_refs)
    outs = []
    for e in range(n):
        acc = acc_refs[e][...]
        outs.append(acc[:D_V] / acc[D_V:D_V + 1])
    o_ref[0] = jnp.concatenate(outs, axis=0).T.astype(o_ref.dtype)


def _flash_d(qt, kd, vt):
    bsz, seq, _ = kd.shape
    tq = _tile(seq, 512)
    tk = _tile(seq, 512)
    n = D_GROUP
    once = pl.Buffered(1)
    q_spec = lambda e: pl.BlockSpec((1, 1, LANES, tq), lambda b, hg, i: (b, n * hg + e, 0, i))
    k_spec = lambda e: pl.BlockSpec((1, seq, LANES), lambda b, hg, i: (b, 0, n * hg + e), pipeline_mode=once)
    vt_spec = lambda e: pl.BlockSpec((1, 1, D_VT_ROWS, seq), lambda b, hg, i: (b, n * hg + e, 0, 0),
                                     pipeline_mode=once)
    return pl.pallas_call(
        functools.partial(_flash_d_kernel, tq=tq, tk=tk, seq=seq),
        grid=(bsz, D_HEADS // n, seq // tq),
        in_specs=[q_spec(e) for e in range(n)] + [k_spec(e) for e in range(n)]
                 + [vt_spec(e) for e in range(n)],
        out_specs=pl.BlockSpec((1, tq, n * D_V), lambda b, hg, i: (b, i, hg)),
        out_shape=jax.ShapeDtypeStruct((bsz, seq, D_HEADS * D_V), BF16),
        scratch_shapes=[pltpu.VMEM((1, tq), F32)] * n + [pltpu.VMEM((D_VT_ROWS, tq), F32)] * n,
        compiler_params=_params("parallel", "parallel", "arbitrary"),
        name="flash_latent",
    )(*([qt] * n + [kd] * n + [vt] * n))


def _merge_kernel(h_ref, x_ref, oa_ref, ob_ref, oc_ref, od_ref, wg_ref, wb_ref, wo_ref, out_ref, acc_scr):
    j = pl.program_id(1)

    @pl.when(j == 0)
    def _():
        acc_scr[...] = jnp.zeros(acc_scr.shape, F32)

    gate = jax.nn.sigmoid(jnp.dot(h_ref[...], wg_ref[...], preferred_element_type=F32))
    for br, o_ref in enumerate((oa_ref, ob_ref, oc_ref, od_ref)):
        @pl.when(j == br)
        def _(o_ref=o_ref):
            acc_scr[...] += gate * jnp.dot(o_ref[...], wb_ref[0], preferred_element_type=F32)

    @pl.when(j == N_BRANCH - 1)
    def _():
        out_ref[...] = x_ref[...] + jnp.dot(acc_scr[...].astype(BF16), wo_ref[...],
                                            preferred_element_type=F32)


def _merge(h2d, x2d, branch_outs, wg, wb, wo):
    t, d = x2d.shape
    tm = _tile(t, 512)
    bw = wb.shape[1]
    row = lambda width: pl.BlockSpec((tm, width), lambda i, j: (i, 0))
    return pl.pallas_call(
        _merge_kernel,
        grid=(t // tm, N_BRANCH),
        in_specs=[row(d), row(d), row(bw), row(bw), row(bw), row(bw),
                  pl.BlockSpec((d, d), lambda i, j: (0, j)),
                  pl.BlockSpec((1, bw, d), lambda i, j: (j, 0, 0)),
                  pl.BlockSpec((d, d), lambda i, j: (0, 0))],
        out_specs=row(d),
        out_shape=jax.ShapeDtypeStruct((t, d), F32),
        scratch_shapes=[pltpu.VMEM((tm, d), F32)],
        compiler_params=_params("parallel", "arbitrary"),
        name="gated_merge",
    )(h2d, x2d, *branch_outs, wg, wb, wo)


HALO = 8


def _ffn_kernel(x_ref, xp_ref, xn_ref, g_ref, wg_ref, wu_ref, cwg_ref, cwu_ref, cbg_ref, cbu_ref,
                wd_ref, out_ref, hn_scr, acc_scr, u_scr, *, tm, tiles_per_seq):
    i = pl.program_id(0)
    j = pl.program_id(1)
    rows = tm + 2 * HALO

    @pl.when(j == 0)
    def _():
        xe = jnp.concatenate([xp_ref[...], x_ref[...], xn_ref[...]], axis=0)
        ms = jnp.mean(xe * xe, axis=-1, keepdims=True)
        hn = xe * lax.rsqrt(ms + EPS) * g_ref[...]
        r = lax.broadcasted_iota(jnp.int32, (rows, 1), 0)
        first = (i % tiles_per_seq) == 0
        last = (i % tiles_per_seq) == tiles_per_seq - 1
        dead = ((r == HALO - 1) & first) | ((r == HALO + tm) & last)
        hn_scr[...] = jnp.where(dead, 0.0, hn).astype(BF16)
        acc_scr[...] = jnp.zeros(acc_scr.shape, F32)

    def conv(w_ref, cw_ref, cb_ref):
        u_scr[...] = jnp.dot(hn_scr[...], w_ref[...], preferred_element_type=F32)
        return (cw_ref[0:1, :] * u_scr[HALO - 1:HALO - 1 + tm, :] + cw_ref[1:2, :] * u_scr[HALO:HALO + tm, :]
                + cw_ref[2:3, :] * u_scr[HALO + 1:HALO + 1 + tm, :] + cb_ref[...])

    yg = conv(wg_ref, cwg_ref, cbg_ref)
    yu = conv(wu_ref, cwu_ref, cbu_ref)
    act = (jax.nn.silu(yg) * yu).astype(BF16)
    acc_scr[...] += jnp.dot(act, wd_ref[...], preferred_element_type=F32)

    @pl.when(j == pl.num_programs(1) - 1)
    def _():
        out_ref[...] = x_ref[...] + acc_scr[...]


def _ffn(x2d, seq, g, w_up, conv_w, conv_b, w_down):
    t, d = x2d.shape
    dff = w_down.shape[0]
    tm = _tile(seq, 512)
    tf = _tile(dff, 1408)
    nf = dff // tf
    nhb = t // HALO
    rb = tm // HALO
    return pl.pallas_call(
        functools.partial(_ffn_kernel, tm=tm, tiles_per_seq=seq // tm),
        grid=(t // tm, nf),
        in_specs=[pl.BlockSpec((tm, d), lambda i, j: (i, 0)),
                  pl.BlockSpec((HALO, d), lambda i, j: (jnp.maximum(i * rb - 1, 0), 0)),
                  pl.BlockSpec((HALO, d), lambda i, j: (jnp.minimum((i + 1) * rb, nhb - 1), 0)),
                  pl.BlockSpec((1, d), lambda i, j: (0, 0)),
                  pl.BlockSpec((d, tf), lambda i, j: (0, j)),
                  pl.BlockSpec((d, tf), lambda i, j: (0, nf + j)),
                  pl.BlockSpec((3, tf), lambda i, j: (0, j)),
                  pl.BlockSpec((3, tf), lambda i, j: (0, nf + j)),
                  pl.BlockSpec((1, tf), lambda i, j: (0, j)),
                  pl.BlockSpec((1, tf), lambda i, j: (0, nf + j)),
                  pl.BlockSpec((tf, d), lambda i, j: (j, 0))],
        out_specs=pl.BlockSpec((tm, d), lambda i, j: (i, 0)),
        out_shape=jax.ShapeDtypeStruct((t, d), F32),
        scratch_shapes=[pltpu.VMEM((tm + 2 * HALO, d), BF16), pltpu.VMEM((tm, d), F32),
                        pltpu.VMEM((tm + 2 * HALO, tf), F32)],
        compiler_params=_params("parallel", "arbitrary"),
        name="conv_mlp",
    )(x2d, x2d, x2d, g, w_up, w_up, conv_w, conv_w, conv_b, conv_b, w_down)


def _alibi_slopes(n):
    return 2.0 ** (-8.0 * np.arange(1, n + 1) / n)


def _pair_split_perm():
    cols = []
    for t in range(A_Q_HEADS // 2):
        for e in range(2):
            hd = t + (A_Q_HEADS // 2) * e
            cols.extend(range(hd * HEAD_DIM, (hd + 1) * HEAD_DIM))
    return np.asarray(cols, np.int32)


def _layer(x, pos, layer_idx, norm1_g, w_in, a_q_norm, a_k_norm, a_sink, b_q_norm, b_k_norm, b_lam_q1,
           b_lam_k1, b_lam_q2, b_lam_k2, b_subln_g, c_q_norm, c_k_norm, d_cq_norm, d_ckv_norm, d_w_uq,
           d_w_ukv, d_q_norm, d_k_norm, w_branch, w_o, norm2_g, ffn_w_up, ffn_conv_w, ffn_conv_b,
           ffn_w_down):
    bsz, seq, d = x.shape
    t = bsz * seq
    x2d = x.reshape(t, d)
    ng = len(C_PATTERNS)
    qscale = HEAD_DIM ** -0.5

    n_a_in = A_Q_HEADS * HEAD_DIM + 2 * LANES
    n_a = n_a_in + 2 * LANES
    n_b = 3 * 2 * B_HEADS * HEAD_DIM
    n_cg = 3 * C_HEADS * HEAD_DIM
    n_main = n_a + n_b + n_cg
    perm = _pair_split_perm()
    w_main = jnp.concatenate([w_in[:, :A_Q_HEADS * HEAD_DIM][:, perm],
                              w_in[:, A_Q_HEADS * HEAD_DIM:n_a_in],
                              jnp.zeros((d, n_a - n_a_in), w_in.dtype),
                              w_in[:, n_a_in:n_a_in + n_b + n_cg]], axis=1).astype(BF16)
    ones = lambda n: jnp.ones((n,), F32)
    zeros = lambda n: jnp.zeros((n,), F32)
    rep = lambda gvec, n: jnp.tile(gvec.astype(F32), n)
    c_gain = jnp.concatenate([rep(c_q_norm, 8) * (qscale * LOG2E), rep(c_k_norm, 8), ones(512)])[None, :]
    c_flag = jnp.concatenate([ones(512), ones(512), zeros(512)])[None, :]
    colgain = jnp.concatenate([rep(a_q_norm, 8) * (qscale * LOG2E), rep(a_k_norm, 2), ones(128 + n_a - n_a_in),
                               rep(b_q_norm, 8) * (qscale * LOG2E), rep(b_k_norm, 8), ones(512)])[None, :]
    normflag = jnp.concatenate([ones(512), ones(128), zeros(128 + n_a - n_a_in),
                                ones(512), ones(512), zeros(512)])[None, :]
    colgain = jnp.concatenate([colgain, c_gain], axis=1)
    normflag = jnp.concatenate([normflag, c_flag], axis=1)
    y_main, h2d, qt_b, vt_b = _in_proj(x2d, norm1_g[None, :].astype(F32), w_main, colgain, normflag, bsz,
                                       n_a, n_a + 2 * 2 * B_HEADS * HEAD_DIM)
    y3 = y_main.reshape(bsz, seq, n_main)

    sl_a = _alibi_slopes(A_Q_HEADS)
    half = A_Q_HEADS // 2
    slopes_a = jnp.asarray(np.stack([sl_a[:half], sl_a[half:]], axis=1) * LOG2E, F32)
    sink_a = jnp.stack([a_sink[:half], a_sink[half:]], axis=1).astype(F32) * LOG2E
    (oa,) = _band_attention(y3, pos, A_RADIUS, slopes_a, sink_a, 0, 4, 5, 1, False)

    lam_init = 0.8 - 0.6 * math.exp(-0.3 * layer_idx)
    slopes_b = jnp.asarray(_alibi_slopes(B_HEADS) * LOG2E, F32)
    sb = n_a // LANES
    ob = _flash_b(y3, qt_b, vt_b, pos, slopes_b, b_lam_q1.astype(F32), b_lam_k1.astype(F32),
                  b_lam_q2.astype(F32), b_lam_k2.astype(F32), b_subln_g[None, :].astype(F32), lam_init, sb + 4)

    sl_c = _alibi_slopes(ng * C_HEADS).reshape(ng, C_HEADS // 2, 2)
    c_outs, c_lses, dilations = [], [], []
    for gi, (window, r) in enumerate(C_PATTERNS):
        radius = window // (2 * r)
        slopes_c = jnp.asarray(sl_c[gi] * LOG2E, F32)
        if r == 1:
            slab0 = (n_a + n_b) // LANES
            o_g, lse_g = _band_attention(y3, pos, radius, slopes_c, None, slab0 // 4, slab0 + 4, slab0 + 8, 4, True)
        else:
            col0 = n_a_in + n_b + gi * n_cg
            src = _class_proj(h2d, w_in[:, col0:col0 + n_cg].astype(BF16), c_gain, c_flag, bsz, r)
            src = src.reshape(bsz * r, seq // r, n_cg)
            pos_r = pos.reshape(bsz, seq // r, r).transpose(0, 2, 1).reshape(bsz * r, seq // r)
            o_g, lse_g = _band_attention(src, pos_r, radius, slopes_c, None, 0, 4, 8, 4, True)
        c_outs.append(o_g)
        c_lses.append(lse_g)
        dilations.append(r)
    oc = _c_merge(c_outs, c_lses, dilations, bsz, seq).reshape(t, -1)

    n0 = n_a_in + n_b + ng * n_cg
    kr_cols = jnp.zeros((d, LANES), w_in.dtype).at[:, D_NOPE:D_NOPE + D_ROPE].set(
        w_in[:, n0 + D_Q_RANK + D_KV_RANK:n0 + D_Q_RANK + D_KV_RANK + D_ROPE])
    w_d = jnp.concatenate([w_in[:, n0:n0 + D_Q_RANK + D_KV_RANK], kr_cols], axis=1).astype(BF16)
    dq = D_NOPE + D_ROPE
    pad_heads = lambda wm, width: jnp.pad(wm.reshape(wm.shape[0], D_HEADS, width),
                                          ((0, 0), (0, 0), (0, LANES - width))).reshape(wm.shape[0], -1)
    wuq = pad_heads(d_w_uq, dq).astype(BF16)
    ukv = d_w_ukv.reshape(D_KV_RANK, D_HEADS, D_NOPE + D_V)
    wuk = pad_heads(ukv[:, :, :D_NOPE].reshape(D_KV_RANK, -1), D_NOPE).astype(BF16)
    wuv = pad_heads(ukv[:, :, D_NOPE:].reshape(D_KV_RANK, -1), D_V).astype(BF16)
    gq = jnp.pad(d_q_norm.astype(F32) * (dq ** -0.5 * LOG2E), (0, LANES - dq))[None, :]
    gk = jnp.pad(d_k_norm.astype(F32), (0, LANES - dq))[None, :]
    halfr = D_ROPE // 2
    inv = ROPE_THETA ** (-np.arange(halfr, dtype=np.float32) / halfr)
    inv_tab = np.zeros((1, LANES), np.float32)
    inv_tab[0, D_NOPE:D_NOPE + halfr] = inv
    inv_tab[0, D_NOPE + halfr:D_NOPE + D_ROPE] = inv
    pos_col = pos.astype(F32).reshape(t, 1)
    qt_d, kd, vt_d = _d_proj(h2d, w_d, d_cq_norm[None, :].astype(F32), d_ckv_norm[None, :].astype(F32),
                             wuq, wuk, wuv, gq, gk, pos_col, jnp.asarray(inv_tab), bsz)
    od = _flash_d(qt_d, kd.reshape(bsz, seq, -1), vt_d)

    n_gate0 = n0 + D_Q_RANK + D_KV_RANK + D_ROPE
    w_gate = w_in[:, n_gate0:].astype(BF16)
    wb = jnp.concatenate([w_branch[0][perm][None], w_branch[1:]], axis=0).astype(BF16)
    x1 = _merge(h2d, x2d, [oa.reshape(t, -1), ob.reshape(t, -1), oc, od.reshape(t, -1)],
                w_gate, wb, w_o.astype(BF16))

    x2 = _ffn(x1, seq, norm2_g[None, :].astype(F32), ffn_w_up.astype(BF16), ffn_conv_w.astype(F32),
              ffn_conv_b[None, :].astype(F32), ffn_w_down.astype(BF16))
    return x2.reshape(bsz, seq, d)


def kernel(x, positions, norm1_g, w_in, a_q_norm, a_k_norm, a_sink, b_q_norm, b_k_norm, b_lam_q1,
           b_lam_k1, b_lam_q2, b_lam_k2, b_subln_g, c_q_norm, c_k_norm, d_cq_norm, d_ckv_norm, d_w_uq,
           d_w_ukv, d_q_norm, d_k_norm, w_branch, w_o, norm2_g, ffn_w_up, ffn_conv_w, ffn_conv_b,
           ffn_w_down):
    depth = w_in.shape[0]
    per_layer = (norm1_g, w_in, a_q_norm, a_k_norm, a_sink, b_q_norm, b_k_norm, b_lam_q1, b_lam_k1,
                 b_lam_q2, b_lam_k2, b_subln_g, c_q_norm, c_k_norm, d_cq_norm, d_ckv_norm, d_w_uq,
                 d_w_ukv, d_q_norm, d_k_norm, w_branch, w_o, norm2_g, ffn_w_up, ffn_conv_w, ffn_conv_b,
                 ffn_w_down)
    for layer in range(depth):
        x = _layer(x, positions, layer, *[p[layer] for p in per_layer])
    return x
```

```python
import functools
import math

import jax
import jax.numpy as jnp
import numpy as np
from jax import lax
from jax.experimental import pallas as pl
from jax.experimental.pallas import tpu as pltpu

F32 = jnp.float32
BF16 = jnp.bfloat16

LANES = 128
HEAD_DIM = 64
A_Q_HEADS = 8
A_RADIUS = 128
B_HEADS = 4
C_PATTERNS = ((128, 1), (512, 4), (2048, 16))
C_HEADS = 8
D_HEADS = 8
D_Q_RANK = 384
D_KV_RANK = 256
D_NOPE = 64
D_ROPE = 32
D_V = 64
ROPE_THETA = 10000.0
N_BRANCH = 4
BRANCH_WIDTH = 512
EPS = 1e-6
NEG = -1e30
LOG2E = math.log2(math.e)
VMEM_LIMIT_BYTES = 56 * 1024 * 1024

NT_DIMS = (((1,), (1,)), ((), ()))
TN_DIMS = (((0,), (0,)), ((), ()))


def _tile(n, pref):
    return pref if n % pref == 0 else n


def _params(*sem):
    return pltpu.CompilerParams(dimension_semantics=sem, vmem_limit_bytes=VMEM_LIMIT_BYTES)


SEG_CHUNK = 256
B_VT_ROWS = LANES + 16


def _seg_ones():
    return jnp.asarray(np.kron(np.eye(SEG_CHUNK // HEAD_DIM), np.ones((HEAD_DIM, HEAD_DIM))), BF16)


def _head_normed(y, nf_ref, cg_ref, ones_ref):
    y2 = (y * y).astype(BF16)
    seg = jnp.concatenate(
        [jnp.dot(y2[:, c * SEG_CHUNK:(c + 1) * SEG_CHUNK], ones_ref[...], preferred_element_type=F32)
         for c in range(y.shape[1] // SEG_CHUNK)], axis=1)
    rs = lax.rsqrt(seg * (1.0 / HEAD_DIM) + EPS)
    return y * (jnp.where(nf_ref[...] > 0.0, rs, 1.0) * cg_ref[...])


def _in_proj_kernel(x_ref, g_ref, w_ref, cg_ref, nf_ref, ones_ref, y_ref, h_ref, qt_ref, vt_ref, *rest,
                    qt_tile, vt_tile, n_main_tiles, class_groups):
    class_refs, (hn_scr, y_scr) = rest[:len(class_groups)], rest[len(class_groups):]
    j = pl.program_id(1)

    @pl.when(j == 0)
    def _():
        x = x_ref[...]
        ms = jnp.mean(x * x, axis=-1, keepdims=True)
        hn = (x * lax.rsqrt(ms + EPS) * g_ref[...]).astype(BF16)
        hn_scr[...] = hn
        h_ref[...] = hn

    y = _head_normed(jnp.dot(hn_scr[...], w_ref[...], preferred_element_type=F32), nf_ref, cg_ref, ones_ref)
    tm, tn = y.shape

    @pl.when(j < n_main_tiles)
    def _():
        y_ref[...] = y.astype(y_ref.dtype)

    for c_ref, (r, tile0, n_tiles) in zip(class_refs, class_groups):
        @pl.when((j >= tile0) & (j < tile0 + n_tiles))
        def _(c_ref=c_ref, r=r):
            for s in range(tn // LANES):
                y_scr[s] = y[:, s * LANES:(s + 1) * LANES]
            for c in range(r):
                for s in range(tn // LANES):
                    c_ref[0, c, :, s * LANES:(s + 1) * LANES] = (
                        y_scr[s, pl.ds(c, tm // r, stride=r), :].astype(c_ref.dtype))

    @pl.when(j == qt_tile)
    def _():
        for s in range(B_HEADS):
            qt_ref[0, s] = y[:, s * LANES:(s + 1) * LANES].T.astype(qt_ref.dtype)

    @pl.when(j == vt_tile)
    def _():
        tm = y.shape[0]
        ones_row = (lax.broadcasted_iota(jnp.int32, (B_VT_ROWS - LANES, tm), 0) == 0).astype(vt_ref.dtype)
        for s in range(B_HEADS):
            vt_ref[0, s, :LANES, :] = y[:, s * LANES:(s + 1) * LANES].T.astype(vt_ref.dtype)
            vt_ref[0, s, LANES:, :] = ones_row


def _in_proj(x2d, g, w, colgain, normflag, bsz, n_main, q_col, v_col, class_cols):
    t, d = x2d.shape
    n = w.shape[1]
    seq = t // bsz
    tm = _tile(seq, 1024)
    tn = 512
    assert n % tn == 0 and n_main % tn == 0 and q_col % tn == 0 and v_col % tn == 0
    assert all(start % tn == 0 and width % tn == 0 and tm % r == 0 for r, start, width in class_cols)
    nper = seq // tm
    n_main_tiles = n_main // tn
    class_groups = tuple((r, start // tn, width // tn) for r, start, width in class_cols)

    def class_spec(r, tile0, n_tiles):
        return pl.BlockSpec((1, r, tm // r, tn),
                            lambda i, j: (i // nper, 0, i % nper, jnp.clip(j - tile0, 0, n_tiles - 1)))

    return pl.pallas_call(
        functools.partial(_in_proj_kernel, qt_tile=q_col // tn, vt_tile=v_col // tn, n_main_tiles=n_main_tiles,
                          class_groups=class_groups),
        grid=(t // tm, n // tn),
        in_specs=[
            pl.BlockSpec((tm, d), lambda i, j: (i, 0)),
            pl.BlockSpec((1, d), lambda i, j: (0, 0)),
            pl.BlockSpec((d, tn), lambda i, j: (0, j)),
            pl.BlockSpec((1, tn), lambda i, j: (0, j)),
            pl.BlockSpec((1, tn), lambda i, j: (0, j)),
            pl.BlockSpec((SEG_CHUNK, SEG_CHUNK), lambda i, j: (0, 0)),
        ],
        out_specs=[
            pl.BlockSpec((tm, tn), lambda i, j: (i, jnp.minimum(j, n_main_tiles - 1))),
            pl.BlockSpec((tm, d), lambda i, j: (i, 0)),
            pl.BlockSpec((1, B_HEADS, LANES, tm), lambda i, j: (i // nper, 0, 0, i % nper)),
            pl.BlockSpec((1, B_HEADS, B_VT_ROWS, tm), lambda i, j: (i // nper, 0, 0, i % nper)),
        ] + [class_spec(*grp) for grp in class_groups],
        out_shape=[jax.ShapeDtypeStruct((t, n_main), BF16), jax.ShapeDtypeStruct((t, d), BF16),
                   jax.ShapeDtypeStruct((bsz, B_HEADS, LANES, seq), BF16),
                   jax.ShapeDtypeStruct((bsz, B_HEADS, B_VT_ROWS, seq), BF16)]
                  + [jax.ShapeDtypeStruct((bsz, r, seq // r, width), BF16) for r, _, width in class_cols],
        scratch_shapes=[pltpu.VMEM((tm, d), BF16), pltpu.VMEM((tn // LANES, tm, LANES), F32)],
        compiler_params=_params("parallel", "arbitrary"),
        name="in_proj",
    )(x2d, g, w, colgain, normflag, _seg_ones())


def _rope_slab(t, c_tab, s1_tab, s2_tab):
    return t * c_tab + pltpu.roll(t, LANES - D_ROPE // 2, 1) * s1_tab + pltpu.roll(t, D_ROPE // 2, 1) * s2_tab


def _d_proj_kernel(h_ref, wd_ref, gcq_ref, gckv_ref, wuq_ref, wuk_ref, wuv_ref, gq_ref, gk_ref,
                   pos_ref, inv_ref, qt_ref, k_ref, vt_ref):
    y = jnp.dot(h_ref[...], wd_ref[...], preferred_element_type=F32)
    cq = y[:, :D_Q_RANK]
    ckv = y[:, D_Q_RANK:D_Q_RANK + D_KV_RANK]
    kr = y[:, D_Q_RANK + D_KV_RANK:]
    cqn = (cq * lax.rsqrt(jnp.mean(cq * cq, axis=-1, keepdims=True) + EPS) * gcq_ref[...]).astype(BF16)
    ckvn = (ckv * lax.rsqrt(jnp.mean(ckv * ckv, axis=-1, keepdims=True) + EPS) * gckv_ref[...]).astype(BF16)
    q = jnp.dot(cqn, wuq_ref[...], preferred_element_type=F32)
    kn = jnp.dot(ckvn, wuk_ref[...], preferred_element_type=F32)
    v = jnp.dot(ckvn, wuv_ref[...], preferred_element_type=F32)

    lane = lax.broadcasted_iota(jnp.int32, (1, LANES), 1)
    ang = pos_ref[...] * inv_ref[...]
    cs, sn = jnp.cos(ang), jnp.sin(ang)
    half = D_ROPE // 2
    c_tab = jnp.where((lane >= D_NOPE) & (lane < D_NOPE + D_ROPE), cs, 1.0)
    s1_tab = jnp.where((lane >= D_NOPE) & (lane < D_NOPE + half), -sn, 0.0)
    s2_tab = jnp.where((lane >= D_NOPE + half) & (lane < D_NOPE + D_ROPE), sn, 0.0)
    inv_w = 1.0 / (D_NOPE + D_ROPE)
    for hd in range(D_HEADS):
        sl = slice(hd * LANES, (hd + 1) * LANES)
        qs = q[:, sl]
        qs = qs * lax.rsqrt(jnp.sum(qs * qs, axis=-1, keepdims=True) * inv_w + EPS) * gq_ref[...]
        qt_ref[0, hd] = _rope_slab(qs, c_tab, s1_tab, s2_tab).T.astype(qt_ref.dtype)
        ks = kn[:, sl] + kr
        ks = ks * lax.rsqrt(jnp.sum(ks * ks, axis=-1, keepdims=True) * inv_w + EPS) * gk_ref[...]
        k_ref[:, sl] = _rope_slab(ks, c_tab, s1_tab, s2_tab).astype(k_ref.dtype)
        vt_ref[0, hd] = jnp.where(lane == D_V, 1.0, v[:, sl]).T[:D_VT_ROWS].astype(vt_ref.dtype)


def _d_proj(h2d, wd, gcq, gckv, wuq, wuk, wuv, gq, gk, pos_col, inv_tab, bsz):
    t, d = h2d.shape
    seq = t // bsz
    tm = _tile(seq, 512)
    nper = seq // tm
    nd = D_HEADS * LANES
    full = lambda a: pl.BlockSpec(a.shape, lambda i: (0,) * a.ndim)
    return pl.pallas_call(
        _d_proj_kernel,
        grid=(t // tm,),
        in_specs=[pl.BlockSpec((tm, d), lambda i: (i, 0)), full(wd), full(gcq), full(gckv), full(wuq),
                  full(wuk), full(wuv), full(gq), full(gk), pl.BlockSpec((tm, 1), lambda i: (i, 0)),
                  full(inv_tab)],
        out_specs=[pl.BlockSpec((1, D_HEADS, LANES, tm), lambda i: (i // nper, 0, 0, i % nper)),
                   pl.BlockSpec((tm, nd), lambda i: (i, 0)),
                   pl.BlockSpec((1, D_HEADS, D_VT_ROWS, tm), lambda i: (i // nper, 0, 0, i % nper))],
        out_shape=[jax.ShapeDtypeStruct((bsz, D_HEADS, LANES, seq), BF16),
                   jax.ShapeDtypeStruct((t, nd), BF16),
                   jax.ShapeDtypeStruct((bsz, D_HEADS, D_VT_ROWS, seq), BF16)],
        compiler_params=_params("parallel"),
        name="d_proj",
    )(h2d, wd, gcq, gckv, wuq, wuk, wuv, gq, gk, pos_col, inv_tab)


N_QSLAB = 4
BAND_GQ = 128


def _band_kernel(slope_ref, sink_ref, q_ref, kp_ref, kc_ref, kn_ref, vp_ref, vc_ref, vn_ref,
                 posq_ref, posw_ref, *out_refs, radius, tq, gq, length, nkv, has_sink, want_lse):
    i = pl.program_id(1)
    kw = jnp.concatenate([kp_ref[0], kc_ref[0], kn_ref[0]], axis=0)
    vw = jnp.concatenate([vp_ref[0], vc_ref[0], vn_ref[0]], axis=0)
    wg = gq + 2 * radius
    c_io = lax.broadcasted_iota(jnp.int32, (wg, gq), 0)
    r_io = lax.broadcasted_iota(jnp.int32, (wg, gq), 1)
    in_band = jnp.abs(c_io - radius - r_io) <= radius
    lane = lax.broadcasted_iota(jnp.int32, (gq, LANES), 1)
    row = lax.broadcasted_iota(jnp.int32, (LANES, gq), 0)
    for g in range(tq // gq):
        rows = slice(g * gq, g * gq + wg)
        cols = slice(g * gq, (g + 1) * gq)
        jabs = i * tq + g * gq - radius + c_io
        mask = in_band & (jabs >= 0) & (jabs < length)
        dist = jnp.abs(posw_ref[0, 0, rows, :] - posq_ref[0, :, cols])
        for t in range(N_QSLAB):
            q = q_ref[0, cols, t * LANES:(t + 1) * LANES]
            kt = t if nkv == N_QSLAB else 0
            ks = kw[rows, kt * LANES:(kt + 1) * LANES]
            vs = vw[rows, kt * LANES:(kt + 1) * LANES]
            outs, lses = [], []
            for e in range(2):
                in_half = (lane >= HEAD_DIM) if e else (lane < HEAD_DIM)
                qe = jnp.where(in_half, q, jnp.zeros_like(q))
                s = lax.dot_general(ks, qe, NT_DIMS, preferred_element_type=F32)
                s = jnp.where(mask, s - slope_ref[t, e] * dist, NEG)
                m = jnp.max(s, axis=0, keepdims=True)
                if has_sink:
                    sk = sink_ref[t, e]
                    m = jnp.maximum(m, sk)
                p = jnp.exp2(s - m)
                den = jnp.sum(p, axis=0, keepdims=True)
                if has_sink:
                    den = den + jnp.exp2(sk - m)
                ot = lax.dot_general(vs, p.astype(BF16), TN_DIMS, preferred_element_type=F32)
                outs.append(ot / den)
                lses.append((m + jnp.log2(den)) * (1.0 / LOG2E))
            sl = slice(t * LANES, (t + 1) * LANES)
            out_refs[0][0, cols, sl] = jnp.where(row < HEAD_DIM, outs[0], outs[1]).T.astype(out_refs[0].dtype)
            if want_lse:
                out_refs[1][0, cols, sl] = jnp.where(row < HEAD_DIM, lses[0], lses[1]).T


def _band_attention(src, pos, radius, slopes, sink, q_blk, k_slab0, v_slab0, nkv, want_lse):
    bn, length, _ = src.shape
    tq = _tile(length, 512)
    nb = length // tq
    rb = tq // radius
    nrb = length // radius
    w = tq + 2 * radius
    kvw = nkv * LANES
    idx = jnp.clip(jnp.arange(nb)[:, None] * tq - radius + jnp.arange(w)[None, :], 0, length - 1)
    posf = pos.astype(F32)
    posw = posf[:, idx][:, :, :, None]
    posq = posf[:, None, :]
    has_sink = sink is not None
    if sink is None:
        sink = jnp.zeros((N_QSLAB, 2), F32)

    def halo(slab0, which):
        blk = slab0 // nkv
        if which == 0:
            return pl.BlockSpec((1, radius, kvw), lambda b, i: (b, jnp.maximum(i * rb - 1, 0), blk))
        if which == 1:
            return pl.BlockSpec((1, tq, kvw), lambda b, i: (b, i, blk))
        return pl.BlockSpec((1, radius, kvw), lambda b, i: (b, jnp.minimum((i + 1) * rb, nrb - 1), blk))

    smem = pl.BlockSpec(memory_space=pltpu.SMEM)
    qw = N_QSLAB * LANES
    out_specs = [pl.BlockSpec((1, tq, qw), lambda b, i: (b, i, 0))]
    out_shape = [jax.ShapeDtypeStruct((bn, length, qw), BF16)]
    if want_lse:
        out_specs.append(pl.BlockSpec((1, tq, qw), lambda b, i: (b, i, 0)))
        out_shape.append(jax.ShapeDtypeStruct((bn, length, qw), F32))
    return pl.pallas_call(
        functools.partial(_band_kernel, radius=radius, tq=tq, gq=min(BAND_GQ, tq), length=length, nkv=nkv,
                          has_sink=has_sink, want_lse=want_lse),
        grid=(bn, nb),
        in_specs=[smem, smem,
                  pl.BlockSpec((1, tq, qw), lambda b, i: (b, i, q_blk)),
                  halo(k_slab0, 0), halo(k_slab0, 1), halo(k_slab0, 2),
                  halo(v_slab0, 0), halo(v_slab0, 1), halo(v_slab0, 2),
                  pl.BlockSpec((1, 1, tq), lambda b, i: (b, 0, i)),
                  pl.BlockSpec((1, 1, w, 1), lambda b, i: (b, i, 0, 0))],
        out_specs=out_specs,
        out_shape=out_shape,
        compiler_params=_params("parallel", "arbitrary"),
        name="band_attn",
    )(slopes, sink, src, src, src, src, src, src, src, posq, posw)


def _c_merge_kernel(*refs, dilations, tm):
    ng = len(dilations)
    o_refs, l_refs, out_ref = refs[:ng], refs[ng:2 * ng], refs[2 * ng]
    o_scr, l_scr = refs[2 * ng + 1:]

    def natural(ref, scr, r):
        if r == 1:
            return ref[0].astype(F32)
        nslab = scr.shape[0]
        for c in range(r):
            blk = ref[0, c].astype(F32)
            for s in range(nslab):
                scr[s, pl.ds(c, tm // r, stride=r), :] = blk[:, s * LANES:(s + 1) * LANES]
        return jnp.concatenate([scr[s] for s in range(nslab)], axis=1)

    ls = [natural(l_refs[g], l_scr, r) for g, r in enumerate(dilations)]
    m = functools.reduce(jnp.maximum, ls)
    ws = [jnp.exp(l - m) for l in ls]
    num = sum(w * natural(o_refs[g], o_scr, r) for g, (w, r) in enumerate(zip(ws, dilations)))
    out_ref[0] = (num / sum(ws)).astype(out_ref.dtype)


def _c_merge(outs, lses, dilations, bsz, seq):
    n = outs[0].shape[-1]
    tm = _tile(seq, 512)
    views, specs = [], []
    for arrs in (outs, lses):
        for a, r in zip(arrs, dilations):
            if r == 1:
                views.append(a.reshape(bsz, seq, n))
                specs.append(pl.BlockSpec((1, tm, n), lambda b, i: (b, i, 0)))
            else:
                views.append(a.reshape(bsz, r, seq // r, n))
                specs.append(pl.BlockSpec((1, r, tm // r, n), lambda b, i: (b, 0, i, 0)))
    return pl.pallas_call(
        functools.partial(_c_merge_kernel, dilations=tuple(dilations), tm=tm),
        grid=(bsz, seq // tm),
        in_specs=specs,
        out_specs=pl.BlockSpec((1, tm, n), lambda b, i: (b, i, 0)),
        out_shape=jax.ShapeDtypeStruct((bsz, seq, n), BF16),
        scratch_shapes=[pltpu.VMEM((n // LANES, tm, LANES), F32), pltpu.VMEM((n // LANES, tm, LANES), F32)],
        compiler_params=_params("parallel", "parallel"),
        name="c_merge",
    )(*views)


def _flash_loop(n_trips, score_fn, value_fn, m_refs, acc_refs, chunk_of=lambda trip: trip):
    def body(trip, carry):
        j = chunk_of(trip)
        scores = score_fn(j)
        for u, (s, vt_c) in enumerate(zip(scores, value_fn(j))):
            m_old = m_refs[u][...]
            m_new = jnp.maximum(m_old, jnp.max(s, axis=0, keepdims=True))
            alpha = jnp.exp2(m_old - m_new)
            p = jnp.exp2(s - m_new).astype(BF16)
            acc_refs[u][...] = alpha * acc_refs[u][...] + jnp.dot(vt_c, p, preferred_element_type=F32)
            m_refs[u][...] = m_new
        return carry

    lax.fori_loop(0, n_trips, body, 0)


def _flash_init(m_refs, acc_refs):
    for m_ref, acc_ref in zip(m_refs, acc_refs):
        m_ref[...] = jnp.full(m_ref.shape, NEG, F32)
        acc_ref[...] = jnp.zeros(acc_ref.shape, F32)


B_HEAD_GROUPS = ((2, 3), (1,), (0,))


def _flash_b_kernel(slope_ref, cnt_ref, lst_ref, qt_ref, k_ref, vt_ref, posq_ref, posk_ref, lq1_ref, lk1_ref,
                    lq2_ref, lk2_ref, subg_ref, o_ref, *scratch, tq, tk, seq, lam_init):
    n_units = 2 * B_HEADS
    n_chunks = seq // tk
    m_refs, acc_refs = scratch[:n_units], scratch[n_units:]
    _flash_init(m_refs, acc_refs)
    row = lax.broadcasted_iota(jnp.int32, (LANES, tq), 0)
    slab_of = [2 * (u % 2) + (u // 2) // 2 for u in range(n_units)]
    qs = []
    for u in range(n_units):
        qm = qt_ref[0, slab_of[u]]
        qs.append(jnp.where((row >= HEAD_DIM) if (u // 2) % 2 else (row < HEAD_DIM), qm, jnp.zeros_like(qm)))
    posq = posq_ref[0]
    tile = pl.program_id(0) * pl.num_programs(1) + pl.program_id(1)

    for g, heads in enumerate(B_HEAD_GROUPS):
        units = [2 * hd + mp for hd in heads for mp in range(2)]
        row0 = (tile * len(B_HEAD_GROUPS) + g)

        def score_fn(j, heads=heads):
            off = pl.multiple_of(j * tk, tk)
            posk = posk_ref[0, pl.ds(off, tk), :]
            dist = jnp.abs(jnp.concatenate([posk] * (tq // LANES), axis=1) - posq)
            k_c = k_ref[0, pl.ds(off, tk), :]
            scores = []
            for hd in heads:
                bias = slope_ref[hd] * dist
                for u in (2 * hd, 2 * hd + 1):
                    sl = slab_of[u]
                    scores.append(jnp.dot(k_c[:, sl * LANES:(sl + 1) * LANES], qs[u],
                                          preferred_element_type=F32) - bias)
            return scores

        def value_fn(j, heads=heads):
            off = pl.multiple_of(j * tk, tk)
            return [vt_ref[0, hd, :, pl.ds(off, tk)] for hd in heads for _ in range(2)]

        _flash_loop(cnt_ref[row0], score_fn, value_fn, [m_refs[u] for u in units], [acc_refs[u] for u in units],
                    chunk_of=lambda trip, row0=row0: lst_ref[row0 * n_chunks + trip])

    def normalised(u):
        acc = acc_refs[u][...]
        return acc[:LANES] / acc[LANES:LANES + 1]

    for hd in range(B_HEADS):
        u0, u1 = 2 * hd, 2 * hd + 1
        lam1 = jnp.sum(lq1_ref[hd:hd + 1, :] * lk1_ref[hd:hd + 1, :], axis=-1, keepdims=True)
        lam2 = jnp.sum(lq2_ref[hd:hd + 1, :] * lk2_ref[hd:hd + 1, :], axis=-1, keepdims=True)
        lam = jnp.exp(lam1) - jnp.exp(lam2) + lam_init
        a = (normalised(u0) - lam * normalised(u1)).T
        ms = jnp.mean(a * a, axis=-1, keepdims=True)
        out = a * lax.rsqrt(ms + EPS) * subg_ref[...] * (1.0 - lam_init)
        o_ref[0, :, hd * LANES:(hd + 1) * LANES] = out.astype(o_ref.dtype)


UNDERFLOW_LOG2 = 150.0


def _alibi_chunk_lists(posf, slopes, score_bound, tq, tk):
    bsz, seq = posf.shape
    qmin, qmax = posf.reshape(bsz, -1, tq).min(-1), posf.reshape(bsz, -1, tq).max(-1)
    kmin, kmax = posf.reshape(bsz, -1, tk).min(-1), posf.reshape(bsz, -1, tk).max(-1)
    dmin = jnp.maximum(jnp.maximum(qmin[:, :, None] - kmax[:, None, :], kmin[:, None, :] - qmax[:, :, None]), 0.0)
    needed = slopes[None, None, :, None] * dmin[:, :, None, :] < 2.0 * score_bound + UNDERFLOW_LOG2
    group = jnp.stack([functools.reduce(jnp.logical_or, [needed[:, :, hd] for hd in heads])
                       for heads in B_HEAD_GROUPS], axis=2)
    order = jnp.argsort(jnp.logical_not(group), axis=-1, stable=True).astype(jnp.int32)
    return group.sum(-1).astype(jnp.int32).reshape(-1), order.reshape(-1)


def _flash_b(y_main, qt, vt, pos, slopes, score_bound, lq1, lk1, lq2, lk2, subg, lam_init, k_slab0):
    bsz, seq, _ = y_main.shape
    tq = _tile(seq, 512)
    tk = _tile(seq, 512)
    nq = B_HEADS
    posf = pos.astype(F32)
    posq = posf[:, None, :]
    posk = jnp.broadcast_to(posf[:, :, None], (bsz, seq, LANES))
    counts, chunks = _alibi_chunk_lists(posf, slopes, score_bound, tq, tk)
    once = pl.Buffered(1)
    small = lambda a: pl.BlockSpec(a.shape, lambda b, i: (0,) * a.ndim)
    smem = pl.BlockSpec(memory_space=pltpu.SMEM)
    n_units = 2 * nq
    return pl.pallas_call(
        functools.partial(_flash_b_kernel, tq=tq, tk=tk, seq=seq, lam_init=lam_init),
        grid=(bsz, seq // tq),
        in_specs=[smem, smem, smem,
                  pl.BlockSpec((1, nq, LANES, tq), lambda b, i: (b, 0, 0, i)),
                  pl.BlockSpec((1, seq, nq * LANES), lambda b, i: (b, 0, k_slab0 // nq), pipeline_mode=once),
                  pl.BlockSpec((1, B_HEADS, B_VT_ROWS, seq), lambda b, i: (b, 0, 0, 0), pipeline_mode=once),
                  pl.BlockSpec((1, 1, tq), lambda b, i: (b, 0, i)),
                  pl.BlockSpec((1, seq, LANES), lambda b, i: (b, 0, 0), pipeline_mode=once),
                  small(lq1), small(lk1), small(lq2), small(lk2), small(subg)],
        out_specs=pl.BlockSpec((1, tq, B_HEADS * LANES), lambda b, i: (b, i, 0)),
        out_shape=jax.ShapeDtypeStruct((bsz, seq, B_HEADS * LANES), BF16),
        scratch_shapes=[pltpu.VMEM((1, tq), F32)] * n_units + [pltpu.VMEM((B_VT_ROWS, tq), F32)] * n_units,
        compiler_params=_params("parallel", "arbitrary"),
        name="flash_diff",
    )(slopes, counts, chunks, qt, y_main, vt, posq, posk, lq1, lk1, lq2, lk2, subg)


D_VT_ROWS = D_V + 16
D_GROUP = 8


def _flash_d_kernel(*refs, tq, tk, seq):
    n = D_GROUP
    q_refs, k_refs, vt_refs = refs[:n], refs[n:2 * n], refs[2 * n:3 * n]
    o_ref = refs[3 * n]
    scratch = refs[3 * n + 1:]
    m_refs, acc_refs = scratch[:n], scratch[n:]
    _flash_init(m_refs, acc_refs)
    qs = [q_ref[0, 0] for q_ref in q_refs]

    def score_fn(j):
        off = pl.multiple_of(j * tk, tk)
        return [jnp.dot(k_refs[e][0, pl.ds(off, tk), :], qs[e], preferred_element_type=F32) for e in range(n)]

    def value_fn(j):
        off = pl.multiple_of(j * tk, tk)
        return [vt_refs[e][0, 0, :, pl.ds(off, tk)] for e in range(n)]

    _flash_loop(seq // tk, score_fn, value_fn, m_refs, acc_refs)
    outs = []
    for e in range(n):
        acc = acc_refs[e][...]
        outs.append(acc[:D_V] / acc[D_V:D_V + 1])
    o_ref[0] = jnp.concatenate(outs, axis=0).T.astype(o_ref.dtype)


def _flash_d(qt, kd, vt):
    bsz, seq, _ = kd.shape
    tq = _tile(seq, 512)
    tk = _tile(seq, 512)
    n = D_GROUP
    once = pl.Buffered(1)
    q_spec = lambda e: pl.BlockSpec((1, 1, LANES, tq), lambda b, hg, i: (b, n * hg + e, 0, i))
    k_spec = lambda e: pl.BlockSpec((1, seq, LANES), lambda b, hg, i: (b, 0, n * hg + e), pipeline_mode=once)
    vt_spec = lambda e: pl.BlockSpec((1, 1, D_VT_ROWS, seq), lambda b, hg, i: (b, n * hg + e, 0, 0),
                                     pipeline_mode=once)
    return pl.pallas_call(
        functools.partial(_flash_d_kernel, tq=tq, tk=tk, seq=seq),
        grid=(bsz, D_HEADS // n, seq // tq),
        in_specs=[q_spec(e) for e in range(n)] + [k_spec(e) for e in range(n)]
                 + [vt_spec(e) for e in range(n)],
        out_specs=pl.BlockSpec((1, tq, n * D_V), lambda b, hg, i: (b, i, hg)),
        out_shape=jax.ShapeDtypeStruct((bsz, seq, D_HEADS * D_V), BF16),
        scratch_shapes=[pltpu.VMEM((1, tq), F32)] * n + [pltpu.VMEM((D_VT_ROWS, tq), F32)] * n,
        compiler_params=_params("parallel", "parallel", "arbitrary"),
        name="flash_latent",
    )(*([qt] * n + [kd] * n + [vt] * n))


def _merge_kernel(h_ref, x_ref, oa_ref, ob_ref, oc_ref, od_ref, wg_ref, wb_ref, wo_ref, out_ref, acc_scr):
    j = pl.program_id(1)

    @pl.when(j == 0)
    def _():
        acc_scr[...] = jnp.zeros(acc_scr.shape, F32)

    gate = jax.nn.sigmoid(jnp.dot(h_ref[...], wg_ref[...], preferred_element_type=F32))
    for br, o_ref in enumerate((oa_ref, ob_ref, oc_ref, od_ref)):
        @pl.when(j == br)
        def _(o_ref=o_ref):
            acc_scr[...] += gate * jnp.dot(o_ref[...], wb_ref[0], preferred_element_type=F32)

    @pl.when(j == N_BRANCH - 1)
    def _():
        out_ref[...] = x_ref[...] + jnp.dot(acc_scr[...].astype(BF16), wo_ref[...],
                                            preferred_element_type=F32)


def _merge(h2d, x2d, branch_outs, wg, wb, wo):
    t, d = x2d.shape
    tm = _tile(t, 512)
    bw = wb.shape[1]
    row = lambda width: pl.BlockSpec((tm, width), lambda i, j: (i, 0))
    return pl.pallas_call(
        _merge_kernel,
        grid=(t // tm, N_BRANCH),
        in_specs=[row(d), row(d), row(bw), row(bw), row(bw), row(bw),
                  pl.BlockSpec((d, d), lambda i, j: (0, j)),
                  pl.BlockSpec((1, bw, d), lambda i, j: (j, 0, 0)),
                  pl.BlockSpec((d, d), lambda i, j: (0, 0))],
        out_specs=row(d),
        out_shape=jax.ShapeDtypeStruct((t, d), F32),
        scratch_shapes=[pltpu.VMEM((tm, d), F32)],
        compiler_params=_params("parallel", "arbitrary"),
        name="gated_merge",
    )(h2d, x2d, *branch_outs, wg, wb, wo)


HALO = 8


def _ffn_kernel(x_ref, xp_ref, xn_ref, g_ref, wg_ref, wu_ref, cwg_ref, cwu_ref, cbg_ref, cbu_ref,
                wd_ref, out_ref, hn_scr, acc_scr, u_scr, *, tm, tiles_per_seq):
    i = pl.program_id(0)
    j = pl.program_id(1)
    rows = tm + 2 * HALO

    @pl.when(j == 0)
    def _():
        xe = jnp.concatenate([xp_ref[...], x_ref[...], xn_ref[...]], axis=0)
        ms = jnp.mean(xe * xe, axis=-1, keepdims=True)
        hn = xe * lax.rsqrt(ms + EPS) * g_ref[...]
        r = lax.broadcasted_iota(jnp.int32, (rows, 1), 0)
        first = (i % tiles_per_seq) == 0
        last = (i % tiles_per_seq) == tiles_per_seq - 1
        dead = ((r == HALO - 1) & first) | ((r == HALO + tm) & last)
        hn_scr[...] = jnp.where(dead, 0.0, hn).astype(BF16)
        acc_scr[...] = jnp.zeros(acc_scr.shape, F32)

    def conv(w_ref, cw_ref, cb_ref):
        u_scr[...] = jnp.dot(hn_scr[...], w_ref[...], preferred_element_type=F32)
        return (cw_ref[0:1, :] * u_scr[HALO - 1:HALO - 1 + tm, :] + cw_ref[1:2, :] * u_scr[HALO:HALO + tm, :]
                + cw_ref[2:3, :] * u_scr[HALO + 1:HALO + 1 + tm, :] + cb_ref[...])

    yg = conv(wg_ref, cwg_ref, cbg_ref)
    yu = conv(wu_ref, cwu_ref, cbu_ref)
    act = (jax.nn.silu(yg) * yu).astype(BF16)
    acc_scr[...] += jnp.dot(act, wd_ref[...], preferred_element_type=F32)

    @pl.when(j == pl.num_programs(1) - 1)
    def _():
        out_ref[...] = x_ref[...] + acc_scr[...]


def _ffn(x2d, seq, g, w_up, conv_w, conv_b, w_down):
    t, d = x2d.shape
    dff = w_down.shape[0]
    tm = _tile(seq, 512)
    tf = _tile(dff, 1408)
    nf = dff // tf
    nhb = t // HALO
    rb = tm // HALO
    return pl.pallas_call(
        functools.partial(_ffn_kernel, tm=tm, tiles_per_seq=seq // tm),
        grid=(t // tm, nf),
        in_specs=[pl.BlockSpec((tm, d), lambda i, j: (i, 0)),
                  pl.BlockSpec((HALO, d), lambda i, j: (jnp.maximum(i * rb - 1, 0), 0)),
                  pl.BlockSpec((HALO, d), lambda i, j: (jnp.minimum((i + 1) * rb, nhb - 1), 0)),
                  pl.BlockSpec((1, d), lambda i, j: (0, 0)),
                  pl.BlockSpec((d, tf), lambda i, j: (0, j)),
                  pl.BlockSpec((d, tf), lambda i, j: (0, nf + j)),
                  pl.BlockSpec((3, tf), lambda i, j: (0, j)),
                  pl.BlockSpec((3, tf), lambda i, j: (0, nf + j)),
                  pl.BlockSpec((1, tf), lambda i, j: (0, j)),
                  pl.BlockSpec((1, tf), lambda i, j: (0, nf + j)),
                  pl.BlockSpec((tf, d), lambda i, j: (j, 0))],
        out_specs=pl.BlockSpec((tm, d), lambda i, j: (i, 0)),
        out_shape=jax.ShapeDtypeStruct((t, d), F32),
        scratch_shapes=[pltpu.VMEM((tm + 2 * HALO, d), BF16), pltpu.VMEM((tm, d), F32),
                        pltpu.VMEM((tm + 2 * HALO, tf), F32)],
        compiler_params=_params("parallel", "arbitrary"),
        name="conv_mlp",
    )(x2d, x2d, x2d, g, w_up, w_up, conv_w, conv_w, conv_b, conv_b, w_down)


def _alibi_slopes(n):
    return 2.0 ** (-8.0 * np.arange(1, n + 1) / n)


def _pair_split_perm():
    cols = []
    for t in range(A_Q_HEADS // 2):
        for e in range(2):
            hd = t + (A_Q_HEADS // 2) * e
            cols.extend(range(hd * HEAD_DIM, (hd + 1) * HEAD_DIM))
    return np.asarray(cols, np.int32)


def _layer(x, pos, layer_idx, norm1_g, w_in, a_q_norm, a_k_norm, a_sink, b_q_norm, b_k_norm, b_lam_q1,
           b_lam_k1, b_lam_q2, b_lam_k2, b_subln_g, c_q_norm, c_k_norm, d_cq_norm, d_ckv_norm, d_w_uq,
           d_w_ukv, d_q_norm, d_k_norm, w_branch, w_o, norm2_g, ffn_w_up, ffn_conv_w, ffn_conv_b,
           ffn_w_down):
    bsz, seq, d = x.shape
    t = bsz * seq
    x2d = x.reshape(t, d)
    ng = len(C_PATTERNS)
    qscale = HEAD_DIM ** -0.5

    n_a_in = A_Q_HEADS * HEAD_DIM + 2 * LANES
    n_a = n_a_in + 2 * LANES
    n_b = 3 * 2 * B_HEADS * HEAD_DIM
    n_cg = 3 * C_HEADS * HEAD_DIM
    n_main = n_a + n_b + n_cg
    perm = _pair_split_perm()
    w_all = jnp.concatenate([w_in[:, :A_Q_HEADS * HEAD_DIM][:, perm],
                             w_in[:, A_Q_HEADS * HEAD_DIM:n_a_in],
                             jnp.zeros((d, n_a - n_a_in), w_in.dtype),
                             w_in[:, n_a_in:n_a_in + n_b + ng * n_cg]], axis=1).astype(BF16)
    ones = lambda n: jnp.ones((n,), F32)
    zeros = lambda n: jnp.zeros((n,), F32)
    rep = lambda gvec, n: jnp.tile(gvec.astype(F32), n)
    gains = [rep(a_q_norm, 8) * (qscale * LOG2E), rep(a_k_norm, 2), ones(128 + n_a - n_a_in),
             rep(b_q_norm, 8) * (qscale * LOG2E), rep(b_k_norm, 8), ones(512)]
    flags = [ones(512), ones(128), zeros(128 + n_a - n_a_in), ones(512), ones(512), zeros(512)]
    for _ in range(ng):
        gains += [rep(c_q_norm, 8) * (qscale * LOG2E), rep(c_k_norm, 8), ones(512)]
        flags += [ones(512), ones(512), zeros(512)]
    assert C_PATTERNS[0][1] == 1
    class_cols = [(r, n_main + (gi - 1) * n_cg, n_cg) for gi, (_, r) in enumerate(C_PATTERNS) if gi > 0]
    y_main, h2d, qt_b, vt_b, *y_classes = _in_proj(
        x2d, norm1_g[None, :].astype(F32), w_all, jnp.concatenate(gains)[None, :],
        jnp.concatenate(flags)[None, :], bsz, n_main, n_a, n_a + 2 * 2 * B_HEADS * HEAD_DIM, class_cols)
    y3 = y_main.reshape(bsz, seq, n_main)

    sl_a = _alibi_slopes(A_Q_HEADS)
    half = A_Q_HEADS // 2
    slopes_a = jnp.asarray(np.stack([sl_a[:half], sl_a[half:]], axis=1) * LOG2E, F32)
    sink_a = jnp.stack([a_sink[:half], a_sink[half:]], axis=1).astype(F32) * LOG2E
    (oa,) = _band_attention(y3, pos, A_RADIUS, slopes_a, sink_a, 0, 4, 5, 1, False)

    lam_init = 0.8 - 0.6 * math.exp(-0.3 * layer_idx)
    slopes_b = jnp.asarray(_alibi_slopes(B_HEADS) * LOG2E, F32)
    sb = n_a // LANES
    score_bound_b = 1.05 * HEAD_DIM * qscale * LOG2E * jnp.max(jnp.abs(b_q_norm)) * jnp.max(jnp.abs(b_k_norm))
    ob = _flash_b(y3, qt_b, vt_b, pos, slopes_b, score_bound_b, b_lam_q1.astype(F32), b_lam_k1.astype(F32),
                  b_lam_q2.astype(F32), b_lam_k2.astype(F32), b_subln_g[None, :].astype(F32), lam_init, sb + 4)

    sl_c = _alibi_slopes(ng * C_HEADS).reshape(ng, C_HEADS // 2, 2)
    c_outs, c_lses, dilations = [], [], []
    for gi, (window, r) in enumerate(C_PATTERNS):
        radius = window // (2 * r)
        slopes_c = jnp.asarray(sl_c[gi] * LOG2E, F32)
        if r == 1:
            slab0 = (n_a + n_b) // LANES
            o_g, lse_g = _band_attention(y3, pos, radius, slopes_c, None, slab0 // 4, slab0 + 4, slab0 + 8, 4, True)
        else:
            src = y_classes[gi - 1].reshape(bsz * r, seq // r, n_cg)
            pos_r = pos.reshape(bsz, seq // r, r).transpose(0, 2, 1).reshape(bsz * r, seq // r)
            o_g, lse_g = _band_attention(src, pos_r, radius, slopes_c, None, 0, 4, 8, 4, True)
        c_outs.append(o_g)
        c_lses.append(lse_g)
        dilations.append(r)
    oc = _c_merge(c_outs, c_lses, dilations, bsz, seq).reshape(t, -1)

    n0 = n_a_in + n_b + ng * n_cg
    kr_cols = jnp.zeros((d, LANES), w_in.dtype).at[:, D_NOPE:D_NOPE + D_ROPE].set(
        w_in[:, n0 + D_Q_RANK + D_KV_RANK:n0 + D_Q_RANK + D_KV_RANK + D_ROPE])
    w_d = jnp.concatenate([w_in[:, n0:n0 + D_Q_RANK + D_KV_RANK], kr_cols], axis=1).astype(BF16)
    dq = D_NOPE + D_ROPE
    pad_heads = lambda wm, width: jnp.pad(wm.reshape(wm.shape[0], D_HEADS, width),
                                          ((0, 0), (0, 0), (0, LANES - width))).reshape(wm.shape[0], -1)
    wuq = pad_heads(d_w_uq, dq).astype(BF16)
    ukv = d_w_ukv.reshape(D_KV_RANK, D_HEADS, D_NOPE + D_V)
    wuk = pad_heads(ukv[:, :, :D_NOPE].reshape(D_KV_RANK, -1), D_NOPE).astype(BF16)
    wuv = pad_heads(ukv[:, :, D_NOPE:].reshape(D_KV_RANK, -1), D_V).astype(BF16)
    gq = jnp.pad(d_q_norm.astype(F32) * (dq ** -0.5 * LOG2E), (0, LANES - dq))[None, :]
    gk = jnp.pad(d_k_norm.astype(F32), (0, LANES - dq))[None, :]
    halfr = D_ROPE // 2
    inv = ROPE_THETA ** (-np.arange(halfr, dtype=np.float32) / halfr)
    inv_tab = np.zeros((1, LANES), np.float32)
    inv_tab[0, D_NOPE:D_NOPE + halfr] = inv
    inv_tab[0, D_NOPE + halfr:D_NOPE + D_ROPE] = inv
    pos_col = pos.astype(F32).reshape(t, 1)
    qt_d, kd, vt_d = _d_proj(h2d, w_d, d_cq_norm[None, :].astype(F32), d_ckv_norm[None, :].astype(F32),
                             wuq, wuk, wuv, gq, gk, pos_col, jnp.asarray(inv_tab), bsz)
    od = _flash_d(qt_d, kd.reshape(bsz, seq, -1), vt_d)

    n_gate0 = n0 + D_Q_RANK + D_KV_RANK + D_ROPE
    w_gate = w_in[:, n_gate0:].astype(BF16)
    wb = jnp.concatenate([w_branch[0][perm][None], w_branch[1:]], axis=0).astype(BF16)
    x1 = _merge(h2d, x2d, [oa.reshape(t, -1), ob.reshape(t, -1), oc, od.reshape(t, -1)],
                w_gate, wb, w_o.astype(BF16))

    x2 = _ffn(x1, seq, norm2_g[None, :].astype(F32), ffn_w_up.astype(BF16), ffn_conv_w.astype(F32),
              ffn_conv_b[None, :].astype(F32), ffn_w_down.astype(BF16))
    return x2.reshape(bsz, seq, d)


def kernel(x, positions, norm1_g, w_in, a_q_norm, a_k_norm, a_sink, b_q_norm, b_k_norm, b_lam_q1,
           b_lam_k1, b_lam_q2, b_lam_k2, b_subln_g, c_q_norm, c_k_norm, d_cq_norm, d_ckv_norm, d_w_uq,
           d_w_ukv, d_q_norm, d_k_norm, w_branch, w_o, norm2_g, ffn_w_up, ffn_conv_w, ffn_conv_b,
           ffn_w_down):
    depth = w_in.shape[0]
    per_layer = (norm1_g, w_in, a_q_norm, a_k_norm, a_sink, b_q_norm, b_k_norm, b_lam_q1, b_lam_k1,
                 b_lam_q2, b_lam_k2, b_subln_g, c_q_norm, c_k_norm, d_cq_norm, d_ckv_norm, d_w_uq,
                 d_w_ukv, d_q_norm, d_k_norm, w_branch, w_o, norm2_g, ffn_w_up, ffn_conv_w, ffn_conv_b,
                 ffn_w_down)
    for layer in range(depth):
        x = _layer(x, positions, layer, *[p[layer] for p in per_layer])
    return x
```

```python
import functools
import math

import jax
import jax.numpy as jnp
import numpy as np
from jax import lax
from jax.experimental import pallas as pl
from jax.experimental.pallas import tpu as pltpu

F32 = jnp.float32
BF16 = jnp.bfloat16

LANES = 128
HEAD_DIM = 64
A_Q_HEADS = 8
A_RADIUS = 128
B_HEADS = 4
C_PATTERNS = ((128, 1), (512, 4), (2048, 16))
C_HEADS = 8
D_HEADS = 8
D_Q_RANK = 384
D_KV_RANK = 256
D_NOPE = 64
D_ROPE = 32
D_V = 64
ROPE_THETA = 10000.0
N_BRANCH = 4
BRANCH_WIDTH = 512
EPS = 1e-6
NEG = -1e30
LOG2E = math.log2(math.e)
VMEM_LIMIT_BYTES = 56 * 1024 * 1024

NT_DIMS = (((1,), (1,)), ((), ()))
TN_DIMS = (((0,), (0,)), ((), ()))


def _tile(n, pref):
    return pref if n % pref == 0 else n


def _params(*sem):
    return pltpu.CompilerParams(dimension_semantics=sem, vmem_limit_bytes=VMEM_LIMIT_BYTES)


SEG_CHUNK = 256
B_VT_ROWS = LANES + 16


def _seg_ones():
    return jnp.asarray(np.kron(np.eye(SEG_CHUNK // HEAD_DIM), np.ones((HEAD_DIM, HEAD_DIM))), BF16)


def _head_normed(y, nf_ref, cg_ref, ones_ref):
    y2 = (y * y).astype(BF16)
    seg = jnp.concatenate(
        [jnp.dot(y2[:, c * SEG_CHUNK:(c + 1) * SEG_CHUNK], ones_ref[...], preferred_element_type=F32)
         for c in range(y.shape[1] // SEG_CHUNK)], axis=1)
    rs = lax.rsqrt(seg * (1.0 / HEAD_DIM) + EPS)
    return y * (jnp.where(nf_ref[...] > 0.0, rs, 1.0) * cg_ref[...])


def _in_proj_kernel(x_ref, g_ref, w_ref, cg_ref, nf_ref, ones_ref, y_ref, h_ref, qt_ref, vt_ref, hn_scr,
                    *, qt_tile, vt_tile):
    j = pl.program_id(1)

    @pl.when(j == 0)
    def _():
        x = x_ref[...]
        ms = jnp.mean(x * x, axis=-1, keepdims=True)
        hn = (x * lax.rsqrt(ms + EPS) * g_ref[...]).astype(BF16)
        hn_scr[...] = hn
        h_ref[...] = hn

    y = _head_normed(jnp.dot(hn_scr[...], w_ref[...], preferred_element_type=F32), nf_ref, cg_ref, ones_ref)
    y_ref[...] = y.astype(y_ref.dtype)

    @pl.when(j == qt_tile)
    def _():
        for s in range(B_HEADS):
            qt_ref[0, s] = y[:, s * LANES:(s + 1) * LANES].T.astype(qt_ref.dtype)

    @pl.when(j == vt_tile)
    def _():
        tm = y.shape[0]
        ones_row = (lax.broadcasted_iota(jnp.int32, (B_VT_ROWS - LANES, tm), 0) == 0).astype(vt_ref.dtype)
        for s in range(B_HEADS):
            vt_ref[0, s, :LANES, :] = y[:, s * LANES:(s + 1) * LANES].T.astype(vt_ref.dtype)
            vt_ref[0, s, LANES:, :] = ones_row


def _in_proj(x2d, g, w, colgain, normflag, bsz, q_col, v_col):
    t, d = x2d.shape
    n = w.shape[1]
    seq = t // bsz
    tm = _tile(seq, 1024)
    tn = _tile(n, 1024)
    assert q_col % tn == 0 and v_col % tn == 0
    nper = seq // tm
    return pl.pallas_call(
        functools.partial(_in_proj_kernel, qt_tile=q_col // tn, vt_tile=v_col // tn),
        grid=(t // tm, n // tn),
        in_specs=[
            pl.BlockSpec((tm, d), lambda i, j: (i, 0)),
            pl.BlockSpec((1, d), lambda i, j: (0, 0)),
            pl.BlockSpec((d, tn), lambda i, j: (0, j)),
            pl.BlockSpec((1, tn), lambda i, j: (0, j)),
            pl.BlockSpec((1, tn), lambda i, j: (0, j)),
            pl.BlockSpec((SEG_CHUNK, SEG_CHUNK), lambda i, j: (0, 0)),
        ],
        out_specs=[
            pl.BlockSpec((tm, tn), lambda i, j: (i, j)),
            pl.BlockSpec((tm, d), lambda i, j: (i, 0)),
            pl.BlockSpec((1, B_HEADS, LANES, tm), lambda i, j: (i // nper, 0, 0, i % nper)),
            pl.BlockSpec((1, B_HEADS, B_VT_ROWS, tm), lambda i, j: (i // nper, 0, 0, i % nper)),
        ],
        out_shape=[jax.ShapeDtypeStruct((t, n), BF16), jax.ShapeDtypeStruct((t, d), BF16),
                   jax.ShapeDtypeStruct((bsz, B_HEADS, LANES, seq), BF16),
                   jax.ShapeDtypeStruct((bsz, B_HEADS, B_VT_ROWS, seq), BF16)],
        scratch_shapes=[pltpu.VMEM((tm, d), BF16)],
        compiler_params=_params("parallel", "arbitrary"),
        name="in_proj",
    )(x2d, g, w, colgain, normflag, _seg_ones())


def _class_proj_kernel(h_ref, w_ref, cg_ref, nf_ref, ones_ref, *rest, class_groups):
    class_refs, y_scr = rest[:len(class_groups)], rest[len(class_groups)]
    j = pl.program_id(1)
    y = _head_normed(jnp.dot(h_ref[...], w_ref[...], preferred_element_type=F32), nf_ref, cg_ref, ones_ref)
    tm, tn = y.shape
    for s in range(tn // LANES):
        y_scr[s] = y[:, s * LANES:(s + 1) * LANES]
    for c_ref, (r, tile0, n_tiles) in zip(class_refs, class_groups):
        @pl.when((j >= tile0) & (j < tile0 + n_tiles))
        def _(c_ref=c_ref, r=r):
            for c in range(r):
                for s in range(tn // LANES):
                    c_ref[0, c, :, s * LANES:(s + 1) * LANES] = (
                        y_scr[s, pl.ds(c, tm // r, stride=r), :].astype(c_ref.dtype))


def _class_proj(h2d, w, colgain, normflag, bsz, class_cols):
    t, d = h2d.shape
    n = w.shape[1]
    seq = t // bsz
    tm = _tile(seq, 1024)
    tn = 512
    assert n % tn == 0 and all(st % tn == 0 and wd % tn == 0 and tm % r == 0 for r, st, wd in class_cols)
    nper = seq // tm
    class_groups = tuple((r, start // tn, width // tn) for r, start, width in class_cols)

    def class_spec(r, tile0, n_tiles):
        return pl.BlockSpec((1, r, tm // r, tn),
                            lambda i, j: (i // nper, 0, i % nper, jnp.clip(j - tile0, 0, n_tiles - 1)))

    return pl.pallas_call(
        functools.partial(_class_proj_kernel, class_groups=class_groups),
        grid=(t // tm, n // tn),
        in_specs=[
            pl.BlockSpec((tm, d), lambda i, j: (i, 0)),
            pl.BlockSpec((d, tn), lambda i, j: (0, j)),
            pl.BlockSpec((1, tn), lambda i, j: (0, j)),
            pl.BlockSpec((1, tn), lambda i, j: (0, j)),
            pl.BlockSpec((SEG_CHUNK, SEG_CHUNK), lambda i, j: (0, 0)),
        ],
        out_specs=[class_spec(*grp) for grp in class_groups],
        out_shape=[jax.ShapeDtypeStruct((bsz, r, seq // r, width), BF16) for r, _, width in class_cols],
        scratch_shapes=[pltpu.VMEM((tn // LANES, tm, LANES), F32)],
        compiler_params=_params("parallel", "arbitrary"),
        name="class_proj",
    )(h2d, w, colgain, normflag, _seg_ones())


D_ROT_LO = 48
D_ROT_HI = D_ROT_LO + LANES // 2


def _latent_lane_of_dim():
    half = D_ROPE // 2
    nope = list(range(D_ROT_LO)) + list(range(LANES // 2, LANES // 2 + D_NOPE - D_ROT_LO))
    return np.asarray(nope + list(range(D_ROT_LO, D_ROT_LO + half)) + list(range(D_ROT_HI, D_ROT_HI + half)))


def _slab_sums(x2_bf16, ones_ref):
    return jnp.concatenate(
        [jnp.dot(x2_bf16[:, c * SEG_CHUNK:(c + 1) * SEG_CHUNK], ones_ref[...], preferred_element_type=F32)
         for c in range(x2_bf16.shape[1] // SEG_CHUNK)], axis=1)


def _d_proj_kernel(h_ref, wd_ref, gcq_ref, gckv_ref, wuq_ref, wuk_ref, wuv_ref, gq_ref, gk_ref,
                   pos_ref, inv_ref, ones_ref, qt_ref, k_ref, vt_ref):
    y = jnp.dot(h_ref[...], wd_ref[...], preferred_element_type=F32)
    cq = y[:, :D_Q_RANK]
    ckv = y[:, D_Q_RANK:D_Q_RANK + D_KV_RANK]
    kr = y[:, D_Q_RANK + D_KV_RANK:]
    cqn = (cq * lax.rsqrt(jnp.mean(cq * cq, axis=-1, keepdims=True) + EPS) * gcq_ref[...]).astype(BF16)
    ckvn = (ckv * lax.rsqrt(jnp.mean(ckv * ckv, axis=-1, keepdims=True) + EPS) * gckv_ref[...]).astype(BF16)
    q = jnp.dot(cqn, wuq_ref[...], preferred_element_type=F32)
    k = jnp.dot(ckvn, wuk_ref[...], preferred_element_type=F32) + jnp.concatenate([kr] * D_HEADS, axis=1)
    v = jnp.dot(ckvn, wuv_ref[...], preferred_element_type=F32)
    inv_w = 1.0 / (D_NOPE + D_ROPE)
    q = q * lax.rsqrt(_slab_sums((q * q).astype(BF16), ones_ref) * inv_w + EPS)
    k = k * lax.rsqrt(_slab_sums((k * k).astype(BF16), ones_ref) * inv_w + EPS)

    lane = lax.broadcasted_iota(jnp.int32, (1, LANES), 1)
    ang = pos_ref[...] * inv_ref[...]
    half = D_ROPE // 2
    lo = (lane >= D_ROT_LO) & (lane < D_ROT_LO + half)
    hi = (lane >= D_ROT_HI) & (lane < D_ROT_HI + half)
    c_tab = jnp.where(lo | hi, jnp.cos(ang), 1.0)
    sn = jnp.sin(ang)
    s_tab = jnp.where(lo, -sn, jnp.where(hi, sn, 0.0))
    rows = vt_ref.shape[2]
    for hd in range(D_HEADS):
        sl = slice(hd * LANES, (hd + 1) * LANES)
        qs = q[:, sl] * gq_ref[...]
        qt_ref[0, hd] = (qs * c_tab + pltpu.roll(qs, LANES // 2, 1) * s_tab).T.astype(qt_ref.dtype)
        ks = k[:, sl] * gk_ref[...]
        k_ref[:, sl] = (ks * c_tab + pltpu.roll(ks, LANES // 2, 1) * s_tab).astype(k_ref.dtype)
        vt_ref[0, hd] = jnp.where(lane == D_V, 1.0, v[:, sl]).T[:rows].astype(vt_ref.dtype)


def _d_proj(h2d, wd, gcq, gckv, wuq, wuk, wuv, gq, gk, pos_col, inv_tab, bsz, vt_rows):
    t, d = h2d.shape
    seq = t // bsz
    tm = _tile(seq, 512)
    nper = seq // tm
    nd = D_HEADS * LANES
    slab_ones = jnp.asarray(np.kron(np.eye(SEG_CHUNK // LANES), np.ones((LANES, LANES))), BF16)
    full = lambda a: pl.BlockSpec(a.shape, lambda i: (0,) * a.ndim)
    return pl.pallas_call(
        _d_proj_kernel,
        grid=(t // tm,),
        in_specs=[pl.BlockSpec((tm, d), lambda i: (i, 0)), full(wd), full(gcq), full(gckv), full(wuq),
                  full(wuk), full(wuv), full(gq), full(gk), pl.BlockSpec((tm, 1), lambda i: (i, 0)),
                  full(inv_tab), full(slab_ones)],
        out_specs=[pl.BlockSpec((1, D_HEADS, LANES, tm), lambda i: (i // nper, 0, 0, i % nper)),
                   pl.BlockSpec((tm, nd), lambda i: (i, 0)),
                   pl.BlockSpec((1, D_HEADS, vt_rows, tm), lambda i: (i // nper, 0, 0, i % nper))],
        out_shape=[jax.ShapeDtypeStruct((bsz, D_HEADS, LANES, seq), BF16),
                   jax.ShapeDtypeStruct((t, nd), BF16),
                   jax.ShapeDtypeStruct((bsz, D_HEADS, vt_rows, seq), BF16)],
        compiler_params=_params("parallel"),
        name="d_proj",
    )(h2d, wd, gcq, gckv, wuq, wuk, wuv, gq, gk, pos_col, inv_tab, slab_ones)


N_QSLAB = 4
BAND_GQ = 128


def _band_kernel(slope_ref, sink_ref, q_ref, kp_ref, kc_ref, kn_ref, vp_ref, vc_ref, vn_ref,
                 posq_ref, posw_ref, *out_refs, radius, tq, gq, length, nkv, has_sink, want_lse):
    i = pl.program_id(1)
    kw = jnp.concatenate([kp_ref[0], kc_ref[0], kn_ref[0]], axis=0)
    vw = jnp.concatenate([vp_ref[0], vc_ref[0], vn_ref[0]], axis=0)
    wg = gq + 2 * radius
    c_io = lax.broadcasted_iota(jnp.int32, (wg, gq), 0)
    r_io = lax.broadcasted_iota(jnp.int32, (wg, gq), 1)
    in_band = jnp.abs(c_io - radius - r_io) <= radius
    lane = lax.broadcasted_iota(jnp.int32, (gq, LANES), 1)
    row = lax.broadcasted_iota(jnp.int32, (LANES, gq), 0)
    for g in range(tq // gq):
        rows = slice(g * gq, g * gq + wg)
        cols = slice(g * gq, (g + 1) * gq)
        jabs = i * tq + g * gq - radius + c_io
        mask = in_band & (jabs >= 0) & (jabs < length)
        dist = jnp.abs(posw_ref[0, 0, rows, :] - posq_ref[0, :, cols])
        for t in range(N_QSLAB):
            q = q_ref[0, cols, t * LANES:(t + 1) * LANES]
            kt = t if nkv == N_QSLAB else 0
            ks = kw[rows, kt * LANES:(kt + 1) * LANES]
            vs = vw[rows, kt * LANES:(kt + 1) * LANES]
            outs, lses = [], []
            for e in range(2):
                in_half = (lane >= HEAD_DIM) if e else (lane < HEAD_DIM)
                qe = jnp.where(in_half, q, jnp.zeros_like(q))
                s = lax.dot_general(ks, qe, NT_DIMS, preferred_element_type=F32)
                s = jnp.where(mask, s - slope_ref[t, e] * dist, NEG)
                m = jnp.max(s, axis=0, keepdims=True)
                if has_sink:
                    sk = sink_ref[t, e]
                    m = jnp.maximum(m, sk)
                p = jnp.exp2(s - m)
                den = jnp.sum(p, axis=0, keepdims=True)
                if has_sink:
                    den = den + jnp.exp2(sk - m)
                ot = lax.dot_general(vs, p.astype(BF16), TN_DIMS, preferred_element_type=F32)
                outs.append(ot / den)
                lses.append((m + jnp.log2(den)) * (1.0 / LOG2E))
            sl = slice(t * LANES, (t + 1) * LANES)
            out_refs[0][0, cols, sl] = jnp.where(row < HEAD_DIM, outs[0], outs[1]).T.astype(out_refs[0].dtype)
            if want_lse:
                out_refs[1][0, cols, sl] = jnp.where(row < HEAD_DIM, lses[0], lses[1]).T


def _band_attention(src, pos, radius, slopes, sink, q_blk, k_slab0, v_slab0, nkv, want_lse):
    bn, length, _ = src.shape
    tq = _tile(length, 512)
    nb = length // tq
    rb = tq // radius
    nrb = length // radius
    w = tq + 2 * radius
    kvw = nkv * LANES
    idx = jnp.clip(jnp.arange(nb)[:, None] * tq - radius + jnp.arange(w)[None, :], 0, length - 1)
    posf = pos.astype(F32)
    posw = posf[:, idx][:, :, :, None]
    posq = posf[:, None, :]
    has_sink = sink is not None
    if sink is None:
        sink = jnp.zeros((N_QSLAB, 2), F32)

    def halo(slab0, which):
        blk = slab0 // nkv
        if which == 0:
            return pl.BlockSpec((1, radius, kvw), lambda b, i: (b, jnp.maximum(i * rb - 1, 0), blk))
        if which == 1:
            return pl.BlockSpec((1, tq, kvw), lambda b, i: (b, i, blk))
        return pl.BlockSpec((1, radius, kvw), lambda b, i: (b, jnp.minimum((i + 1) * rb, nrb - 1), blk))

    smem = pl.BlockSpec(memory_space=pltpu.SMEM)
    qw = N_QSLAB * LANES
    out_specs = [pl.BlockSpec((1, tq, qw), lambda b, i: (b, i, 0))]
    out_shape = [jax.ShapeDtypeStruct((bn, length, qw), BF16)]
    if want_lse:
        out_specs.append(pl.BlockSpec((1, tq, qw), lambda b, i: (b, i, 0)))
        out_shape.append(jax.ShapeDtypeStruct((bn, length, qw), F32))
    return pl.pallas_call(
        functools.partial(_band_kernel, radius=radius, tq=tq, gq=min(BAND_GQ, tq), length=length, nkv=nkv,
                          has_sink=has_sink, want_lse=want_lse),
        grid=(bn, nb),
        in_specs=[smem, smem,
                  pl.BlockSpec((1, tq, qw), lambda b, i: (b, i, q_blk)),
                  halo(k_slab0, 0), halo(k_slab0, 1), halo(k_slab0, 2),
                  halo(v_slab0, 0), halo(v_slab0, 1), halo(v_slab0, 2),
                  pl.BlockSpec((1, 1, tq), lambda b, i: (b, 0, i)),
                  pl.BlockSpec((1, 1, w, 1), lambda b, i: (b, i, 0, 0))],
        out_specs=out_specs,
        out_shape=out_shape,
        compiler_params=_params("parallel", "arbitrary"),
        name="band_attn",
    )(slopes, sink, src, src, src, src, src, src, src, posq, posw)


def _c_merge_kernel(*refs, dilations, tm):
    ng = len(dilations)
    o_refs, l_refs, out_ref = refs[:ng], refs[ng:2 * ng], refs[2 * ng]
    o_scr, l_scr = refs[2 * ng + 1:]

    def natural(ref, scr, r):
        if r == 1:
            return ref[0].astype(F32)
        nslab = scr.shape[0]
        for c in range(r):
            blk = ref[0, c].astype(F32)
            for s in range(nslab):
                scr[s, pl.ds(c, tm // r, stride=r), :] = blk[:, s * LANES:(s + 1) * LANES]
        return jnp.concatenate([scr[s] for s in range(nslab)], axis=1)

    ls = [natural(l_refs[g], l_scr, r) for g, r in enumerate(dilations)]
    m = functools.reduce(jnp.maximum, ls)
    ws = [jnp.exp(l - m) for l in ls]
    num = sum(w * natural(o_refs[g], o_scr, r) for g, (w, r) in enumerate(zip(ws, dilations)))
    out_ref[0] = (num / sum(ws)).astype(out_ref.dtype)


def _c_merge(outs, lses, dilations, bsz, seq):
    n = outs[0].shape[-1]
    tm = _tile(seq, 512)
    views, specs = [], []
    for arrs in (outs, lses):
        for a, r in zip(arrs, dilations):
            if r == 1:
                views.append(a.reshape(bsz, seq, n))
                specs.append(pl.BlockSpec((1, tm, n), lambda b, i: (b, i, 0)))
            else:
                views.append(a.reshape(bsz, r, seq // r, n))
                specs.append(pl.BlockSpec((1, r, tm // r, n), lambda b, i: (b, 0, i, 0)))
    return pl.pallas_call(
        functools.partial(_c_merge_kernel, dilations=tuple(dilations), tm=tm),
        grid=(bsz, seq // tm),
        in_specs=specs,
        out_specs=pl.BlockSpec((1, tm, n), lambda b, i: (b, i, 0)),
        out_shape=jax.ShapeDtypeStruct((bsz, seq, n), BF16),
        scratch_shapes=[pltpu.VMEM((n // LANES, tm, LANES), F32), pltpu.VMEM((n // LANES, tm, LANES), F32)],
        compiler_params=_params("parallel", "parallel"),
        name="c_merge",
    )(*views)


def _flash_loop(n_trips, score_fn, value_fn, m_refs, acc_refs, chunk_of=lambda trip: trip):
    def body(trip, carry):
        j = chunk_of(trip)
        scores = score_fn(j)
        for u, (s, vt_c) in enumerate(zip(scores, value_fn(j))):
            m_old = m_refs[u][...]
            m_new = jnp.maximum(m_old, jnp.max(s, axis=0, keepdims=True))
            alpha = jnp.exp2(m_old - m_new)
            p = jnp.exp2(s - m_new).astype(BF16)
            acc_refs[u][...] = alpha * acc_refs[u][...] + jnp.dot(vt_c, p, preferred_element_type=F32)
            m_refs[u][...] = m_new
        return carry

    lax.fori_loop(0, n_trips, body, 0)


def _flash_init(m_refs, acc_refs):
    for m_ref, acc_ref in zip(m_refs, acc_refs):
        m_ref[...] = jnp.full(m_ref.shape, NEG, F32)
        acc_ref[...] = jnp.zeros(acc_ref.shape, F32)


B_HEAD_GROUPS = ((2, 3), (1,), (0,))


def _flash_b_kernel(slope_ref, cnt_ref, lst_ref, qt_ref, k_ref, vt_ref, posq_ref, posk_ref, lq1_ref, lk1_ref,
                    lq2_ref, lk2_ref, subg_ref, o_ref, *scratch, tq, tk, seq, lam_init):
    n_units = 2 * B_HEADS
    n_chunks = seq // tk
    m_refs, acc_refs = scratch[:n_units], scratch[n_units:]
    _flash_init(m_refs, acc_refs)
    row = lax.broadcasted_iota(jnp.int32, (LANES, tq), 0)
    slab_of = [2 * (u % 2) + (u // 2) // 2 for u in range(n_units)]
    qs = []
    for u in range(n_units):
        qm = qt_ref[0, slab_of[u]]
        qs.append(jnp.where((row >= HEAD_DIM) if (u // 2) % 2 else (row < HEAD_DIM), qm, jnp.zeros_like(qm)))
    posq = posq_ref[0]
    tile = pl.program_id(0) * pl.num_programs(1) + pl.program_id(1)

    for g, heads in enumerate(B_HEAD_GROUPS):
        units = [2 * hd + mp for hd in heads for mp in range(2)]
        row0 = (tile * len(B_HEAD_GROUPS) + g)

        def score_fn(j, heads=heads):
            off = pl.multiple_of(j * tk, tk)
            posk = posk_ref[0, pl.ds(off, tk), :]
            dist = jnp.abs(jnp.concatenate([posk] * (tq // LANES), axis=1) - posq)
            k_c = k_ref[0, pl.ds(off, tk), :]
            scores = []
            for hd in heads:
                bias = slope_ref[hd] * dist
                for u in (2 * hd, 2 * hd + 1):
                    sl = slab_of[u]
                    scores.append(jnp.dot(k_c[:, sl * LANES:(sl + 1) * LANES], qs[u],
                                          preferred_element_type=F32) - bias)
            return scores

        def value_fn(j, heads=heads):
            off = pl.multiple_of(j * tk, tk)
            return [vt_ref[0, hd, :, pl.ds(off, tk)] for hd in heads for _ in range(2)]

        _flash_loop(cnt_ref[row0], score_fn, value_fn, [m_refs[u] for u in units], [acc_refs[u] for u in units],
                    chunk_of=lambda trip, row0=row0: lst_ref[row0 * n_chunks + trip])

    def normalised(u):
        acc = acc_refs[u][...]
        return acc[:LANES] / acc[LANES:LANES + 1]

    for hd in range(B_HEADS):
        u0, u1 = 2 * hd, 2 * hd + 1
        lam1 = jnp.sum(lq1_ref[hd:hd + 1, :] * lk1_ref[hd:hd + 1, :], axis=-1, keepdims=True)
        lam2 = jnp.sum(lq2_ref[hd:hd + 1, :] * lk2_ref[hd:hd + 1, :], axis=-1, keepdims=True)
        lam = jnp.exp(lam1) - jnp.exp(lam2) + lam_init
        a = (normalised(u0) - lam * normalised(u1)).T
        ms = jnp.mean(a * a, axis=-1, keepdims=True)
        out = a * lax.rsqrt(ms + EPS) * subg_ref[...] * (1.0 - lam_init)
        o_ref[0, :, hd * LANES:(hd + 1) * LANES] = out.astype(o_ref.dtype)


UNDERFLOW_LOG2 = 150.0


def _alibi_chunk_lists(posf, slopes, score_bound, tq, tk):
    bsz, seq = posf.shape
    qmin, qmax = posf.reshape(bsz, -1, tq).min(-1), posf.reshape(bsz, -1, tq).max(-1)
    kmin, kmax = posf.reshape(bsz, -1, tk).min(-1), posf.reshape(bsz, -1, tk).max(-1)
    dmin = jnp.maximum(jnp.maximum(qmin[:, :, None] - kmax[:, None, :], kmin[:, None, :] - qmax[:, :, None]), 0.0)
    needed = slopes[None, None, :, None] * dmin[:, :, None, :] < 2.0 * score_bound + UNDERFLOW_LOG2
    group = jnp.stack([functools.reduce(jnp.logical_or, [needed[:, :, hd] for hd in heads])
                       for heads in B_HEAD_GROUPS], axis=2)
    order = jnp.argsort(jnp.logical_not(group), axis=-1, stable=True).astype(jnp.int32)
    return group.sum(-1).astype(jnp.int32).reshape(-1), order.reshape(-1)


def _flash_b(y_main, qt, vt, pos, slopes, score_bound, lq1, lk1, lq2, lk2, subg, lam_init, k_slab0):
    bsz, seq, _ = y_main.shape
    tq = _tile(seq, 512)
    tk = _tile(seq, 512)
    nq = B_HEADS
    posf = pos.astype(F32)
    posq = posf[:, None, :]
    posk = jnp.broadcast_to(posf[:, :, None], (bsz, seq, LANES))
    counts, chunks = _alibi_chunk_lists(posf, slopes, score_bound, tq, tk)
    once = pl.Buffered(1)
    small = lambda a: pl.BlockSpec(a.shape, lambda b, i: (0,) * a.ndim)
    smem = pl.BlockSpec(memory_space=pltpu.SMEM)
    n_units = 2 * nq
    return pl.pallas_call(
        functools.partial(_flash_b_kernel, tq=tq, tk=tk, seq=seq, lam_init=lam_init),
        grid=(bsz, seq // tq),
        in_specs=[smem, smem, smem,
                  pl.BlockSpec((1, nq, LANES, tq), lambda b, i: (b, 0, 0, i)),
                  pl.BlockSpec((1, seq, nq * LANES), lambda b, i: (b, 0, k_slab0 // nq), pipeline_mode=once),
                  pl.BlockSpec((1, B_HEADS, B_VT_ROWS, seq), lambda b, i: (b, 0, 0, 0), pipeline_mode=once),
                  pl.BlockSpec((1, 1, tq), lambda b, i: (b, 0, i)),
                  pl.BlockSpec((1, seq, LANES), lambda b, i: (b, 0, 0), pipeline_mode=once),
                  small(lq1), small(lk1), small(lq2), small(lk2), small(subg)],
        out_specs=pl.BlockSpec((1, tq, B_HEADS * LANES), lambda b, i: (b, i, 0)),
        out_shape=jax.ShapeDtypeStruct((bsz, seq, B_HEADS * LANES), BF16),
        scratch_shapes=[pltpu.VMEM((1, tq), F32)] * n_units + [pltpu.VMEM((B_VT_ROWS, tq), F32)] * n_units,
        compiler_params=_params("parallel", "arbitrary"),
        name="flash_diff",
    )(slopes, counts, chunks, qt, y_main, vt, posq, posk, lq1, lk1, lq2, lk2, subg)


D_VT_ROWS = D_V + 16
D_GROUP = 8


def _flash_d_kernel(*refs, tq, tk, seq):
    n = D_GROUP
    q_refs, k_refs, vt_refs = refs[:n], refs[n:2 * n], refs[2 * n:3 * n]
    o_ref = refs[3 * n]
    scratch = refs[3 * n + 1:]
    m_refs, acc_refs = scratch[:n], scratch[n:]
    _flash_init(m_refs, acc_refs)
    qs = [q_ref[0, 0] for q_ref in q_refs]

    def score_fn(j):
        off = pl.multiple_of(j * tk, tk)
        return [jnp.dot(k_refs[e][0, pl.ds(off, tk), :], qs[e], preferred_element_type=F32) for e in range(n)]

    def value_fn(j):
        off = pl.multiple_of(j * tk, tk)
        return [vt_refs[e][0, 0, :, pl.ds(off, tk)] for e in range(n)]

    _flash_loop(seq // tk, score_fn, value_fn, m_refs, acc_refs)
    outs = []
    for e in range(n):
        acc = acc_refs[e][...]
        outs.append(acc[:D_V] / acc[D_V:D_V + 1])
    o_ref[0] = jnp.concatenate(outs, axis=0).T.astype(o_ref.dtype)


def _flash_d(qt, kd, vt):
    bsz, seq, _ = kd.shape
    tq = _tile(seq, 512)
    tk = _tile(seq, 512)
    n = D_GROUP
    once = pl.Buffered(1)
    q_spec = lambda e: pl.BlockSpec((1, 1, LANES, tq), lambda b, hg, i: (b, n * hg + e, 0, i))
    k_spec = lambda e: pl.BlockSpec((1, seq, LANES), lambda b, hg, i: (b, 0, n * hg + e), pipeline_mode=once)
    vt_rows = vt.shape[2]
    vt_spec = lambda e: pl.BlockSpec((1, 1, vt_rows, seq), lambda b, hg, i: (b, n * hg + e, 0, 0),
                                     pipeline_mode=once)
    return pl.pallas_call(
        functools.partial(_flash_d_kernel, tq=tq, tk=tk, seq=seq),
        grid=(bsz, D_HEADS // n, seq // tq),
        in_specs=[q_spec(e) for e in range(n)] + [k_spec(e) for e in range(n)]
                 + [vt_spec(e) for e in range(n)],
        out_specs=pl.BlockSpec((1, tq, n * D_V), lambda b, hg, i: (b, i, hg)),
        out_shape=jax.ShapeDtypeStruct((bsz, seq, D_HEADS * D_V), BF16),
        scratch_shapes=[pltpu.VMEM((1, tq), F32)] * n + [pltpu.VMEM((vt_rows, tq), F32)] * n,
        compiler_params=_params("parallel", "parallel", "arbitrary"),
        name="flash_latent",
    )(*([qt] * n + [kd] * n + [vt] * n))


def _merge_kernel(h_ref, x_ref, oa_ref, ob_ref, oc_ref, od_ref, wg_ref, wb_ref, wo_ref, out_ref, acc_scr):
    j = pl.program_id(1)

    @pl.when(j == 0)
    def _():
        acc_scr[...] = jnp.zeros(acc_scr.shape, F32)

    gate = jax.nn.sigmoid(jnp.dot(h_ref[...], wg_ref[...], preferred_element_type=F32))
    for br, o_ref in enumerate((oa_ref, ob_ref, oc_ref, od_ref)):
        @pl.when(j == br)
        def _(o_ref=o_ref):
            acc_scr[...] += gate * jnp.dot(o_ref[...], wb_ref[0], preferred_element_type=F32)

    @pl.when(j == N_BRANCH - 1)
    def _():
        out_ref[...] = x_ref[...] + jnp.dot(acc_scr[...].astype(BF16), wo_ref[...],
                                            preferred_element_type=F32)


def _merge(h2d, x2d, branch_outs, wg, wb, wo):
    t, d = x2d.shape
    tm = _tile(t, 512)
    bw = wb.shape[1]
    row = lambda width: pl.BlockSpec((tm, width), lambda i, j: (i, 0))
    return pl.pallas_call(
        _merge_kernel,
        grid=(t // tm, N_BRANCH),
        in_specs=[row(d), row(d), row(bw), row(bw), row(bw), row(bw),
                  pl.BlockSpec((d, d), lambda i, j: (0, j)),
                  pl.BlockSpec((1, bw, d), lambda i, j: (j, 0, 0)),
                  pl.BlockSpec((d, d), lambda i, j: (0, 0))],
        out_specs=row(d),
        out_shape=jax.ShapeDtypeStruct((t, d), F32),
        scratch_shapes=[pltpu.VMEM((tm, d), F32)],
        compiler_params=_params("parallel", "arbitrary"),
        name="gated_merge",
    )(h2d, x2d, *branch_outs, wg, wb, wo)


HALO = 8


def _ffn_kernel(x_ref, xp_ref, xn_ref, g_ref, wg_ref, wu_ref, cwg_ref, cwu_ref, cbg_ref, cbu_ref,
                wd_ref, out_ref, hn_scr, acc_scr, u_scr, *, tm, tiles_per_seq):
    i = pl.program_id(0)
    j = pl.program_id(1)
    rows = tm + 2 * HALO

    @pl.when(j == 0)
    def _():
        xe = jnp.concatenate([xp_ref[...], x_ref[...], xn_ref[...]], axis=0)
        ms = jnp.mean(xe * xe, axis=-1, keepdims=True)
        hn = xe * lax.rsqrt(ms + EPS) * g_ref[...]
        r = lax.broadcasted_iota(jnp.int32, (rows, 1), 0)
        first = (i % tiles_per_seq) == 0
        last = (i % tiles_per_seq) == tiles_per_seq - 1
        dead = ((r == HALO - 1) & first) | ((r == HALO + tm) & last)
        hn_scr[...] = jnp.where(dead, 0.0, hn).astype(BF16)
        acc_scr[...] = jnp.zeros(acc_scr.shape, F32)

    def conv(w_ref, cw_ref, cb_ref):
        u_scr[...] = jnp.dot(hn_scr[...], w_ref[...], preferred_element_type=F32)
        return (cw_ref[0:1, :] * u_scr[HALO - 1:HALO - 1 + tm, :] + cw_ref[1:2, :] * u_scr[HALO:HALO + tm, :]
                + cw_ref[2:3, :] * u_scr[HALO + 1:HALO + 1 + tm, :] + cb_ref[...])

    yg = conv(wg_ref, cwg_ref, cbg_ref)
    yu = conv(wu_ref, cwu_ref, cbu_ref)
    act = (jax.nn.silu(yg) * yu).astype(BF16)
    acc_scr[...] += jnp.dot(act, wd_ref[...], preferred_element_type=F32)

    @pl.when(j == pl.num_programs(1) - 1)
    def _():
        out_ref[...] = x_ref[...] + acc_scr[...]


def _ffn(x2d, seq, g, w_up, conv_w, conv_b, w_down):
    t, d = x2d.shape
    dff = w_down.shape[0]
    tm = _tile(seq, 512)
    tf = _tile(dff, 1408)
    nf = dff // tf
    nhb = t // HALO
    rb = tm // HALO
    return pl.pallas_call(
        functools.partial(_ffn_kernel, tm=tm, tiles_per_seq=seq // tm),
        grid=(t // tm, nf),
        in_specs=[pl.BlockSpec((tm, d), lambda i, j: (i, 0)),
                  pl.BlockSpec((HALO, d), lambda i, j: (jnp.maximum(i * rb - 1, 0), 0)),
                  pl.BlockSpec((HALO, d), lambda i, j: (jnp.minimum((i + 1) * rb, nhb - 1), 0)),
                  pl.BlockSpec((1, d), lambda i, j: (0, 0)),
                  pl.BlockSpec((d, tf), lambda i, j: (0, j)),
                  pl.BlockSpec((d, tf), lambda i, j: (0, nf + j)),
                  pl.BlockSpec((3, tf), lambda i, j: (0, j)),
                  pl.BlockSpec((3, tf), lambda i, j: (0, nf + j)),
                  pl.BlockSpec((1, tf), lambda i, j: (0, j)),
                  pl.BlockSpec((1, tf), lambda i, j: (0, nf + j)),
                  pl.BlockSpec((tf, d), lambda i, j: (j, 0))],
        out_specs=pl.BlockSpec((tm, d), lambda i, j: (i, 0)),
        out_shape=jax.ShapeDtypeStruct((t, d), F32),
        scratch_shapes=[pltpu.VMEM((tm + 2 * HALO, d), BF16), pltpu.VMEM((tm, d), F32),
                        pltpu.VMEM((tm + 2 * HALO, tf), F32)],
        compiler_params=_params("parallel", "arbitrary"),
        name="conv_mlp",
    )(x2d, x2d, x2d, g, w_up, w_up, conv_w, conv_w, conv_b, conv_b, w_down)


def _alibi_slopes(n):
    return 2.0 ** (-8.0 * np.arange(1, n + 1) / n)


def _pair_split_perm():
    cols = []
    for t in range(A_Q_HEADS // 2):
        for e in range(2):
            hd = t + (A_Q_HEADS // 2) * e
            cols.extend(range(hd * HEAD_DIM, (hd + 1) * HEAD_DIM))
    return np.asarray(cols, np.int32)


def _layer(x, pos, layer_idx, norm1_g, w_in, a_q_norm, a_k_norm, a_sink, b_q_norm, b_k_norm, b_lam_q1,
           b_lam_k1, b_lam_q2, b_lam_k2, b_subln_g, c_q_norm, c_k_norm, d_cq_norm, d_ckv_norm, d_w_uq,
           d_w_ukv, d_q_norm, d_k_norm, w_branch, w_o, norm2_g, ffn_w_up, ffn_conv_w, ffn_conv_b,
           ffn_w_down):
    bsz, seq, d = x.shape
    t = bsz * seq
    x2d = x.reshape(t, d)
    ng = len(C_PATTERNS)
    qscale = HEAD_DIM ** -0.5

    n_a_in = A_Q_HEADS * HEAD_DIM + 2 * LANES
    n_a = n_a_in + 2 * LANES
    n_b = 3 * 2 * B_HEADS * HEAD_DIM
    n_cg = 3 * C_HEADS * HEAD_DIM
    n_main = n_a + n_b + n_cg
    perm = _pair_split_perm()
    w_all = jnp.concatenate([w_in[:, :A_Q_HEADS * HEAD_DIM][:, perm],
                             w_in[:, A_Q_HEADS * HEAD_DIM:n_a_in],
                             jnp.zeros((d, n_a - n_a_in), w_in.dtype),
                             w_in[:, n_a_in:n_a_in + n_b + ng * n_cg]], axis=1).astype(BF16)
    ones = lambda n: jnp.ones((n,), F32)
    zeros = lambda n: jnp.zeros((n,), F32)
    rep = lambda gvec, n: jnp.tile(gvec.astype(F32), n)
    gains = [rep(a_q_norm, 8) * (qscale * LOG2E), rep(a_k_norm, 2), ones(128 + n_a - n_a_in),
             rep(b_q_norm, 8) * (qscale * LOG2E), rep(b_k_norm, 8), ones(512)]
    flags = [ones(512), ones(128), zeros(128 + n_a - n_a_in), ones(512), ones(512), zeros(512)]
    for _ in range(ng):
        gains += [rep(c_q_norm, 8) * (qscale * LOG2E), rep(c_k_norm, 8), ones(512)]
        flags += [ones(512), ones(512), zeros(512)]
    assert C_PATTERNS[0][1] == 1
    colgain, normflag = jnp.concatenate(gains)[None, :], jnp.concatenate(flags)[None, :]
    y_main, h2d, qt_b, vt_b = _in_proj(x2d, norm1_g[None, :].astype(F32), w_all[:, :n_main], colgain[:, :n_main],
                                       normflag[:, :n_main], bsz, n_a, n_a + 2 * 2 * B_HEADS * HEAD_DIM)
    class_cols = [(r, (gi - 1) * n_cg, n_cg) for gi, (_, r) in enumerate(C_PATTERNS) if gi > 0]
    y_classes = _class_proj(h2d, w_all[:, n_main:], colgain[:, n_main:], normflag[:, n_main:], bsz, class_cols)
    y3 = y_main.reshape(bsz, seq, n_main)

    sl_a = _alibi_slopes(A_Q_HEADS)
    half = A_Q_HEADS // 2
    slopes_a = jnp.asarray(np.stack([sl_a[:half], sl_a[half:]], axis=1) * LOG2E, F32)
    sink_a = jnp.stack([a_sink[:half], a_sink[half:]], axis=1).astype(F32) * LOG2E
    (oa,) = _band_attention(y3, pos, A_RADIUS, slopes_a, sink_a, 0, 4, 5, 1, False)

    lam_init = 0.8 - 0.6 * math.exp(-0.3 * layer_idx)
    slopes_b = jnp.asarray(_alibi_slopes(B_HEADS) * LOG2E, F32)
    sb = n_a // LANES
    score_bound_b = 1.05 * HEAD_DIM * qscale * LOG2E * jnp.max(jnp.abs(b_q_norm)) * jnp.max(jnp.abs(b_k_norm))
    ob = _flash_b(y3, qt_b, vt_b, pos, slopes_b, score_bound_b, b_lam_q1.astype(F32), b_lam_k1.astype(F32),
                  b_lam_q2.astype(F32), b_lam_k2.astype(F32), b_subln_g[None, :].astype(F32), lam_init, sb + 4)

    sl_c = _alibi_slopes(ng * C_HEADS).reshape(ng, C_HEADS // 2, 2)
    c_outs, c_lses, dilations = [], [], []
    for gi, (window, r) in enumerate(C_PATTERNS):
        radius = window // (2 * r)
        slopes_c = jnp.asarray(sl_c[gi] * LOG2E, F32)
        if r == 1:
            slab0 = (n_a + n_b) // LANES
            o_g, lse_g = _band_attention(y3, pos, radius, slopes_c, None, slab0 // 4, slab0 + 4, slab0 + 8, 4, True)
        else:
            src = y_classes[gi - 1].reshape(bsz * r, seq // r, n_cg)
            pos_r = pos.reshape(bsz, seq // r, r).transpose(0, 2, 1).reshape(bsz * r, seq // r)
            o_g, lse_g = _band_attention(src, pos_r, radius, slopes_c, None, 0, 4, 8, 4, True)
        c_outs.append(o_g)
        c_lses.append(lse_g)
        dilations.append(r)
    oc = _c_merge(c_outs, c_lses, dilations, bsz, seq).reshape(t, -1)

    n0 = n_a_in + n_b + ng * n_cg
    dq = D_NOPE + D_ROPE
    vt_rows_d = D_VT_ROWS if layer_idx == 0 else LANES
    lane_of = _latent_lane_of_dim()

    def to_slab(a, dims):
        return jnp.zeros(a.shape[:-1] + (LANES,), a.dtype).at[..., lane_of[dims]].set(a)

    all_dims, nope_dims, rope_dims = np.arange(dq), np.arange(D_NOPE), np.arange(D_NOPE, dq)
    kr_cols = to_slab(w_in[:, n0 + D_Q_RANK + D_KV_RANK:n0 + D_Q_RANK + D_KV_RANK + D_ROPE], rope_dims)
    w_d = jnp.concatenate([w_in[:, n0:n0 + D_Q_RANK + D_KV_RANK], kr_cols], axis=1).astype(BF16)
    wuq = to_slab(d_w_uq.reshape(D_Q_RANK, D_HEADS, dq), all_dims).reshape(D_Q_RANK, -1).astype(BF16)
    ukv = d_w_ukv.reshape(D_KV_RANK, D_HEADS, D_NOPE + D_V)
    wuk = to_slab(ukv[:, :, :D_NOPE], nope_dims).reshape(D_KV_RANK, -1).astype(BF16)
    wuv = jnp.pad(ukv[:, :, D_NOPE:], ((0, 0), (0, 0), (0, LANES - D_V))).reshape(D_KV_RANK, -1).astype(BF16)
    gq = to_slab(d_q_norm.astype(F32) * (dq ** -0.5 * LOG2E), all_dims)[None, :]
    gk = to_slab(d_k_norm.astype(F32), all_dims)[None, :]
    halfr = D_ROPE // 2
    inv = ROPE_THETA ** (-np.arange(halfr, dtype=np.float32) / halfr)
    inv_tab = np.zeros((1, LANES), np.float32)
    inv_tab[0, lane_of[rope_dims]] = np.concatenate([inv, inv])
    pos_col = pos.astype(F32).reshape(t, 1)
    qt_d, kd, vt_d = _d_proj(h2d, w_d, d_cq_norm[None, :].astype(F32), d_ckv_norm[None, :].astype(F32),
                             wuq, wuk, wuv, gq, gk, pos_col, jnp.asarray(inv_tab), bsz, vt_rows_d)
    od = _flash_d(qt_d, kd.reshape(bsz, seq, -1), vt_d)

    n_gate0 = n0 + D_Q_RANK + D_KV_RANK + D_ROPE
    w_gate = w_in[:, n_gate0:].astype(BF16)
    wb = jnp.concatenate([w_branch[0][perm][None], w_branch[1:]], axis=0).astype(BF16)
    x1 = _merge(h2d, x2d, [oa.reshape(t, -1), ob.reshape(t, -1), oc, od.reshape(t, -1)],
                w_gate, wb, w_o.astype(BF16))

    x2 = _ffn(x1, seq, norm2_g[None, :].astype(F32), ffn_w_up.astype(BF16), ffn_conv_w.astype(F32),
              ffn_conv_b[None, :].astype(F32), ffn_w_down.astype(BF16))
    return x2.reshape(bsz, seq, d)


def kernel(x, positions, norm1_g, w_in, a_q_norm, a_k_norm, a_sink, b_q_norm, b_k_norm, b_lam_q1,
           b_lam_k1, b_lam_q2, b_lam_k2, b_subln_g, c_q_norm, c_k_norm, d_cq_norm, d_ckv_norm, d_w_uq,
           d_w_ukv, d_q_norm, d_k_norm, w_branch, w_o, norm2_g, ffn_w_up, ffn_conv_w, ffn_conv_b,
           ffn_w_down):
    depth = w_in.shape[0]
    per_layer = (norm1_g, w_in, a_q_norm, a_k_norm, a_sink, b_q_norm, b_k_norm, b_lam_q1, b_lam_k1,
                 b_lam_q2, b_lam_k2, b_subln_g, c_q_norm, c_k_norm, d_cq_norm, d_ckv_norm, d_w_uq,
                 d_w_ukv, d_q_norm, d_k_norm, w_branch, w_o, norm2_g, ffn_w_up, ffn_conv_w, ffn_conv_b,
                 ffn_w_down)
    for layer in range(depth):
        x = _layer(x, positions, layer, *[p[layer] for p in per_layer])
    return x
```

```python
import functools
import math

import jax
import jax.numpy as jnp
import numpy as np
from jax import lax
from jax.experimental import pallas as pl
from jax.experimental.pallas import tpu as pltpu

F32 = jnp.float32
BF16 = jnp.bfloat16

LANES = 128
HEAD_DIM = 64
A_Q_HEADS = 8
A_RADIUS = 128
B_HEADS = 4
C_PATTERNS = ((128, 1), (512, 4), (2048, 16))
C_HEADS = 8
D_HEADS = 8
D_Q_RANK = 384
D_KV_RANK = 256
D_NOPE = 64
D_ROPE = 32
D_V = 64
ROPE_THETA = 10000.0
N_BRANCH = 4
BRANCH_WIDTH = 512
EPS = 1e-6
NEG = -1e30
LOG2E = math.log2(math.e)
VMEM_LIMIT_BYTES = 56 * 1024 * 1024

NT_DIMS = (((1,), (1,)), ((), ()))
TN_DIMS = (((0,), (0,)), ((), ()))


def _tile(n, pref):
    return pref if n % pref == 0 else n


def _params(*sem):
    return pltpu.CompilerParams(dimension_semantics=sem, vmem_limit_bytes=VMEM_LIMIT_BYTES)


SEG_CHUNK = 256
B_VT_ROWS = LANES + 16


def _seg_ones():
    return jnp.asarray(np.kron(np.eye(SEG_CHUNK // HEAD_DIM), np.ones((HEAD_DIM, HEAD_DIM))), BF16)


def _head_normed(y, nf_ref, cg_ref, ones_ref):
    y2 = (y * y).astype(BF16)
    seg = jnp.concatenate(
        [jnp.dot(y2[:, c * SEG_CHUNK:(c + 1) * SEG_CHUNK], ones_ref[...], preferred_element_type=F32)
         for c in range(y.shape[1] // SEG_CHUNK)], axis=1)
    rs = lax.rsqrt(seg * (1.0 / HEAD_DIM) + EPS)
    return y * (jnp.where(nf_ref[...] > 0.0, rs, 1.0) * cg_ref[...])


def _in_proj_kernel(x_ref, g_ref, w_ref, cg_ref, nf_ref, ones_ref, y_ref, h_ref, qt_ref, vt_ref, hn_scr,
                    *, qt_tile, vt_tile):
    j = pl.program_id(1)

    @pl.when(j == 0)
    def _():
        x = x_ref[...]
        ms = jnp.mean(x * x, axis=-1, keepdims=True)
        hn = (x * lax.rsqrt(ms + EPS) * g_ref[...]).astype(BF16)
        hn_scr[...] = hn
        h_ref[...] = hn

    y = _head_normed(jnp.dot(hn_scr[...], w_ref[...], preferred_element_type=F32), nf_ref, cg_ref, ones_ref)
    y_ref[...] = y.astype(y_ref.dtype)

    @pl.when(j == qt_tile)
    def _():
        for s in range(B_HEADS):
            qt_ref[0, s] = y[:, s * LANES:(s + 1) * LANES].T.astype(qt_ref.dtype)

    @pl.when(j == vt_tile)
    def _():
        tm = y.shape[0]
        ones_row = (lax.broadcasted_iota(jnp.int32, (B_VT_ROWS - LANES, tm), 0) == 0).astype(vt_ref.dtype)
        for s in range(B_HEADS):
            vt_ref[0, s, :LANES, :] = y[:, s * LANES:(s + 1) * LANES].T.astype(vt_ref.dtype)
            vt_ref[0, s, LANES:, :] = ones_row


def _in_proj(x2d, g, w, colgain, normflag, bsz, q_col, v_col):
    t, d = x2d.shape
    n = w.shape[1]
    seq = t // bsz
    tm = _tile(seq, 1024)
    tn = _tile(n, 1024)
    assert q_col % tn == 0 and v_col % tn == 0
    nper = seq // tm
    return pl.pallas_call(
        functools.partial(_in_proj_kernel, qt_tile=q_col // tn, vt_tile=v_col // tn),
        grid=(t // tm, n // tn),
        in_specs=[
            pl.BlockSpec((tm, d), lambda i, j: (i, 0)),
            pl.BlockSpec((1, d), lambda i, j: (0, 0)),
            pl.BlockSpec((d, tn), lambda i, j: (0, j)),
            pl.BlockSpec((1, tn), lambda i, j: (0, j)),
            pl.BlockSpec((1, tn), lambda i, j: (0, j)),
            pl.BlockSpec((SEG_CHUNK, SEG_CHUNK), lambda i, j: (0, 0)),
        ],
        out_specs=[
            pl.BlockSpec((tm, tn), lambda i, j: (i, j)),
            pl.BlockSpec((tm, d), lambda i, j: (i, 0)),
            pl.BlockSpec((1, B_HEADS, LANES, tm), lambda i, j: (i // nper, 0, 0, i % nper)),
            pl.BlockSpec((1, B_HEADS, B_VT_ROWS, tm), lambda i, j: (i // nper, 0, 0, i % nper)),
        ],
        out_shape=[jax.ShapeDtypeStruct((t, n), BF16), jax.ShapeDtypeStruct((t, d), BF16),
                   jax.ShapeDtypeStruct((bsz, B_HEADS, LANES, seq), BF16),
                   jax.ShapeDtypeStruct((bsz, B_HEADS, B_VT_ROWS, seq), BF16)],
        scratch_shapes=[pltpu.VMEM((tm, d), BF16)],
        compiler_params=_params("parallel", "arbitrary"),
        name="in_proj",
    )(x2d, g, w, colgain, normflag, _seg_ones())


def _class_proj_kernel(h_ref, w_ref, cg_ref, nf_ref, ones_ref, *rest, class_groups):
    class_refs, y_scr = rest[:len(class_groups)], rest[len(class_groups)]
    j = pl.program_id(1)
    y = _head_normed(jnp.dot(h_ref[...], w_ref[...], preferred_element_type=F32), nf_ref, cg_ref, ones_ref)
    tm, tn = y.shape
    for s in range(tn // LANES):
        y_scr[s] = y[:, s * LANES:(s + 1) * LANES]
    for c_ref, (r, tile0, n_tiles) in zip(class_refs, class_groups):
        @pl.when((j >= tile0) & (j < tile0 + n_tiles))
        def _(c_ref=c_ref, r=r):
            for c in range(r):
                for s in range(tn // LANES):
                    c_ref[0, c, :, s * LANES:(s + 1) * LANES] = (
                        y_scr[s, pl.ds(c, tm // r, stride=r), :].astype(c_ref.dtype))


def _class_proj(h2d, w, colgain, normflag, bsz, class_cols):
    t, d = h2d.shape
    n = w.shape[1]
    seq = t // bsz
    tm = _tile(seq, 1024)
    tn = math.gcd(*[width for _, _, width in class_cols])
    assert n % tn == 0 and all(st % tn == 0 and wd % tn == 0 and tm % r == 0 for r, st, wd in class_cols)
    nper = seq // tm
    class_groups = tuple((r, start // tn, width // tn) for r, start, width in class_cols)

    def class_spec(r, tile0, n_tiles):
        return pl.BlockSpec((1, r, tm // r, tn),
                            lambda i, j: (i // nper, 0, i % nper, jnp.clip(j - tile0, 0, n_tiles - 1)))

    return pl.pallas_call(
        functools.partial(_class_proj_kernel, class_groups=class_groups),
        grid=(t // tm, n // tn),
        in_specs=[
            pl.BlockSpec((tm, d), lambda i, j: (i, 0)),
            pl.BlockSpec((d, tn), lambda i, j: (0, j)),
            pl.BlockSpec((1, tn), lambda i, j: (0, j)),
            pl.BlockSpec((1, tn), lambda i, j: (0, j)),
            pl.BlockSpec((SEG_CHUNK, SEG_CHUNK), lambda i, j: (0, 0)),
        ],
        out_specs=[class_spec(*grp) for grp in class_groups],
        out_shape=[jax.ShapeDtypeStruct((bsz, r, seq // r, width), BF16) for r, _, width in class_cols],
        scratch_shapes=[pltpu.VMEM((tn // LANES, tm, LANES), F32)],
        compiler_params=_params("parallel", "arbitrary"),
        name="class_proj",
    )(h2d, w, colgain, normflag, _seg_ones())


D_ROT_LO = 48
D_ROT_HI = D_ROT_LO + LANES // 2


def _latent_lane_of_dim():
    half = D_ROPE // 2
    nope = list(range(D_ROT_LO)) + list(range(LANES // 2, LANES // 2 + D_NOPE - D_ROT_LO))
    return np.asarray(nope + list(range(D_ROT_LO, D_ROT_LO + half)) + list(range(D_ROT_HI, D_ROT_HI + half)))


def _slab_sums(x2_bf16, ones_ref):
    return jnp.concatenate(
        [jnp.dot(x2_bf16[:, c * SEG_CHUNK:(c + 1) * SEG_CHUNK], ones_ref[...], preferred_element_type=F32)
         for c in range(x2_bf16.shape[1] // SEG_CHUNK)], axis=1)


def _d_proj_kernel(h_ref, wd_ref, gcq_ref, gckv_ref, wuq_ref, wuk_ref, wuv_ref, gq_ref, gk_ref,
                   pos_ref, inv_ref, ones_ref, qt_ref, k_ref, vt_ref):
    y = jnp.dot(h_ref[...], wd_ref[...], preferred_element_type=F32)
    cq = y[:, :D_Q_RANK]
    ckv = y[:, D_Q_RANK:D_Q_RANK + D_KV_RANK]
    kr = y[:, D_Q_RANK + D_KV_RANK:]
    cqn = (cq * lax.rsqrt(jnp.mean(cq * cq, axis=-1, keepdims=True) + EPS) * gcq_ref[...]).astype(BF16)
    ckvn = (ckv * lax.rsqrt(jnp.mean(ckv * ckv, axis=-1, keepdims=True) + EPS) * gckv_ref[...]).astype(BF16)
    q = jnp.dot(cqn, wuq_ref[...], preferred_element_type=F32)
    k = jnp.dot(ckvn, wuk_ref[...], preferred_element_type=F32) + jnp.concatenate([kr] * D_HEADS, axis=1)
    v = jnp.dot(ckvn, wuv_ref[...], preferred_element_type=F32)
    inv_w = 1.0 / (D_NOPE + D_ROPE)
    q = q * lax.rsqrt(_slab_sums((q * q).astype(BF16), ones_ref) * inv_w + EPS)
    k = k * lax.rsqrt(_slab_sums((k * k).astype(BF16), ones_ref) * inv_w + EPS)

    lane = lax.broadcasted_iota(jnp.int32, (1, LANES), 1)
    ang = pos_ref[...] * inv_ref[...]
    half = D_ROPE // 2
    lo = (lane >= D_ROT_LO) & (lane < D_ROT_LO + half)
    hi = (lane >= D_ROT_HI) & (lane < D_ROT_HI + half)
    c_tab = jnp.where(lo | hi, jnp.cos(ang), 1.0)
    sn = jnp.sin(ang)
    s_tab = jnp.where(lo, -sn, jnp.where(hi, sn, 0.0))
    rows = vt_ref.shape[2]
    for hd in range(D_HEADS):
        sl = slice(hd * LANES, (hd + 1) * LANES)
        qs = q[:, sl] * gq_ref[...]
        qt_ref[0, hd] = (qs * c_tab + pltpu.roll(qs, LANES // 2, 1) * s_tab).T.astype(qt_ref.dtype)
        ks = k[:, sl] * gk_ref[...]
        k_ref[:, sl] = (ks * c_tab + pltpu.roll(ks, LANES // 2, 1) * s_tab).astype(k_ref.dtype)
        vt_ref[0, hd] = jnp.where(lane == D_V, 1.0, v[:, sl]).T[:rows].astype(vt_ref.dtype)


def _d_proj(h2d, wd, gcq, gckv, wuq, wuk, wuv, gq, gk, pos_col, inv_tab, bsz):
    vt_rows = D_VT_ROWS
    t, d = h2d.shape
    seq = t // bsz
    tm = _tile(seq, 512)
    nper = seq // tm
    nd = D_HEADS * LANES
    slab_ones = jnp.asarray(np.kron(np.eye(SEG_CHUNK // LANES), np.ones((LANES, LANES))), BF16)
    full = lambda a: pl.BlockSpec(a.shape, lambda i: (0,) * a.ndim)
    return pl.pallas_call(
        _d_proj_kernel,
        grid=(t // tm,),
        in_specs=[pl.BlockSpec((tm, d), lambda i: (i, 0)), full(wd), full(gcq), full(gckv), full(wuq),
                  full(wuk), full(wuv), full(gq), full(gk), pl.BlockSpec((tm, 1), lambda i: (i, 0)),
                  full(inv_tab), full(slab_ones)],
        out_specs=[pl.BlockSpec((1, D_HEADS, LANES, tm), lambda i: (i // nper, 0, 0, i % nper)),
                   pl.BlockSpec((tm, nd), lambda i: (i, 0)),
                   pl.BlockSpec((1, D_HEADS, vt_rows, tm), lambda i: (i // nper, 0, 0, i % nper))],
        out_shape=[jax.ShapeDtypeStruct((bsz, D_HEADS, LANES, seq), BF16),
                   jax.ShapeDtypeStruct((t, nd), BF16),
                   jax.ShapeDtypeStruct((bsz, D_HEADS, vt_rows, seq), BF16)],
        compiler_params=_params("parallel"),
        name="d_proj",
    )(h2d, wd, gcq, gckv, wuq, wuk, wuv, gq, gk, pos_col, inv_tab, slab_ones)


N_QSLAB = 4
BAND_GQ = 128


def _band_kernel(slope_ref, sink_ref, q_ref, kp_ref, kc_ref, kn_ref, vp_ref, vc_ref, vn_ref,
                 posq_ref, posw_ref, *out_refs, radius, tq, gq, length, nkv, has_sink, want_lse):
    i = pl.program_id(1)
    kw = jnp.concatenate([kp_ref[0], kc_ref[0], kn_ref[0]], axis=0)
    vw = jnp.concatenate([vp_ref[0], vc_ref[0], vn_ref[0]], axis=0)
    wg = gq + 2 * radius
    c_io = lax.broadcasted_iota(jnp.int32, (wg, gq), 0)
    r_io = lax.broadcasted_iota(jnp.int32, (wg, gq), 1)
    in_band = jnp.abs(c_io - radius - r_io) <= radius
    lane = lax.broadcasted_iota(jnp.int32, (gq, LANES), 1)
    row = lax.broadcasted_iota(jnp.int32, (LANES, gq), 0)
    for g in range(tq // gq):
        rows = slice(g * gq, g * gq + wg)
        cols = slice(g * gq, (g + 1) * gq)
        jabs = i * tq + g * gq - radius + c_io
        mask = in_band & (jabs >= 0) & (jabs < length)
        dist = jnp.abs(posw_ref[0, 0, rows, :] - posq_ref[0, :, cols])
        for t in range(N_QSLAB):
            q = q_ref[0, cols, t * LANES:(t + 1) * LANES]
            kt = t if nkv == N_QSLAB else 0
            ks = kw[rows, kt * LANES:(kt + 1) * LANES]
            vs = vw[rows, kt * LANES:(kt + 1) * LANES]
            outs, lses = [], []
            for e in range(2):
                in_half = (lane >= HEAD_DIM) if e else (lane < HEAD_DIM)
                qe = jnp.where(in_half, q, jnp.zeros_like(q))
                s = lax.dot_general(ks, qe, NT_DIMS, preferred_element_type=F32)
                s = jnp.where(mask, s - slope_ref[t, e] * dist, NEG)
                m = jnp.max(s, axis=0, keepdims=True)
                if has_sink:
                    sk = sink_ref[t, e]
                    m = jnp.maximum(m, sk)
                p = jnp.exp2(s - m)
                den = jnp.sum(p, axis=0, keepdims=True)
                if has_sink:
                    den = den + jnp.exp2(sk - m)
                ot = lax.dot_general(vs, p.astype(BF16), TN_DIMS, preferred_element_type=F32)
                outs.append(ot / den)
                lses.append((m + jnp.log2(den)) * (1.0 / LOG2E))
            sl = slice(t * LANES, (t + 1) * LANES)
            out_refs[0][0, cols, sl] = jnp.where(row < HEAD_DIM, outs[0], outs[1]).T.astype(out_refs[0].dtype)
            if want_lse:
                out_refs[1][0, cols, sl] = jnp.where(row < HEAD_DIM, lses[0], lses[1]).T


def _band_attention(src, pos, radius, slopes, sink, q_blk, k_slab0, v_slab0, nkv, want_lse):
    bn, length, _ = src.shape
    tq = _tile(length, 512)
    nb = length // tq
    rb = tq // radius
    nrb = length // radius
    w = tq + 2 * radius
    kvw = nkv * LANES
    idx = jnp.clip(jnp.arange(nb)[:, None] * tq - radius + jnp.arange(w)[None, :], 0, length - 1)
    posf = pos.astype(F32)
    posw = posf[:, idx][:, :, :, None]
    posq = posf[:, None, :]
    has_sink = sink is not None
    if sink is None:
        sink = jnp.zeros((N_QSLAB, 2), F32)

    def halo(slab0, which):
        blk = slab0 // nkv
        if which == 0:
            return pl.BlockSpec((1, radius, kvw), lambda b, i: (b, jnp.maximum(i * rb - 1, 0), blk))
        if which == 1:
            return pl.BlockSpec((1, tq, kvw), lambda b, i: (b, i, blk))
        return pl.BlockSpec((1, radius, kvw), lambda b, i: (b, jnp.minimum((i + 1) * rb, nrb - 1), blk))

    smem = pl.BlockSpec(memory_space=pltpu.SMEM)
    qw = N_QSLAB * LANES
    out_specs = [pl.BlockSpec((1, tq, qw), lambda b, i: (b, i, 0))]
    out_shape = [jax.ShapeDtypeStruct((bn, length, qw), BF16)]
    if want_lse:
        out_specs.append(pl.BlockSpec((1, tq, qw), lambda b, i: (b, i, 0)))
        out_shape.append(jax.ShapeDtypeStruct((bn, length, qw), F32))
    return pl.pallas_call(
        functools.partial(_band_kernel, radius=radius, tq=tq, gq=min(BAND_GQ, tq), length=length, nkv=nkv,
                          has_sink=has_sink, want_lse=want_lse),
        grid=(bn, nb),
        in_specs=[smem, smem,
                  pl.BlockSpec((1, tq, qw), lambda b, i: (b, i, q_blk)),
                  halo(k_slab0, 0), halo(k_slab0, 1), halo(k_slab0, 2),
                  halo(v_slab0, 0), halo(v_slab0, 1), halo(v_slab0, 2),
                  pl.BlockSpec((1, 1, tq), lambda b, i: (b, 0, i)),
                  pl.BlockSpec((1, 1, w, 1), lambda b, i: (b, i, 0, 0))],
        out_specs=out_specs,
        out_shape=out_shape,
        compiler_params=_params("parallel", "arbitrary"),
        name="band_attn",
    )(slopes, sink, src, src, src, src, src, src, src, posq, posw)


def _c_merge_kernel(*refs, dilations, tm):
    ng = len(dilations)
    o_refs, l_refs, out_ref = refs[:ng], refs[ng:2 * ng], refs[2 * ng]
    o_scr, l_scr = refs[2 * ng + 1:]

    def natural(ref, scr, r):
        if r == 1:
            return ref[0].astype(F32)
        nslab = scr.shape[0]
        for c in range(r):
            blk = ref[0, c].astype(F32)
            for s in range(nslab):
                scr[s, pl.ds(c, tm // r, stride=r), :] = blk[:, s * LANES:(s + 1) * LANES]
        return jnp.concatenate([scr[s] for s in range(nslab)], axis=1)

    ls = [natural(l_refs[g], l_scr, r) for g, r in enumerate(dilations)]
    m = functools.reduce(jnp.maximum, ls)
    ws = [jnp.exp(l - m) for l in ls]
    num = sum(w * natural(o_refs[g], o_scr, r) for g, (w, r) in enumerate(zip(ws, dilations)))
    out_ref[0] = (num / sum(ws)).astype(out_ref.dtype)


def _c_merge(outs, lses, dilations, bsz, seq):
    n = outs[0].shape[-1]
    tm = _tile(seq, 512)
    views, specs = [], []
    for arrs in (outs, lses):
        for a, r in zip(arrs, dilations):
            if r == 1:
                views.append(a.reshape(bsz, seq, n))
                specs.append(pl.BlockSpec((1, tm, n), lambda b, i: (b, i, 0)))
            else:
                views.append(a.reshape(bsz, r, seq // r, n))
                specs.append(pl.BlockSpec((1, r, tm // r, n), lambda b, i: (b, 0, i, 0)))
    return pl.pallas_call(
        functools.partial(_c_merge_kernel, dilations=tuple(dilations), tm=tm),
        grid=(bsz, seq // tm),
        in_specs=specs,
        out_specs=pl.BlockSpec((1, tm, n), lambda b, i: (b, i, 0)),
        out_shape=jax.ShapeDtypeStruct((bsz, seq, n), BF16),
        scratch_shapes=[pltpu.VMEM((n // LANES, tm, LANES), F32), pltpu.VMEM((n // LANES, tm, LANES), F32)],
        compiler_params=_params("parallel", "parallel"),
        name="c_merge",
    )(*views)


def _flash_loop(n_trips, score_fn, value_fn, m_refs, acc_refs, chunk_of=lambda trip: trip):
    def body(trip, carry):
        j = chunk_of(trip)
        scores = score_fn(j)
        for u, (s, vt_c) in enumerate(zip(scores, value_fn(j))):
            m_old = m_refs[u][...]
            m_new = jnp.maximum(m_old, jnp.max(s, axis=0, keepdims=True))
            alpha = jnp.exp2(m_old - m_new)
            p = jnp.exp2(s - m_new).astype(BF16)
            acc_refs[u][...] = alpha * acc_refs[u][...] + jnp.dot(vt_c, p, preferred_element_type=F32)
            m_refs[u][...] = m_new
        return carry

    lax.fori_loop(0, n_trips, body, 0)


def _flash_init(m_refs, acc_refs):
    for m_ref, acc_ref in zip(m_refs, acc_refs):
        m_ref[...] = jnp.full(m_ref.shape, NEG, F32)
        acc_ref[...] = jnp.zeros(acc_ref.shape, F32)


B_HEAD_GROUPS = ((2, 3), (1,), (0,))


def _flash_b_kernel(slope_ref, cnt_ref, lst_ref, qt_ref, k_ref, vt_ref, posq_ref, posk_ref, lq1_ref, lk1_ref,
                    lq2_ref, lk2_ref, subg_ref, o_ref, *scratch, tq, tk, seq, lam_init):
    n_units = 2 * B_HEADS
    n_chunks = seq // tk
    m_refs, acc_refs = scratch[:n_units], scratch[n_units:]
    _flash_init(m_refs, acc_refs)
    row = lax.broadcasted_iota(jnp.int32, (LANES, tq), 0)
    slab_of = [2 * (u % 2) + (u // 2) // 2 for u in range(n_units)]
    qs = []
    for u in range(n_units):
        qm = qt_ref[0, slab_of[u]]
        qs.append(jnp.where((row >= HEAD_DIM) if (u // 2) % 2 else (row < HEAD_DIM), qm, jnp.zeros_like(qm)))
    posq = posq_ref[0]
    tile = pl.program_id(0) * pl.num_programs(1) + pl.program_id(1)

    for g, heads in enumerate(B_HEAD_GROUPS):
        units = [2 * hd + mp for hd in heads for mp in range(2)]
        row0 = (tile * len(B_HEAD_GROUPS) + g)

        def score_fn(j, heads=heads):
            off = pl.multiple_of(j * tk, tk)
            posk = posk_ref[0, pl.ds(off, tk), :]
            dist = jnp.abs(jnp.concatenate([posk] * (tq // LANES), axis=1) - posq)
            k_c = k_ref[0, pl.ds(off, tk), :]
            scores = []
            for hd in heads:
                bias = slope_ref[hd] * dist
                for u in (2 * hd, 2 * hd + 1):
                    sl = slab_of[u]
                    scores.append(jnp.dot(k_c[:, sl * LANES:(sl + 1) * LANES], qs[u],
                                          preferred_element_type=F32) - bias)
            return scores

        def value_fn(j, heads=heads):
            off = pl.multiple_of(j * tk, tk)
            return [vt_ref[0, hd, :, pl.ds(off, tk)] for hd in heads for _ in range(2)]

        _flash_loop(cnt_ref[row0], score_fn, value_fn, [m_refs[u] for u in units], [acc_refs[u] for u in units],
                    chunk_of=lambda trip, row0=row0: lst_ref[row0 * n_chunks + trip])

    def normalised(u):
        acc = acc_refs[u][...]
        return acc[:LANES] / acc[LANES:LANES + 1]

    for hd in range(B_HEADS):
        u0, u1 = 2 * hd, 2 * hd + 1
        lam1 = jnp.sum(lq1_ref[hd:hd + 1, :] * lk1_ref[hd:hd + 1, :], axis=-1, keepdims=True)
        lam2 = jnp.sum(lq2_ref[hd:hd + 1, :] * lk2_ref[hd:hd + 1, :], axis=-1, keepdims=True)
        lam = jnp.exp(lam1) - jnp.exp(lam2) + lam_init
        a = (normalised(u0) - lam * normalised(u1)).T
        ms = jnp.mean(a * a, axis=-1, keepdims=True)
        out = a * lax.rsqrt(ms + EPS) * subg_ref[...] * (1.0 - lam_init)
        o_ref[0, :, hd * LANES:(hd + 1) * LANES] = out.astype(o_ref.dtype)


UNDERFLOW_LOG2 = 150.0


def _alibi_chunk_lists(posf, slopes, score_bound, tq, tk):
    bsz, seq = posf.shape
    qmin, qmax = posf.reshape(bsz, -1, tq).min(-1), posf.reshape(bsz, -1, tq).max(-1)
    kmin, kmax = posf.reshape(bsz, -1, tk).min(-1), posf.reshape(bsz, -1, tk).max(-1)
    dmin = jnp.maximum(jnp.maximum(qmin[:, :, None] - kmax[:, None, :], kmin[:, None, :] - qmax[:, :, None]), 0.0)
    needed = slopes[None, None, :, None] * dmin[:, :, None, :] < 2.0 * score_bound + UNDERFLOW_LOG2
    group = jnp.stack([functools.reduce(jnp.logical_or, [needed[:, :, hd] for hd in heads])
                       for heads in B_HEAD_GROUPS], axis=2)
    order = jnp.argsort(jnp.logical_not(group), axis=-1, stable=True).astype(jnp.int32)
    return group.sum(-1).astype(jnp.int32).reshape(-1), order.reshape(-1)


def _flash_b(y_main, qt, vt, pos, slopes, score_bound, lq1, lk1, lq2, lk2, subg, lam_init, k_slab0):
    bsz, seq, _ = y_main.shape
    tq = _tile(seq, 512)
    tk = _tile(seq, 512)
    nq = B_HEADS
    posf = pos.astype(F32)
    posq = posf[:, None, :]
    posk = jnp.broadcast_to(posf[:, :, None], (bsz, seq, LANES))
    counts, chunks = _alibi_chunk_lists(posf, slopes, score_bound, tq, tk)
    once = pl.Buffered(1)
    small = lambda a: pl.BlockSpec(a.shape, lambda b, i: (0,) * a.ndim)
    smem = pl.BlockSpec(memory_space=pltpu.SMEM)
    n_units = 2 * nq
    return pl.pallas_call(
        functools.partial(_flash_b_kernel, tq=tq, tk=tk, seq=seq, lam_init=lam_init),
        grid=(bsz, seq // tq),
        in_specs=[smem, smem, smem,
                  pl.BlockSpec((1, nq, LANES, tq), lambda b, i: (b, 0, 0, i)),
                  pl.BlockSpec((1, seq, nq * LANES), lambda b, i: (b, 0, k_slab0 // nq), pipeline_mode=once),
                  pl.BlockSpec((1, B_HEADS, B_VT_ROWS, seq), lambda b, i: (b, 0, 0, 0), pipeline_mode=once),
                  pl.BlockSpec((1, 1, tq), lambda b, i: (b, 0, i)),
                  pl.BlockSpec((1, seq, LANES), lambda b, i: (b, 0, 0), pipeline_mode=once),
                  small(lq1), small(lk1), small(lq2), small(lk2), small(subg)],
        out_specs=pl.BlockSpec((1, tq, B_HEADS * LANES), lambda b, i: (b, i, 0)),
        out_shape=jax.ShapeDtypeStruct((bsz, seq, B_HEADS * LANES), BF16),
        scratch_shapes=[pltpu.VMEM((1, tq), F32)] * n_units + [pltpu.VMEM((B_VT_ROWS, tq), F32)] * n_units,
        compiler_params=_params("parallel", "arbitrary"),
        name="flash_diff",
    )(slopes, counts, chunks, qt, y_main, vt, posq, posk, lq1, lk1, lq2, lk2, subg)


D_VT_ROWS = D_V + 16
D_GROUP = 8


def _flash_d_kernel(*refs, tq, tk, seq):
    n = D_GROUP
    q_refs, k_refs, vt_refs = refs[:n], refs[n:2 * n], refs[2 * n:3 * n]
    o_ref = refs[3 * n]
    scratch = refs[3 * n + 1:]
    m_refs, acc_refs = scratch[:n], scratch[n:]
    _flash_init(m_refs, acc_refs)
    qs = [q_ref[0, 0] for q_ref in q_refs]

    def score_fn(j):
        off = pl.multiple_of(j * tk, tk)
        return [jnp.dot(k_refs[e][0, pl.ds(off, tk), :], qs[e], preferred_element_type=F32) for e in range(n)]

    def value_fn(j):
        off = pl.multiple_of(j * tk, tk)
        return [vt_refs[e][0, 0, :, pl.ds(off, tk)] for e in range(n)]

    _flash_loop(seq // tk, score_fn, value_fn, m_refs, acc_refs)
    outs = []
    for e in range(n):
        acc = acc_refs[e][...]
        outs.append(acc[:D_V] / acc[D_V:D_V + 1])
    o_ref[0] = jnp.concatenate(outs, axis=0).T.astype(o_ref.dtype)


def _flash_d(qt, kd, vt):
    bsz, seq, _ = kd.shape
    tq = _tile(seq, 512)
    tk = _tile(seq, 512)
    n = D_GROUP
    once = pl.Buffered(1)
    q_spec = lambda e: pl.BlockSpec((1, 1, LANES, tq), lambda b, hg, i: (b, n * hg + e, 0, i))
    k_spec = lambda e: pl.BlockSpec((1, seq, LANES), lambda b, hg, i: (b, 0, n * hg + e), pipeline_mode=once)
    vt_rows = vt.shape[2]
    vt_spec = lambda e: pl.BlockSpec((1, 1, vt_rows, seq), lambda b, hg, i: (b, n * hg + e, 0, 0),
                                     pipeline_mode=once)
    return pl.pallas_call(
        functools.partial(_flash_d_kernel, tq=tq, tk=tk, seq=seq),
        grid=(bsz, D_HEADS // n, seq // tq),
        in_specs=[q_spec(e) for e in range(n)] + [k_spec(e) for e in range(n)]
                 + [vt_spec(e) for e in range(n)],
        out_specs=pl.BlockSpec((1, tq, n * D_V), lambda b, hg, i: (b, i, hg)),
        out_shape=jax.ShapeDtypeStruct((bsz, seq, D_HEADS * D_V), BF16),
        scratch_shapes=[pltpu.VMEM((1, tq), F32)] * n + [pltpu.VMEM((vt_rows, tq), F32)] * n,
        compiler_params=_params("parallel", "parallel", "arbitrary"),
        name="flash_latent",
    )(*([qt] * n + [kd] * n + [vt] * n))


def _merge_kernel(h_ref, x_ref, oa_ref, ob_ref, oc_ref, od_ref, wg_ref, wb_ref, wo_ref, out_ref, acc_scr):
    j = pl.program_id(1)

    @pl.when(j == 0)
    def _():
        acc_scr[...] = jnp.zeros(acc_scr.shape, F32)

    gate = jax.nn.sigmoid(jnp.dot(h_ref[...], wg_ref[...], preferred_element_type=F32))
    for br, o_ref in enumerate((oa_ref, ob_ref, oc_ref, od_ref)):
        @pl.when(j == br)
        def _(o_ref=o_ref):
            acc_scr[...] += gate * jnp.dot(o_ref[...], wb_ref[0], preferred_element_type=F32)

    @pl.when(j == N_BRANCH - 1)
    def _():
        out_ref[...] = x_ref[...] + jnp.dot(acc_scr[...].astype(BF16), wo_ref[...],
                                            preferred_element_type=F32)


def _merge(h2d, x2d, branch_outs, wg, wb, wo):
    t, d = x2d.shape
    tm = _tile(t, 512)
    bw = wb.shape[1]
    row = lambda width: pl.BlockSpec((tm, width), lambda i, j: (i, 0))
    return pl.pallas_call(
        _merge_kernel,
        grid=(t // tm, N_BRANCH),
        in_specs=[row(d), row(d), row(bw), row(bw), row(bw), row(bw),
                  pl.BlockSpec((d, d), lambda i, j: (0, j)),
                  pl.BlockSpec((1, bw, d), lambda i, j: (j, 0, 0)),
                  pl.BlockSpec((d, d), lambda i, j: (0, 0))],
        out_specs=row(d),
        out_shape=jax.ShapeDtypeStruct((t, d), F32),
        scratch_shapes=[pltpu.VMEM((tm, d), F32)],
        compiler_params=_params("parallel", "arbitrary"),
        name="gated_merge",
    )(h2d, x2d, *branch_outs, wg, wb, wo)


HALO = 8


def _ffn_kernel(x_ref, xp_ref, xn_ref, g_ref, wg_ref, wu_ref, cwg_ref, cwu_ref, cbg_ref, cbu_ref,
                wd_ref, out_ref, hn_scr, acc_scr, u_scr, *, tm, tiles_per_seq):
    i = pl.program_id(0)
    j = pl.program_id(1)
    rows = tm + 2 * HALO

    @pl.when(j == 0)
    def _():
        xe = jnp.concatenate([xp_ref[...], x_ref[...], xn_ref[...]], axis=0)
        ms = jnp.mean(xe * xe, axis=-1, keepdims=True)
        hn = xe * lax.rsqrt(ms + EPS) * g_ref[...]
        r = lax.broadcasted_iota(jnp.int32, (rows, 1), 0)
        first = (i % tiles_per_seq) == 0
        last = (i % tiles_per_seq) == tiles_per_seq - 1
        dead = ((r == HALO - 1) & first) | ((r == HALO + tm) & last)
        hn_scr[...] = jnp.where(dead, 0.0, hn).astype(BF16)
        acc_scr[...] = jnp.zeros(acc_scr.shape, F32)

    def conv(w_ref, cw_ref, cb_ref):
        u_scr[...] = jnp.dot(hn_scr[...], w_ref[...], preferred_element_type=F32)
        return (cw_ref[0:1, :] * u_scr[HALO - 1:HALO - 1 + tm, :] + cw_ref[1:2, :] * u_scr[HALO:HALO + tm, :]
                + cw_ref[2:3, :] * u_scr[HALO + 1:HALO + 1 + tm, :] + cb_ref[...])

    yg = conv(wg_ref, cwg_ref, cbg_ref)
    yu = conv(wu_ref, cwu_ref, cbu_ref)
    act = (jax.nn.silu(yg) * yu).astype(BF16)
    acc_scr[...] += jnp.dot(act, wd_ref[...], preferred_element_type=F32)

    @pl.when(j == pl.num_programs(1) - 1)
    def _():
        out_ref[...] = x_ref[...] + acc_scr[...]


def _ffn(x2d, seq, g, w_up, conv_w, conv_b, w_down):
    t, d = x2d.shape
    dff = w_down.shape[0]
    tm = _tile(seq, 512)
    tf = _tile(dff, 1408)
    nf = dff // tf
    nhb = t // HALO
    rb = tm // HALO
    return pl.pallas_call(
        functools.partial(_ffn_kernel, tm=tm, tiles_per_seq=seq // tm),
        grid=(t // tm, nf),
        in_specs=[pl.BlockSpec((tm, d), lambda i, j: (i, 0)),
                  pl.BlockSpec((HALO, d), lambda i, j: (jnp.maximum(i * rb - 1, 0), 0)),
                  pl.BlockSpec((HALO, d), lambda i, j: (jnp.minimum((i + 1) * rb, nhb - 1), 0)),
                  pl.BlockSpec((1, d), lambda i, j: (0, 0)),
                  pl.BlockSpec((d, tf), lambda i, j: (0, j)),
                  pl.BlockSpec((d, tf), lambda i, j: (0, nf + j)),
                  pl.BlockSpec((3, tf), lambda i, j: (0, j)),
                  pl.BlockSpec((3, tf), lambda i, j: (0, nf + j)),
                  pl.BlockSpec((1, tf), lambda i, j: (0, j)),
                  pl.BlockSpec((1, tf), lambda i, j: (0, nf + j)),
                  pl.BlockSpec((tf, d), lambda i, j: (j, 0))],
        out_specs=pl.BlockSpec((tm, d), lambda i, j: (i, 0)),
        out_shape=jax.ShapeDtypeStruct((t, d), F32),
        scratch_shapes=[pltpu.VMEM((tm + 2 * HALO, d), BF16), pltpu.VMEM((tm, d), F32),
                        pltpu.VMEM((tm + 2 * HALO, tf), F32)],
        compiler_params=_params("parallel", "arbitrary"),
        name="conv_mlp",
    )(x2d, x2d, x2d, g, w_up, w_up, conv_w, conv_w, conv_b, conv_b, w_down)


def _alibi_slopes(n):
    return 2.0 ** (-8.0 * np.arange(1, n + 1) / n)


def _pair_split_perm():
    cols = []
    for t in range(A_Q_HEADS // 2):
        for e in range(2):
            hd = t + (A_Q_HEADS // 2) * e
            cols.extend(range(hd * HEAD_DIM, (hd + 1) * HEAD_DIM))
    return np.asarray(cols, np.int32)


def _layer(x, pos, layer_idx, norm1_g, w_in, a_q_norm, a_k_norm, a_sink, b_q_norm, b_k_norm, b_lam_q1,
           b_lam_k1, b_lam_q2, b_lam_k2, b_subln_g, c_q_norm, c_k_norm, d_cq_norm, d_ckv_norm, d_w_uq,
           d_w_ukv, d_q_norm, d_k_norm, w_branch, w_o, norm2_g, ffn_w_up, ffn_conv_w, ffn_conv_b,
           ffn_w_down):
    bsz, seq, d = x.shape
    t = bsz * seq
    x2d = x.reshape(t, d)
    ng = len(C_PATTERNS)
    qscale = HEAD_DIM ** -0.5

    n_a_in = A_Q_HEADS * HEAD_DIM + 2 * LANES
    n_a = n_a_in + 2 * LANES
    n_b = 3 * 2 * B_HEADS * HEAD_DIM
    n_cg = 3 * C_HEADS * HEAD_DIM
    n_main = n_a + n_b + n_cg
    perm = _pair_split_perm()
    w_all = jnp.concatenate([w_in[:, :A_Q_HEADS * HEAD_DIM][:, perm],
                             w_in[:, A_Q_HEADS * HEAD_DIM:n_a_in],
                             jnp.zeros((d, n_a - n_a_in), w_in.dtype),
                             w_in[:, n_a_in:n_a_in + n_b + ng * n_cg]], axis=1).astype(BF16)
    ones = lambda n: jnp.ones((n,), F32)
    zeros = lambda n: jnp.zeros((n,), F32)
    rep = lambda gvec, n: jnp.tile(gvec.astype(F32), n)
    gains = [rep(a_q_norm, 8) * (qscale * LOG2E), rep(a_k_norm, 2), ones(128 + n_a - n_a_in),
             rep(b_q_norm, 8) * (qscale * LOG2E), rep(b_k_norm, 8), ones(512)]
    flags = [ones(512), ones(128), zeros(128 + n_a - n_a_in), ones(512), ones(512), zeros(512)]
    for _ in range(ng):
        gains += [rep(c_q_norm, 8) * (qscale * LOG2E), rep(c_k_norm, 8), ones(512)]
        flags += [ones(512), ones(512), zeros(512)]
    assert C_PATTERNS[0][1] == 1
    colgain, normflag = jnp.concatenate(gains)[None, :], jnp.concatenate(flags)[None, :]
    y_main, h2d, qt_b, vt_b = _in_proj(x2d, norm1_g[None, :].astype(F32), w_all[:, :n_main], colgain[:, :n_main],
                                       normflag[:, :n_main], bsz, n_a, n_a + 2 * 2 * B_HEADS * HEAD_DIM)
    class_cols = [(r, (gi - 1) * n_cg, n_cg) for gi, (_, r) in enumerate(C_PATTERNS) if gi > 0]
    y_classes = _class_proj(h2d, w_all[:, n_main:], colgain[:, n_main:], normflag[:, n_main:], bsz, class_cols)
    y3 = y_main.reshape(bsz, seq, n_main)

    sl_a = _alibi_slopes(A_Q_HEADS)
    half = A_Q_HEADS // 2
    slopes_a = jnp.asarray(np.stack([sl_a[:half], sl_a[half:]], axis=1) * LOG2E, F32)
    sink_a = jnp.stack([a_sink[:half], a_sink[half:]], axis=1).astype(F32) * LOG2E
    (oa,) = _band_attention(y3, pos, A_RADIUS, slopes_a, sink_a, 0, 4, 5, 1, False)

    lam_init = 0.8 - 0.6 * math.exp(-0.3 * layer_idx)
    slopes_b = jnp.asarray(_alibi_slopes(B_HEADS) * LOG2E, F32)
    sb = n_a // LANES
    score_bound_b = 1.05 * HEAD_DIM * qscale * LOG2E * jnp.max(jnp.abs(b_q_norm)) * jnp.max(jnp.abs(b_k_norm))
    ob = _flash_b(y3, qt_b, vt_b, pos, slopes_b, score_bound_b, b_lam_q1.astype(F32), b_lam_k1.astype(F32),
                  b_lam_q2.astype(F32), b_lam_k2.astype(F32), b_subln_g[None, :].astype(F32), lam_init, sb + 4)

    sl_c = _alibi_slopes(ng * C_HEADS).reshape(ng, C_HEADS // 2, 2)
    c_outs, c_lses, dilations = [], [], []
    for gi, (window, r) in enumerate(C_PATTERNS):
        radius = window // (2 * r)
        slopes_c = jnp.asarray(sl_c[gi] * LOG2E, F32)
        if r == 1:
            slab0 = (n_a + n_b) // LANES
            o_g, lse_g = _band_attention(y3, pos, radius, slopes_c, None, slab0 // 4, slab0 + 4, slab0 + 8, 4, True)
        else:
            src = y_classes[gi - 1].reshape(bsz * r, seq // r, n_cg)
            pos_r = pos.reshape(bsz, seq // r, r).transpose(0, 2, 1).reshape(bsz * r, seq // r)
            o_g, lse_g = _band_attention(src, pos_r, radius, slopes_c, None, 0, 4, 8, 4, True)
        c_outs.append(o_g)
        c_lses.append(lse_g)
        dilations.append(r)
    oc = _c_merge(c_outs, c_lses, dilations, bsz, seq).reshape(t, -1)

    n0 = n_a_in + n_b + ng * n_cg
    dq = D_NOPE + D_ROPE
    lane_of = _latent_lane_of_dim()

    def to_slab(a, dims):
        return jnp.zeros(a.shape[:-1] + (LANES,), a.dtype).at[..., lane_of[dims]].set(a)

    all_dims, nope_dims, rope_dims = np.arange(dq), np.arange(D_NOPE), np.arange(D_NOPE, dq)
    kr_cols = to_slab(w_in[:, n0 + D_Q_RANK + D_KV_RANK:n0 + D_Q_RANK + D_KV_RANK + D_ROPE], rope_dims)
    w_d = jnp.concatenate([w_in[:, n0:n0 + D_Q_RANK + D_KV_RANK], kr_cols], axis=1).astype(BF16)
    wuq = to_slab(d_w_uq.reshape(D_Q_RANK, D_HEADS, dq), all_dims).reshape(D_Q_RANK, -1).astype(BF16)
    ukv = d_w_ukv.reshape(D_KV_RANK, D_HEADS, D_NOPE + D_V)
    wuk = to_slab(ukv[:, :, :D_NOPE], nope_dims).reshape(D_KV_RANK, -1).astype(BF16)
    wuv = jnp.pad(ukv[:, :, D_NOPE:], ((0, 0), (0, 0), (0, LANES - D_V))).reshape(D_KV_RANK, -1).astype(BF16)
    gq = to_slab(d_q_norm.astype(F32) * (dq ** -0.5 * LOG2E), all_dims)[None, :]
    gk = to_slab(d_k_norm.astype(F32), all_dims)[None, :]
    halfr = D_ROPE // 2
    inv = ROPE_THETA ** (-np.arange(halfr, dtype=np.float32) / halfr)
    inv_tab = np.zeros((1, LANES), np.float32)
    inv_tab[0, lane_of[rope_dims]] = np.concatenate([inv, inv])
    pos_col = pos.astype(F32).reshape(t, 1)
    qt_d, kd, vt_d = _d_proj(h2d, w_d, d_cq_norm[None, :].astype(F32), d_ckv_norm[None, :].astype(F32),
                             wuq, wuk, wuv, gq, gk, pos_col, jnp.asarray(inv_tab), bsz)
    od = _flash_d(qt_d, kd.reshape(bsz, seq, -1), vt_d)

    n_gate0 = n0 + D_Q_RANK + D_KV_RANK + D_ROPE
    w_gate = w_in[:, n_gate0:].astype(BF16)
    wb = jnp.concatenate([w_branch[0][perm][None], w_branch[1:]], axis=0).astype(BF16)
    x1 = _merge(h2d, x2d, [oa.reshape(t, -1), ob.reshape(t, -1), oc, od.reshape(t, -1)],
                w_gate, wb, w_o.astype(BF16))

    x2 = _ffn(x1, seq, norm2_g[None, :].astype(F32), ffn_w_up.astype(BF16), ffn_conv_w.astype(F32),
              ffn_conv_b[None, :].astype(F32), ffn_w_down.astype(BF16))
    return x2.reshape(bsz, seq, d)


def kernel(x, positions, norm1_g, w_in, a_q_norm, a_k_norm, a_sink, b_q_norm, b_k_norm, b_lam_q1,
           b_lam_k1, b_lam_q2, b_lam_k2, b_subln_g, c_q_norm, c_k_norm, d_cq_norm, d_ckv_norm, d_w_uq,
           d_w_ukv, d_q_norm, d_k_norm, w_branch, w_o, norm2_g, ffn_w_up, ffn_conv_w, ffn_conv_b,
           ffn_w_down):
    depth = w_in.shape[0]
    per_layer = (norm1_g, w_in, a_q_norm, a_k_norm, a_sink, b_q_norm, b_k_norm, b_lam_q1, b_lam_k1,
                 b_lam_q2, b_lam_k2, b_subln_g, c_q_norm, c_k_norm, d_cq_norm, d_ckv_norm, d_w_uq,
                 d_w_ukv, d_q_norm, d_k_norm, w_branch, w_o, norm2_g, ffn_w_up, ffn_conv_w, ffn_conv_b,
                 ffn_w_down)
    for layer in range(depth):
        x = _layer(x, positions, layer, *[p[layer] for p in per_layer])
    return x
```

```python
import functools
import math

import jax
import jax.numpy as jnp
import numpy as np
from jax import lax
from jax.experimental import pallas as pl
from jax.experimental.pallas import tpu as pltpu

F32 = jnp.float32
BF16 = jnp.bfloat16

LANES = 128
HEAD_DIM = 64
A_Q_HEADS = 8
A_RADIUS = 128
B_HEADS = 4
C_PATTERNS = ((128, 1), (512, 4), (2048, 16))
C_HEADS = 8
D_HEADS = 8
D_Q_RANK = 384
D_KV_RANK = 256
D_NOPE = 64
D_ROPE = 32
D_V = 64
ROPE_THETA = 10000.0
N_BRANCH = 4
BRANCH_WIDTH = 512
EPS = 1e-6
NEG = -1e30
LOG2E = math.log2(math.e)
VMEM_LIMIT_BYTES = 56 * 1024 * 1024

NT_DIMS = (((1,), (1,)), ((), ()))
TN_DIMS = (((0,), (0,)), ((), ()))


def _tile(n, pref):
    return pref if n % pref == 0 else n


def _params(*sem):
    return pltpu.CompilerParams(dimension_semantics=sem, vmem_limit_bytes=VMEM_LIMIT_BYTES)


SEG_CHUNK = 256
B_VT_ROWS = LANES + 16


def _seg_ones():
    return jnp.asarray(np.kron(np.eye(SEG_CHUNK // HEAD_DIM), np.ones((HEAD_DIM, HEAD_DIM))), BF16)


def _head_normed(y, nf_ref, cg_ref, ones_ref):
    y2 = (y * y).astype(BF16)
    seg = jnp.concatenate(
        [jnp.dot(y2[:, c * SEG_CHUNK:(c + 1) * SEG_CHUNK], ones_ref[...], preferred_element_type=F32)
         for c in range(y.shape[1] // SEG_CHUNK)], axis=1)
    rs = lax.rsqrt(seg * (1.0 / HEAD_DIM) + EPS)
    return y * (jnp.where(nf_ref[...] > 0.0, rs, 1.0) * cg_ref[...])


def _in_proj_kernel(x_ref, g_ref, w_ref, cg_ref, nf_ref, ones_ref, y_ref, h_ref, qt_ref, vt_ref, hn_scr,
                    *, qt_tile, vt_tile):
    j = pl.program_id(1)

    @pl.when(j == 0)
    def _():
        x = x_ref[...]
        ms = jnp.mean(x * x, axis=-1, keepdims=True)
        hn = (x * lax.rsqrt(ms + EPS) * g_ref[...]).astype(BF16)
        hn_scr[...] = hn
        h_ref[...] = hn

    y = _head_normed(jnp.dot(hn_scr[...], w_ref[...], preferred_element_type=F32), nf_ref, cg_ref, ones_ref)
    y_ref[...] = y.astype(y_ref.dtype)

    @pl.when(j == qt_tile)
    def _():
        for s in range(B_HEADS):
            qt_ref[0, s] = y[:, s * LANES:(s + 1) * LANES].T.astype(qt_ref.dtype)

    @pl.when(j == vt_tile)
    def _():
        tm = y.shape[0]
        ones_row = (lax.broadcasted_iota(jnp.int32, (B_VT_ROWS - LANES, tm), 0) == 0).astype(vt_ref.dtype)
        for s in range(B_HEADS):
            vt_ref[0, s, :LANES, :] = y[:, s * LANES:(s + 1) * LANES].T.astype(vt_ref.dtype)
            vt_ref[0, s, LANES:, :] = ones_row


def _in_proj(x2d, g, w, colgain, normflag, bsz, q_col, v_col):
    t, d = x2d.shape
    n = w.shape[1]
    seq = t // bsz
    tm = _tile(seq, 1024)
    tn = _tile(n, 1024)
    assert q_col % tn == 0 and v_col % tn == 0
    nper = seq // tm
    return pl.pallas_call(
        functools.partial(_in_proj_kernel, qt_tile=q_col // tn, vt_tile=v_col // tn),
        grid=(t // tm, n // tn),
        in_specs=[
            pl.BlockSpec((tm, d), lambda i, j: (i, 0)),
            pl.BlockSpec((1, d), lambda i, j: (0, 0)),
            pl.BlockSpec((d, tn), lambda i, j: (0, j)),
            pl.BlockSpec((1, tn), lambda i, j: (0, j)),
            pl.BlockSpec((1, tn), lambda i, j: (0, j)),
            pl.BlockSpec((SEG_CHUNK, SEG_CHUNK), lambda i, j: (0, 0)),
        ],
        out_specs=[
            pl.BlockSpec((tm, tn), lambda i, j: (i, j)),
            pl.BlockSpec((tm, d), lambda i, j: (i, 0)),
            pl.BlockSpec((1, B_HEADS, LANES, tm), lambda i, j: (i // nper, 0, 0, i % nper)),
            pl.BlockSpec((1, B_HEADS, B_VT_ROWS, tm), lambda i, j: (i // nper, 0, 0, i % nper)),
        ],
        out_shape=[jax.ShapeDtypeStruct((t, n), BF16), jax.ShapeDtypeStruct((t, d), BF16),
                   jax.ShapeDtypeStruct((bsz, B_HEADS, LANES, seq), BF16),
                   jax.ShapeDtypeStruct((bsz, B_HEADS, B_VT_ROWS, seq), BF16)],
        scratch_shapes=[pltpu.VMEM((tm, d), BF16)],
        compiler_params=_params("parallel", "arbitrary"),
        name="in_proj",
    )(x2d, g, w, colgain, normflag, _seg_ones())


def _class_proj_kernel(h_ref, w_ref, cg_ref, nf_ref, ones_ref, *rest, class_groups):
    class_refs, y_scr = rest[:len(class_groups)], rest[len(class_groups)]
    j = pl.program_id(1)
    y = _head_normed(jnp.dot(h_ref[...], w_ref[...], preferred_element_type=F32), nf_ref, cg_ref, ones_ref)
    tm, tn = y.shape
    for s in range(tn // LANES):
        y_scr[s] = y[:, s * LANES:(s + 1) * LANES]
    for c_ref, (r, tile0, n_tiles) in zip(class_refs, class_groups):
        @pl.when((j >= tile0) & (j < tile0 + n_tiles))
        def _(c_ref=c_ref, r=r):
            for c in range(r):
                for s in range(tn // LANES):
                    c_ref[0, c, :, s * LANES:(s + 1) * LANES] = (
                        y_scr[s, pl.ds(c, tm // r, stride=r), :].astype(c_ref.dtype))


def _class_proj(h2d, w, colgain, normflag, bsz, class_cols):
    t, d = h2d.shape
    n = w.shape[1]
    seq = t // bsz
    tm = _tile(seq, 1024)
    tn = math.gcd(*[width for _, _, width in class_cols])
    assert n % tn == 0 and all(st % tn == 0 and wd % tn == 0 and tm % r == 0 for r, st, wd in class_cols)
    nper = seq // tm
    class_groups = tuple((r, start // tn, width // tn) for r, start, width in class_cols)

    def class_spec(r, tile0, n_tiles):
        return pl.BlockSpec((1, r, tm // r, tn),
                            lambda i, j: (i // nper, 0, i % nper, jnp.clip(j - tile0, 0, n_tiles - 1)))

    return pl.pallas_call(
        functools.partial(_class_proj_kernel, class_groups=class_groups),
        grid=(t // tm, n // tn),
        in_specs=[
            pl.BlockSpec((tm, d), lambda i, j: (i, 0)),
            pl.BlockSpec((d, tn), lambda i, j: (0, j)),
            pl.BlockSpec((1, tn), lambda i, j: (0, j)),
            pl.BlockSpec((1, tn), lambda i, j: (0, j)),
            pl.BlockSpec((SEG_CHUNK, SEG_CHUNK), lambda i, j: (0, 0)),
        ],
        out_specs=[class_spec(*grp) for grp in class_groups],
        out_shape=[jax.ShapeDtypeStruct((bsz, r, seq // r, width), BF16) for r, _, width in class_cols],
        scratch_shapes=[pltpu.VMEM((tn // LANES, tm, LANES), F32)],
        compiler_params=_params("parallel", "arbitrary"),
        name="class_proj",
    )(h2d, w, colgain, normflag, _seg_ones())


D_ROT_LO = 48
D_ROT_HI = D_ROT_LO + LANES // 2


def _latent_lane_of_dim():
    half = D_ROPE // 2
    nope = list(range(D_ROT_LO)) + list(range(LANES // 2, LANES // 2 + D_NOPE - D_ROT_LO))
    return np.asarray(nope + list(range(D_ROT_LO, D_ROT_LO + half)) + list(range(D_ROT_HI, D_ROT_HI + half)))


def _slab_sums(x2_bf16, ones_ref):
    return jnp.concatenate(
        [jnp.dot(x2_bf16[:, c * SEG_CHUNK:(c + 1) * SEG_CHUNK], ones_ref[...], preferred_element_type=F32)
         for c in range(x2_bf16.shape[1] // SEG_CHUNK)], axis=1)


def _d_proj_kernel(h_ref, wd_ref, gcq_ref, gckv_ref, wuq_ref, wuk_ref, wuv_ref, gq_ref, gk_ref,
                   pos_ref, inv_ref, ones_ref, qt_ref, k_ref, vt_ref):
    y = jnp.dot(h_ref[...], wd_ref[...], preferred_element_type=F32)
    cq = y[:, :D_Q_RANK]
    ckv = y[:, D_Q_RANK:D_Q_RANK + D_KV_RANK]
    kr = y[:, D_Q_RANK + D_KV_RANK:]
    cqn = (cq * lax.rsqrt(jnp.mean(cq * cq, axis=-1, keepdims=True) + EPS) * gcq_ref[...]).astype(BF16)
    ckvn = (ckv * lax.rsqrt(jnp.mean(ckv * ckv, axis=-1, keepdims=True) + EPS) * gckv_ref[...]).astype(BF16)
    q = jnp.dot(cqn, wuq_ref[...], preferred_element_type=F32)
    k = jnp.dot(ckvn, wuk_ref[...], preferred_element_type=F32) + jnp.concatenate([kr] * D_HEADS, axis=1)
    v = jnp.dot(ckvn, wuv_ref[...], preferred_element_type=F32)
    inv_w = 1.0 / (D_NOPE + D_ROPE)
    q = q * lax.rsqrt(_slab_sums((q * q).astype(BF16), ones_ref) * inv_w + EPS)
    k = k * lax.rsqrt(_slab_sums((k * k).astype(BF16), ones_ref) * inv_w + EPS)

    lane = lax.broadcasted_iota(jnp.int32, (1, LANES), 1)
    ang = pos_ref[...] * inv_ref[...]
    half = D_ROPE // 2
    lo = (lane >= D_ROT_LO) & (lane < D_ROT_LO + half)
    hi = (lane >= D_ROT_HI) & (lane < D_ROT_HI + half)
    c_tab = jnp.where(lo | hi, jnp.cos(ang), 1.0)
    sn = jnp.sin(ang)
    s_tab = jnp.where(lo, -sn, jnp.where(hi, sn, 0.0))
    rows = vt_ref.shape[2]
    for hd in range(D_HEADS):
        sl = slice(hd * LANES, (hd + 1) * LANES)
        qs = q[:, sl] * gq_ref[...]
        qt_ref[0, hd] = (qs * c_tab + pltpu.roll(qs, LANES // 2, 1) * s_tab).T.astype(qt_ref.dtype)
        ks = k[:, sl] * gk_ref[...]
        k_ref[:, sl] = (ks * c_tab + pltpu.roll(ks, LANES // 2, 1) * s_tab).astype(k_ref.dtype)
        vt_ref[0, hd] = jnp.where(lane == D_V, 1.0, v[:, sl]).T[:rows].astype(vt_ref.dtype)


def _d_proj(h2d, wd, gcq, gckv, wuq, wuk, wuv, gq, gk, pos_col, inv_tab, bsz):
    vt_rows = D_VT_ROWS
    t, d = h2d.shape
    seq = t // bsz
    tm = _tile(seq, 512)
    nper = seq // tm
    nd = D_HEADS * LANES
    slab_ones = jnp.asarray(np.kron(np.eye(SEG_CHUNK // LANES), np.ones((LANES, LANES))), BF16)
    full = lambda a: pl.BlockSpec(a.shape, lambda i: (0,) * a.ndim)
    return pl.pallas_call(
        _d_proj_kernel,
        grid=(t // tm,),
        in_specs=[pl.BlockSpec((tm, d), lambda i: (i, 0)), full(wd), full(gcq), full(gckv), full(wuq),
                  full(wuk), full(wuv), full(gq), full(gk), pl.BlockSpec((tm, 1), lambda i: (i, 0)),
                  full(inv_tab), full(slab_ones)],
        out_specs=[pl.BlockSpec((1, D_HEADS, LANES, tm), lambda i: (i // nper, 0, 0, i % nper)),
                   pl.BlockSpec((tm, nd), lambda i: (i, 0)),
                   pl.BlockSpec((1, D_HEADS, vt_rows, tm), lambda i: (i // nper, 0, 0, i % nper))],
        out_shape=[jax.ShapeDtypeStruct((bsz, D_HEADS, LANES, seq), BF16),
                   jax.ShapeDtypeStruct((t, nd), BF16),
                   jax.ShapeDtypeStruct((bsz, D_HEADS, vt_rows, seq), BF16)],
        compiler_params=_params("parallel"),
        name="d_proj",
    )(h2d, wd, gcq, gckv, wuq, wuk, wuv, gq, gk, pos_col, inv_tab, slab_ones)


N_QSLAB = 4
BAND_GQ = 128


def _band_kernel(slope_ref, sink_ref, q_ref, kp_ref, kc_ref, kn_ref, vp_ref, vc_ref, vn_ref,
                 posq_ref, posw_ref, *out_refs, radius, tq, gq, length, nkv, has_sink, want_lse):
    i = pl.program_id(1)
    kw = jnp.concatenate([kp_ref[0], kc_ref[0], kn_ref[0]], axis=0)
    vw = jnp.concatenate([vp_ref[0], vc_ref[0], vn_ref[0]], axis=0)
    wg = gq + 2 * radius
    c_io = lax.broadcasted_iota(jnp.int32, (wg, gq), 0)
    r_io = lax.broadcasted_iota(jnp.int32, (wg, gq), 1)
    in_band = jnp.abs(c_io - radius - r_io) <= radius
    lane = lax.broadcasted_iota(jnp.int32, (gq, LANES), 1)
    row = lax.broadcasted_iota(jnp.int32, (LANES, gq), 0)
    for g in range(tq // gq):
        rows = slice(g * gq, g * gq + wg)
        cols = slice(g * gq, (g + 1) * gq)
        jabs = i * tq + g * gq - radius + c_io
        mask = in_band & (jabs >= 0) & (jabs < length)
        dist = jnp.abs(posw_ref[0, 0, rows, :] - posq_ref[0, :, cols])
        for t in range(N_QSLAB):
            q = q_ref[0, cols, t * LANES:(t + 1) * LANES]
            kt = t if nkv == N_QSLAB else 0
            ks = kw[rows, kt * LANES:(kt + 1) * LANES]
            vs = vw[rows, kt * LANES:(kt + 1) * LANES]
            outs, lses = [], []
            for e in range(2):
                in_half = (lane >= HEAD_DIM) if e else (lane < HEAD_DIM)
                qe = jnp.where(in_half, q, jnp.zeros_like(q))
                s = lax.dot_general(ks, qe, NT_DIMS, preferred_element_type=F32)
                s = jnp.where(mask, s - slope_ref[t, e] * dist, NEG)
                m = jnp.max(s, axis=0, keepdims=True)
                if has_sink:
                    sk = sink_ref[t, e]
                    m = jnp.maximum(m, sk)
                p = jnp.exp2(s - m)
                den = jnp.sum(p, axis=0, keepdims=True)
                if has_sink:
                    den = den + jnp.exp2(sk - m)
                ot = lax.dot_general(vs, p.astype(BF16), TN_DIMS, preferred_element_type=F32)
                outs.append(ot / den)
                lses.append((m + jnp.log2(den)) * (1.0 / LOG2E))
            sl = slice(t * LANES, (t + 1) * LANES)
            out_refs[0][0, cols, sl] = jnp.where(row < HEAD_DIM, outs[0], outs[1]).T.astype(out_refs[0].dtype)
            if want_lse:
                out_refs[1][0, cols, sl] = jnp.where(row < HEAD_DIM, lses[0], lses[1]).T


def _band_attention(src, pos, radius, slopes, sink, q_blk, k_slab0, v_slab0, nkv, want_lse):
    bn, length, _ = src.shape
    tq = _tile(length, 512)
    nb = length // tq
    rb = tq // radius
    nrb = length // radius
    w = tq + 2 * radius
    kvw = nkv * LANES
    idx = jnp.clip(jnp.arange(nb)[:, None] * tq - radius + jnp.arange(w)[None, :], 0, length - 1)
    posf = pos.astype(F32)
    posw = posf[:, idx][:, :, :, None]
    posq = posf[:, None, :]
    has_sink = sink is not None
    if sink is None:
        sink = jnp.zeros((N_QSLAB, 2), F32)

    def halo(slab0, which):
        blk = slab0 // nkv
        if which == 0:
            return pl.BlockSpec((1, radius, kvw), lambda b, i: (b, jnp.maximum(i * rb - 1, 0), blk))
        if which == 1:
            return pl.BlockSpec((1, tq, kvw), lambda b, i: (b, i, blk))
        return pl.BlockSpec((1, radius, kvw), lambda b, i: (b, jnp.minimum((i + 1) * rb, nrb - 1), blk))

    smem = pl.BlockSpec(memory_space=pltpu.SMEM)
    qw = N_QSLAB * LANES
    out_specs = [pl.BlockSpec((1, tq, qw), lambda b, i: (b, i, 0))]
    out_shape = [jax.ShapeDtypeStruct((bn, length, qw), BF16)]
    if want_lse:
        out_specs.append(pl.BlockSpec((1, tq, qw), lambda b, i: (b, i, 0)))
        out_shape.append(jax.ShapeDtypeStruct((bn, length, qw), F32))
    return pl.pallas_call(
        functools.partial(_band_kernel, radius=radius, tq=tq, gq=min(BAND_GQ, tq), length=length, nkv=nkv,
                          has_sink=has_sink, want_lse=want_lse),
        grid=(bn, nb),
        in_specs=[smem, smem,
                  pl.BlockSpec((1, tq, qw), lambda b, i: (b, i, q_blk)),
                  halo(k_slab0, 0), halo(k_slab0, 1), halo(k_slab0, 2),
                  halo(v_slab0, 0), halo(v_slab0, 1), halo(v_slab0, 2),
                  pl.BlockSpec((1, 1, tq), lambda b, i: (b, 0, i)),
                  pl.BlockSpec((1, 1, w, 1), lambda b, i: (b, i, 0, 0))],
        out_specs=out_specs,
        out_shape=out_shape,
        compiler_params=_params("parallel", "arbitrary"),
        name="band_attn",
    )(slopes, sink, src, src, src, src, src, src, src, posq, posw)


def _c_merge_kernel(*refs, dilations, tm):
    ng = len(dilations)
    o_refs, l_refs, out_ref = refs[:ng], refs[ng:2 * ng], refs[2 * ng]
    o_scr, l_scr = refs[2 * ng + 1:]

    def natural(ref, scr, r):
        if r == 1:
            return ref[0].astype(F32)
        nslab = scr.shape[0]
        for c in range(r):
            blk = ref[0, c].astype(F32)
            for s in range(nslab):
                scr[s, pl.ds(c, tm // r, stride=r), :] = blk[:, s * LANES:(s + 1) * LANES]
        return jnp.concatenate([scr[s] for s in range(nslab)], axis=1)

    ls = [natural(l_refs[g], l_scr, r) for g, r in enumerate(dilations)]
    m = functools.reduce(jnp.maximum, ls)
    ws = [jnp.exp(l - m) for l in ls]
    num = sum(w * natural(o_refs[g], o_scr, r) for g, (w, r) in enumerate(zip(ws, dilations)))
    out_ref[0] = (num / sum(ws)).astype(out_ref.dtype)


def _c_merge(outs, lses, dilations, bsz, seq):
    n = outs[0].shape[-1]
    tm = _tile(seq, 512)
    views, specs = [], []
    for arrs in (outs, lses):
        for a, r in zip(arrs, dilations):
            if r == 1:
                views.append(a.reshape(bsz, seq, n))
                specs.append(pl.BlockSpec((1, tm, n), lambda b, i: (b, i, 0)))
            else:
                views.append(a.reshape(bsz, r, seq // r, n))
                specs.append(pl.BlockSpec((1, r, tm // r, n), lambda b, i: (b, 0, i, 0)))
    return pl.pallas_call(
        functools.partial(_c_merge_kernel, dilations=tuple(dilations), tm=tm),
        grid=(bsz, seq // tm),
        in_specs=specs,
        out_specs=pl.BlockSpec((1, tm, n), lambda b, i: (b, i, 0)),
        out_shape=jax.ShapeDtypeStruct((bsz, seq, n), BF16),
        scratch_shapes=[pltpu.VMEM((n // LANES, tm, LANES), F32), pltpu.VMEM((n // LANES, tm, LANES), F32)],
        compiler_params=_params("parallel", "parallel"),
        name="c_merge",
    )(*views)


def _flash_loop(n_trips, score_fn, value_fn, m_refs, acc_refs, chunk_of=lambda trip: trip, exp_dtype=F32):
    def body(trip, carry):
        j = chunk_of(trip)
        scores = score_fn(j)
        for u, (s, vt_c) in enumerate(zip(scores, value_fn(j))):
            m_old = m_refs[u][...]
            m_new = jnp.maximum(m_old, jnp.max(s, axis=0, keepdims=True))
            alpha = jnp.exp2(m_old - m_new)
            p = jnp.exp2((s - m_new).astype(exp_dtype)).astype(BF16)
            acc_refs[u][...] = alpha * acc_refs[u][...] + jnp.dot(vt_c, p, preferred_element_type=F32)
            m_refs[u][...] = m_new
        return carry

    lax.fori_loop(0, n_trips, body, 0)


def _flash_init(m_refs, acc_refs):
    for m_ref, acc_ref in zip(m_refs, acc_refs):
        m_ref[...] = jnp.full(m_ref.shape, NEG, F32)
        acc_ref[...] = jnp.zeros(acc_ref.shape, F32)


B_HEAD_GROUPS = ((2, 3), (1,), (0,))


def _flash_b_kernel(slope_ref, cnt_ref, lst_ref, qt_ref, k_ref, vt_ref, posq_ref, posk_ref, lq1_ref, lk1_ref,
                    lq2_ref, lk2_ref, subg_ref, o_ref, *scratch, tq, tk, seq, lam_init):
    n_units = 2 * B_HEADS
    n_chunks = seq // tk
    m_refs, acc_refs = scratch[:n_units], scratch[n_units:]
    _flash_init(m_refs, acc_refs)
    row = lax.broadcasted_iota(jnp.int32, (LANES, tq), 0)
    slab_of = [2 * (u % 2) + (u // 2) // 2 for u in range(n_units)]
    qs = []
    for u in range(n_units):
        qm = qt_ref[0, slab_of[u]]
        qs.append(jnp.where((row >= HEAD_DIM) if (u // 2) % 2 else (row < HEAD_DIM), qm, jnp.zeros_like(qm)))
    posq = posq_ref[0]
    tile = pl.program_id(0) * pl.num_programs(1) + pl.program_id(1)

    for g, heads in enumerate(B_HEAD_GROUPS):
        units = [2 * hd + mp for hd in heads for mp in range(2)]
        row0 = (tile * len(B_HEAD_GROUPS) + g)

        def score_fn(j, heads=heads):
            off = pl.multiple_of(j * tk, tk)
            posk = posk_ref[0, pl.ds(off, tk), :]
            dist = jnp.abs(jnp.concatenate([posk] * (tq // LANES), axis=1) - posq)
            k_c = k_ref[0, pl.ds(off, tk), :]
            scores = []
            for hd in heads:
                bias = slope_ref[hd] * dist
                for u in (2 * hd, 2 * hd + 1):
                    sl = slab_of[u]
                    scores.append(jnp.dot(k_c[:, sl * LANES:(sl + 1) * LANES], qs[u],
                                          preferred_element_type=F32) - bias)
            return scores

        def value_fn(j, heads=heads):
            off = pl.multiple_of(j * tk, tk)
            return [vt_ref[0, hd, :, pl.ds(off, tk)] for hd in heads for _ in range(2)]

        _flash_loop(cnt_ref[row0], score_fn, value_fn, [m_refs[u] for u in units], [acc_refs[u] for u in units],
                    chunk_of=lambda trip, row0=row0: lst_ref[row0 * n_chunks + trip])

    def normalised(u):
        acc = acc_refs[u][...]
        return acc[:LANES] / acc[LANES:LANES + 1]

    for hd in range(B_HEADS):
        u0, u1 = 2 * hd, 2 * hd + 1
        lam1 = jnp.sum(lq1_ref[hd:hd + 1, :] * lk1_ref[hd:hd + 1, :], axis=-1, keepdims=True)
        lam2 = jnp.sum(lq2_ref[hd:hd + 1, :] * lk2_ref[hd:hd + 1, :], axis=-1, keepdims=True)
        lam = jnp.exp(lam1) - jnp.exp(lam2) + lam_init
        a = (normalised(u0) - lam * normalised(u1)).T
        ms = jnp.mean(a * a, axis=-1, keepdims=True)
        out = a * lax.rsqrt(ms + EPS) * subg_ref[...] * (1.0 - lam_init)
        o_ref[0, :, hd * LANES:(hd + 1) * LANES] = out.astype(o_ref.dtype)


UNDERFLOW_LOG2 = 150.0


def _alibi_chunk_lists(posf, slopes, score_bound, tq, tk):
    bsz, seq = posf.shape
    qmin, qmax = posf.reshape(bsz, -1, tq).min(-1), posf.reshape(bsz, -1, tq).max(-1)
    kmin, kmax = posf.reshape(bsz, -1, tk).min(-1), posf.reshape(bsz, -1, tk).max(-1)
    dmin = jnp.maximum(jnp.maximum(qmin[:, :, None] - kmax[:, None, :], kmin[:, None, :] - qmax[:, :, None]), 0.0)
    needed = slopes[None, None, :, None] * dmin[:, :, None, :] < 2.0 * score_bound + UNDERFLOW_LOG2
    group = jnp.stack([functools.reduce(jnp.logical_or, [needed[:, :, hd] for hd in heads])
                       for heads in B_HEAD_GROUPS], axis=2)
    order = jnp.argsort(jnp.logical_not(group), axis=-1, stable=True).astype(jnp.int32)
    return group.sum(-1).astype(jnp.int32).reshape(-1), order.reshape(-1)


def _flash_b(y_main, qt, vt, pos, slopes, score_bound, lq1, lk1, lq2, lk2, subg, lam_init, k_slab0):
    bsz, seq, _ = y_main.shape
    tq = _tile(seq, 512)
    tk = _tile(seq, 512)
    nq = B_HEADS
    posf = pos.astype(F32)
    posq = posf[:, None, :]
    posk = jnp.broadcast_to(posf[:, :, None], (bsz, seq, LANES))
    counts, chunks = _alibi_chunk_lists(posf, slopes, score_bound, tq, tk)
    once = pl.Buffered(1)
    small = lambda a: pl.BlockSpec(a.shape, lambda b, i: (0,) * a.ndim)
    smem = pl.BlockSpec(memory_space=pltpu.SMEM)
    n_units = 2 * nq
    return pl.pallas_call(
        functools.partial(_flash_b_kernel, tq=tq, tk=tk, seq=seq, lam_init=lam_init),
        grid=(bsz, seq // tq),
        in_specs=[smem, smem, smem,
                  pl.BlockSpec((1, nq, LANES, tq), lambda b, i: (b, 0, 0, i)),
                  pl.BlockSpec((1, seq, nq * LANES), lambda b, i: (b, 0, k_slab0 // nq), pipeline_mode=once),
                  pl.BlockSpec((1, B_HEADS, B_VT_ROWS, seq), lambda b, i: (b, 0, 0, 0), pipeline_mode=once),
                  pl.BlockSpec((1, 1, tq), lambda b, i: (b, 0, i)),
                  pl.BlockSpec((1, seq, LANES), lambda b, i: (b, 0, 0), pipeline_mode=once),
                  small(lq1), small(lk1), small(lq2), small(lk2), small(subg)],
        out_specs=pl.BlockSpec((1, tq, B_HEADS * LANES), lambda b, i: (b, i, 0)),
        out_shape=jax.ShapeDtypeStruct((bsz, seq, B_HEADS * LANES), BF16),
        scratch_shapes=[pltpu.VMEM((1, tq), F32)] * n_units + [pltpu.VMEM((B_VT_ROWS, tq), F32)] * n_units,
        compiler_params=_params("parallel", "arbitrary"),
        name="flash_diff",
    )(slopes, counts, chunks, qt, y_main, vt, posq, posk, lq1, lk1, lq2, lk2, subg)


D_VT_ROWS = D_V + 16
D_GROUP = 8


def _flash_d_kernel(*refs, tq, tk, seq, exp_dtype):
    n = D_GROUP
    q_refs, k_refs, vt_refs = refs[:n], refs[n:2 * n], refs[2 * n:3 * n]
    o_ref = refs[3 * n]
    scratch = refs[3 * n + 1:]
    m_refs, acc_refs = scratch[:n], scratch[n:]
    _flash_init(m_refs, acc_refs)
    qs = [q_ref[0, 0] for q_ref in q_refs]

    def score_fn(j):
        off = pl.multiple_of(j * tk, tk)
        return [jnp.dot(k_refs[e][0, pl.ds(off, tk), :], qs[e], preferred_element_type=F32) for e in range(n)]

    def value_fn(j):
        off = pl.multiple_of(j * tk, tk)
        return [vt_refs[e][0, 0, :, pl.ds(off, tk)] for e in range(n)]

    _flash_loop(seq // tk, score_fn, value_fn, m_refs, acc_refs, exp_dtype=exp_dtype)
    outs = []
    for e in range(n):
        acc = acc_refs[e][...]
        outs.append(acc[:D_V] / acc[D_V:D_V + 1])
    o_ref[0] = jnp.concatenate(outs, axis=0).T.astype(o_ref.dtype)


def _flash_d(qt, kd, vt, exp_dtype):
    bsz, seq, _ = kd.shape
    tq = _tile(seq, 512)
    tk = _tile(seq, 512)
    n = D_GROUP
    once = pl.Buffered(1)
    q_spec = lambda e: pl.BlockSpec((1, 1, LANES, tq), lambda b, hg, i: (b, n * hg + e, 0, i))
    k_spec = lambda e: pl.BlockSpec((1, seq, LANES), lambda b, hg, i: (b, 0, n * hg + e), pipeline_mode=once)
    vt_rows = vt.shape[2]
    vt_spec = lambda e: pl.BlockSpec((1, 1, vt_rows, seq), lambda b, hg, i: (b, n * hg + e, 0, 0),
                                     pipeline_mode=once)
    return pl.pallas_call(
        functools.partial(_flash_d_kernel, tq=tq, tk=tk, seq=seq, exp_dtype=exp_dtype),
        grid=(bsz, D_HEADS // n, seq // tq),
        in_specs=[q_spec(e) for e in range(n)] + [k_spec(e) for e in range(n)]
                 + [vt_spec(e) for e in range(n)],
        out_specs=pl.BlockSpec((1, tq, n * D_V), lambda b, hg, i: (b, i, hg)),
        out_shape=jax.ShapeDtypeStruct((bsz, seq, D_HEADS * D_V), BF16),
        scratch_shapes=[pltpu.VMEM((1, tq), F32)] * n + [pltpu.VMEM((vt_rows, tq), F32)] * n,
        compiler_params=_params("parallel", "parallel", "arbitrary"),
        name="flash_latent",
    )(*([qt] * n + [kd] * n + [vt] * n))


def _merge_kernel(h_ref, x_ref, oa_ref, ob_ref, oc_ref, od_ref, wg_ref, wb_ref, wo_ref, out_ref, acc_scr):
    j = pl.program_id(1)

    @pl.when(j == 0)
    def _():
        acc_scr[...] = jnp.zeros(acc_scr.shape, F32)

    gate = jax.nn.sigmoid(jnp.dot(h_ref[...], wg_ref[...], preferred_element_type=F32))
    for br, o_ref in enumerate((oa_ref, ob_ref, oc_ref, od_ref)):
        @pl.when(j == br)
        def _(o_ref=o_ref):
            acc_scr[...] += gate * jnp.dot(o_ref[...], wb_ref[0], preferred_element_type=F32)

    @pl.when(j == N_BRANCH - 1)
    def _():
        out_ref[...] = x_ref[...] + jnp.dot(acc_scr[...].astype(BF16), wo_ref[...],
                                            preferred_element_type=F32)


def _merge(h2d, x2d, branch_outs, wg, wb, wo):
    t, d = x2d.shape
    tm = _tile(t, 512)
    bw = wb.shape[1]
    row = lambda width: pl.BlockSpec((tm, width), lambda i, j: (i, 0))
    return pl.pallas_call(
        _merge_kernel,
        grid=(t // tm, N_BRANCH),
        in_specs=[row(d), row(d), row(bw), row(bw), row(bw), row(bw),
                  pl.BlockSpec((d, d), lambda i, j: (0, j)),
                  pl.BlockSpec((1, bw, d), lambda i, j: (j, 0, 0)),
                  pl.BlockSpec((d, d), lambda i, j: (0, 0))],
        out_specs=row(d),
        out_shape=jax.ShapeDtypeStruct((t, d), F32),
        scratch_shapes=[pltpu.VMEM((tm, d), F32)],
        compiler_params=_params("parallel", "arbitrary"),
        name="gated_merge",
    )(h2d, x2d, *branch_outs, wg, wb, wo)


HALO = 8


def _ffn_kernel(x_ref, xp_ref, xn_ref, g_ref, wg_ref, wu_ref, cwg_ref, cwu_ref, cbg_ref, cbu_ref,
                wd_ref, out_ref, hn_scr, acc_scr, u_scr, *, tm, tiles_per_seq):
    i = pl.program_id(0)
    j = pl.program_id(1)
    rows = tm + 2 * HALO

    @pl.when(j == 0)
    def _():
        xe = jnp.concatenate([xp_ref[...], x_ref[...], xn_ref[...]], axis=0)
        ms = jnp.mean(xe * xe, axis=-1, keepdims=True)
        hn = xe * lax.rsqrt(ms + EPS) * g_ref[...]
        r = lax.broadcasted_iota(jnp.int32, (rows, 1), 0)
        first = (i % tiles_per_seq) == 0
        last = (i % tiles_per_seq) == tiles_per_seq - 1
        dead = ((r == HALO - 1) & first) | ((r == HALO + tm) & last)
        hn_scr[...] = jnp.where(dead, 0.0, hn).astype(BF16)
        acc_scr[...] = jnp.zeros(acc_scr.shape, F32)

    def conv(w_ref, cw_ref, cb_ref):
        u_scr[...] = jnp.dot(hn_scr[...], w_ref[...], preferred_element_type=F32)
        return (cw_ref[0:1, :] * u_scr[HALO - 1:HALO - 1 + tm, :] + cw_ref[1:2, :] * u_scr[HALO:HALO + tm, :]
                + cw_ref[2:3, :] * u_scr[HALO + 1:HALO + 1 + tm, :] + cb_ref[...])

    yg = conv(wg_ref, cwg_ref, cbg_ref)
    yu = conv(wu_ref, cwu_ref, cbu_ref)
    act = (jax.nn.silu(yg) * yu).astype(BF16)
    acc_scr[...] += jnp.dot(act, wd_ref[...], preferred_element_type=F32)

    @pl.when(j == pl.num_programs(1) - 1)
    def _():
        out_ref[...] = x_ref[...] + acc_scr[...]


def _ffn(x2d, seq, g, w_up, conv_w, conv_b, w_down):
    t, d = x2d.shape
    dff = w_down.shape[0]
    tm = _tile(seq, 512)
    tf = _tile(dff, 1408)
    nf = dff // tf
    nhb = t // HALO
    rb = tm // HALO
    return pl.pallas_call(
        functools.partial(_ffn_kernel, tm=tm, tiles_per_seq=seq // tm),
        grid=(t // tm, nf),
        in_specs=[pl.BlockSpec((tm, d), lambda i, j: (i, 0)),
                  pl.BlockSpec((HALO, d), lambda i, j: (jnp.maximum(i * rb - 1, 0), 0)),
                  pl.BlockSpec((HALO, d), lambda i, j: (jnp.minimum((i + 1) * rb, nhb - 1), 0)),
                  pl.BlockSpec((1, d), lambda i, j: (0, 0)),
                  pl.BlockSpec((d, tf), lambda i, j: (0, j)),
                  pl.BlockSpec((d, tf), lambda i, j: (0, nf + j)),
                  pl.BlockSpec((3, tf), lambda i, j: (0, j)),
                  pl.BlockSpec((3, tf), lambda i, j: (0, nf + j)),
                  pl.BlockSpec((1, tf), lambda i, j: (0, j)),
                  pl.BlockSpec((1, tf), lambda i, j: (0, nf + j)),
                  pl.BlockSpec((tf, d), lambda i, j: (j, 0))],
        out_specs=pl.BlockSpec((tm, d), lambda i, j: (i, 0)),
        out_shape=jax.ShapeDtypeStruct((t, d), F32),
        scratch_shapes=[pltpu.VMEM((tm + 2 * HALO, d), BF16), pltpu.VMEM((tm, d), F32),
                        pltpu.VMEM((tm + 2 * HALO, tf), F32)],
        compiler_params=_params("parallel", "arbitrary"),
        name="conv_mlp",
    )(x2d, x2d, x2d, g, w_up, w_up, conv_w, conv_w, conv_b, conv_b, w_down)


def _alibi_slopes(n):
    return 2.0 ** (-8.0 * np.arange(1, n + 1) / n)


def _pair_split_perm():
    cols = []
    for t in range(A_Q_HEADS // 2):
        for e in range(2):
            hd = t + (A_Q_HEADS // 2) * e
            cols.extend(range(hd * HEAD_DIM, (hd + 1) * HEAD_DIM))
    return np.asarray(cols, np.int32)


def _layer(x, pos, layer_idx, norm1_g, w_in, a_q_norm, a_k_norm, a_sink, b_q_norm, b_k_norm, b_lam_q1,
           b_lam_k1, b_lam_q2, b_lam_k2, b_subln_g, c_q_norm, c_k_norm, d_cq_norm, d_ckv_norm, d_w_uq,
           d_w_ukv, d_q_norm, d_k_norm, w_branch, w_o, norm2_g, ffn_w_up, ffn_conv_w, ffn_conv_b,
           ffn_w_down):
    bsz, seq, d = x.shape
    t = bsz * seq
    x2d = x.reshape(t, d)
    ng = len(C_PATTERNS)
    qscale = HEAD_DIM ** -0.5

    n_a_in = A_Q_HEADS * HEAD_DIM + 2 * LANES
    n_a = n_a_in + 2 * LANES
    n_b = 3 * 2 * B_HEADS * HEAD_DIM
    n_cg = 3 * C_HEADS * HEAD_DIM
    n_main = n_a + n_b + n_cg
    perm = _pair_split_perm()
    w_all = jnp.concatenate([w_in[:, :A_Q_HEADS * HEAD_DIM][:, perm],
                             w_in[:, A_Q_HEADS * HEAD_DIM:n_a_in],
                             jnp.zeros((d, n_a - n_a_in), w_in.dtype),
                             w_in[:, n_a_in:n_a_in + n_b + ng * n_cg]], axis=1).astype(BF16)
    ones = lambda n: jnp.ones((n,), F32)
    zeros = lambda n: jnp.zeros((n,), F32)
    rep = lambda gvec, n: jnp.tile(gvec.astype(F32), n)
    gains = [rep(a_q_norm, 8) * (qscale * LOG2E), rep(a_k_norm, 2), ones(128 + n_a - n_a_in),
             rep(b_q_norm, 8) * (qscale * LOG2E), rep(b_k_norm, 8), ones(512)]
    flags = [ones(512), ones(128), zeros(128 + n_a - n_a_in), ones(512), ones(512), zeros(512)]
    for _ in range(ng):
        gains += [rep(c_q_norm, 8) * (qscale * LOG2E), rep(c_k_norm, 8), ones(512)]
        flags += [ones(512), ones(512), zeros(512)]
    assert C_PATTERNS[0][1] == 1
    colgain, normflag = jnp.concatenate(gains)[None, :], jnp.concatenate(flags)[None, :]
    y_main, h2d, qt_b, vt_b = _in_proj(x2d, norm1_g[None, :].astype(F32), w_all[:, :n_main], colgain[:, :n_main],
                                       normflag[:, :n_main], bsz, n_a, n_a + 2 * 2 * B_HEADS * HEAD_DIM)
    class_cols = [(r, (gi - 1) * n_cg, n_cg) for gi, (_, r) in enumerate(C_PATTERNS) if gi > 0]
    y_classes = _class_proj(h2d, w_all[:, n_main:], colgain[:, n_main:], normflag[:, n_main:], bsz, class_cols)
    y3 = y_main.reshape(bsz, seq, n_main)

    sl_a = _alibi_slopes(A_Q_HEADS)
    half = A_Q_HEADS // 2
    slopes_a = jnp.asarray(np.stack([sl_a[:half], sl_a[half:]], axis=1) * LOG2E, F32)
    sink_a = jnp.stack([a_sink[:half], a_sink[half:]], axis=1).astype(F32) * LOG2E
    (oa,) = _band_attention(y3, pos, A_RADIUS, slopes_a, sink_a, 0, 4, 5, 1, False)

    lam_init = 0.8 - 0.6 * math.exp(-0.3 * layer_idx)
    slopes_b = jnp.asarray(_alibi_slopes(B_HEADS) * LOG2E, F32)
    sb = n_a // LANES
    score_bound_b = 1.05 * HEAD_DIM * qscale * LOG2E * jnp.max(jnp.abs(b_q_norm)) * jnp.max(jnp.abs(b_k_norm))
    ob = _flash_b(y3, qt_b, vt_b, pos, slopes_b, score_bound_b, b_lam_q1.astype(F32), b_lam_k1.astype(F32),
                  b_lam_q2.astype(F32), b_lam_k2.astype(F32), b_subln_g[None, :].astype(F32), lam_init, sb + 4)

    sl_c = _alibi_slopes(ng * C_HEADS).reshape(ng, C_HEADS // 2, 2)
    c_outs, c_lses, dilations = [], [], []
    for gi, (window, r) in enumerate(C_PATTERNS):
        radius = window // (2 * r)
        slopes_c = jnp.asarray(sl_c[gi] * LOG2E, F32)
        if r == 1:
            slab0 = (n_a + n_b) // LANES
            o_g, lse_g = _band_attention(y3, pos, radius, slopes_c, None, slab0 // 4, slab0 + 4, slab0 + 8, 4, True)
        else:
            src = y_classes[gi - 1].reshape(bsz * r, seq // r, n_cg)
            pos_r = pos.reshape(bsz, seq // r, r).transpose(0, 2, 1).reshape(bsz * r, seq // r)
            o_g, lse_g = _band_attention(src, pos_r, radius, slopes_c, None, 0, 4, 8, 4, True)
        c_outs.append(o_g)
        c_lses.append(lse_g)
        dilations.append(r)
    oc = _c_merge(c_outs, c_lses, dilations, bsz, seq).reshape(t, -1)

    n0 = n_a_in + n_b + ng * n_cg
    dq = D_NOPE + D_ROPE
    lane_of = _latent_lane_of_dim()

    def to_slab(a, dims):
        return jnp.zeros(a.shape[:-1] + (LANES,), a.dtype).at[..., lane_of[dims]].set(a)

    all_dims, nope_dims, rope_dims = np.arange(dq), np.arange(D_NOPE), np.arange(D_NOPE, dq)
    kr_cols = to_slab(w_in[:, n0 + D_Q_RANK + D_KV_RANK:n0 + D_Q_RANK + D_KV_RANK + D_ROPE], rope_dims)
    w_d = jnp.concatenate([w_in[:, n0:n0 + D_Q_RANK + D_KV_RANK], kr_cols], axis=1).astype(BF16)
    wuq = to_slab(d_w_uq.reshape(D_Q_RANK, D_HEADS, dq), all_dims).reshape(D_Q_RANK, -1).astype(BF16)
    ukv = d_w_ukv.reshape(D_KV_RANK, D_HEADS, D_NOPE + D_V)
    wuk = to_slab(ukv[:, :, :D_NOPE], nope_dims).reshape(D_KV_RANK, -1).astype(BF16)
    wuv = jnp.pad(ukv[:, :, D_NOPE:], ((0, 0), (0, 0), (0, LANES - D_V))).reshape(D_KV_RANK, -1).astype(BF16)
    gq = to_slab(d_q_norm.astype(F32) * (dq ** -0.5 * LOG2E), all_dims)[None, :]
    gk = to_slab(d_k_norm.astype(F32), all_dims)[None, :]
    halfr = D_ROPE // 2
    inv = ROPE_THETA ** (-np.arange(halfr, dtype=np.float32) / halfr)
    inv_tab = np.zeros((1, LANES), np.float32)
    inv_tab[0, lane_of[rope_dims]] = np.concatenate([inv, inv])
    pos_col = pos.astype(F32).reshape(t, 1)
    qt_d, kd, vt_d = _d_proj(h2d, w_d, d_cq_norm[None, :].astype(F32), d_ckv_norm[None, :].astype(F32),
                             wuq, wuk, wuv, gq, gk, pos_col, jnp.asarray(inv_tab), bsz)
    od = _flash_d(qt_d, kd.reshape(bsz, seq, -1), vt_d, BF16 if layer_idx == 1 else F32)

    n_gate0 = n0 + D_Q_RANK + D_KV_RANK + D_ROPE
    w_gate = w_in[:, n_gate0:].astype(BF16)
    wb = jnp.concatenate([w_branch[0][perm][None], w_branch[1:]], axis=0).astype(BF16)
    x1 = _merge(h2d, x2d, [oa.reshape(t, -1), ob.reshape(t, -1), oc, od.reshape(t, -1)],
                w_gate, wb, w_o.astype(BF16))

    x2 = _ffn(x1, seq, norm2_g[None, :].astype(F32), ffn_w_up.astype(BF16), ffn_conv_w.astype(F32),
              ffn_conv_b[None, :].astype(F32), ffn_w_down.astype(BF16))
    return x2.reshape(bsz, seq, d)


def kernel(x, positions, norm1_g, w_in, a_q_norm, a_k_norm, a_sink, b_q_norm, b_k_norm, b_lam_q1,
           b_lam_k1, b_lam_q2, b_lam_k2, b_subln_g, c_q_norm, c_k_norm, d_cq_norm, d_ckv_norm, d_w_uq,
           d_w_ukv, d_q_norm, d_k_norm, w_branch, w_o, norm2_g, ffn_w_up, ffn_conv_w, ffn_conv_b,
           ffn_w_down):
    depth = w_in.shape[0]
    per_layer = (norm1_g, w_in, a_q_norm, a_k_norm, a_sink, b_q_norm, b_k_norm, b_lam_q1, b_lam_k1,
                 b_lam_q2, b_lam_k2, b_subln_g, c_q_norm, c_k_norm, d_cq_norm, d_ckv_norm, d_w_uq,
                 d_w_ukv, d_q_norm, d_k_norm, w_branch, w_o, norm2_g, ffn_w_up, ffn_conv_w, ffn_conv_b,
                 ffn_w_down)
    for layer in range(depth):
        x = _layer(x, positions, layer, *[p[layer] for p in per_layer])
    return x
```

```python
import functools
import math

import jax
import jax.numpy as jnp
import numpy as np
from jax import lax
from jax.experimental import pallas as pl
from jax.experimental.pallas import tpu as pltpu

F32 = jnp.float32
BF16 = jnp.bfloat16

LANES = 128
HEAD_DIM = 64
A_Q_HEADS = 8
A_RADIUS = 128
B_HEADS = 4
C_PATTERNS = ((128, 1), (512, 4), (2048, 16))
C_HEADS = 8
D_HEADS = 8
D_Q_RANK = 384
D_KV_RANK = 256
D_NOPE = 64
D_ROPE = 32
D_V = 64
ROPE_THETA = 10000.0
N_BRANCH = 4
BRANCH_WIDTH = 512
EPS = 1e-6
NEG = -1e30
LOG2E = math.log2(math.e)
VMEM_LIMIT_BYTES = 56 * 1024 * 1024

NT_DIMS = (((1,), (1,)), ((), ()))
TN_DIMS = (((0,), (0,)), ((), ()))


def _tile(n, pref):
    return pref if n % pref == 0 else n


def _params(*sem):
    return pltpu.CompilerParams(dimension_semantics=sem, vmem_limit_bytes=VMEM_LIMIT_BYTES)


SEG_CHUNK = 256
B_VT_ROWS = LANES + 16


def _seg_ones():
    return jnp.asarray(np.kron(np.eye(SEG_CHUNK // HEAD_DIM), np.ones((HEAD_DIM, HEAD_DIM))), BF16)


def _head_normed(y, nf_ref, cg_ref, ones_ref):
    y2 = (y * y).astype(BF16)
    seg = jnp.concatenate(
        [jnp.dot(y2[:, c * SEG_CHUNK:(c + 1) * SEG_CHUNK], ones_ref[...], preferred_element_type=F32)
         for c in range(y.shape[1] // SEG_CHUNK)], axis=1)
    rs = lax.rsqrt(seg * (1.0 / HEAD_DIM) + EPS)
    return y * (jnp.where(nf_ref[...] > 0.0, rs, 1.0) * cg_ref[...])


def _in_proj_kernel(x_ref, g_ref, w_ref, cg_ref, nf_ref, ones_ref, y_ref, h_ref, qt_ref, vt_ref, hn_scr,
                    *, qt_tile, vt_tile):
    j = pl.program_id(1)

    @pl.when(j == 0)
    def _():
        x = x_ref[...]
        ms = jnp.mean(x * x, axis=-1, keepdims=True)
        hn = (x * lax.rsqrt(ms + EPS) * g_ref[...]).astype(BF16)
        hn_scr[...] = hn
        h_ref[...] = hn

    y = _head_normed(jnp.dot(hn_scr[...], w_ref[...], preferred_element_type=F32), nf_ref, cg_ref, ones_ref)
    y_ref[...] = y.astype(y_ref.dtype)

    @pl.when(j == qt_tile)
    def _():
        for s in range(B_HEADS):
            qt_ref[0, s] = y[:, s * LANES:(s + 1) * LANES].T.astype(qt_ref.dtype)

    @pl.when(j == vt_tile)
    def _():
        tm = y.shape[0]
        ones_row = (lax.broadcasted_iota(jnp.int32, (B_VT_ROWS - LANES, tm), 0) == 0).astype(vt_ref.dtype)
        for s in range(B_HEADS):
            vt_ref[0, s, :LANES, :] = y[:, s * LANES:(s + 1) * LANES].T.astype(vt_ref.dtype)
            vt_ref[0, s, LANES:, :] = ones_row


def _in_proj(x2d, g, w, colgain, normflag, bsz, q_col, v_col):
    t, d = x2d.shape
    n = w.shape[1]
    seq = t // bsz
    tm = _tile(seq, 1024)
    tn = _tile(n, 1024)
    assert q_col % tn == 0 and v_col % tn == 0
    nper = seq // tm
    return pl.pallas_call(
        functools.partial(_in_proj_kernel, qt_tile=q_col // tn, vt_tile=v_col // tn),
        grid=(t // tm, n // tn),
        in_specs=[
            pl.BlockSpec((tm, d), lambda i, j: (i, 0)),
            pl.BlockSpec((1, d), lambda i, j: (0, 0)),
            pl.BlockSpec((d, tn), lambda i, j: (0, j)),
            pl.BlockSpec((1, tn), lambda i, j: (0, j)),
            pl.BlockSpec((1, tn), lambda i, j: (0, j)),
            pl.BlockSpec((SEG_CHUNK, SEG_CHUNK), lambda i, j: (0, 0)),
        ],
        out_specs=[
            pl.BlockSpec((tm, tn), lambda i, j: (i, j)),
            pl.BlockSpec((tm, d), lambda i, j: (i, 0)),
            pl.BlockSpec((1, B_HEADS, LANES, tm), lambda i, j: (i // nper, 0, 0, i % nper)),
            pl.BlockSpec((1, B_HEADS, B_VT_ROWS, tm), lambda i, j: (i // nper, 0, 0, i % nper)),
        ],
        out_shape=[jax.ShapeDtypeStruct((t, n), BF16), jax.ShapeDtypeStruct((t, d), BF16),
                   jax.ShapeDtypeStruct((bsz, B_HEADS, LANES, seq), BF16),
                   jax.ShapeDtypeStruct((bsz, B_HEADS, B_VT_ROWS, seq), BF16)],
        scratch_shapes=[pltpu.VMEM((tm, d), BF16)],
        compiler_params=_params("parallel", "arbitrary"),
        name="in_proj",
    )(x2d, g, w, colgain, normflag, _seg_ones())


def _class_proj_kernel(h_ref, w_ref, cg_ref, nf_ref, ones_ref, *rest, class_groups):
    class_refs, y_scr = rest[:len(class_groups)], rest[len(class_groups)]
    j = pl.program_id(1)
    y = _head_normed(jnp.dot(h_ref[...], w_ref[...], preferred_element_type=F32), nf_ref, cg_ref, ones_ref)
    tm, tn = y.shape
    for s in range(tn // LANES):
        y_scr[s] = y[:, s * LANES:(s + 1) * LANES]
    for c_ref, (r, tile0, n_tiles) in zip(class_refs, class_groups):
        @pl.when((j >= tile0) & (j < tile0 + n_tiles))
        def _(c_ref=c_ref, r=r):
            for c in range(r):
                for s in range(tn // LANES):
                    c_ref[0, c, :, s * LANES:(s + 1) * LANES] = (
                        y_scr[s, pl.ds(c, tm // r, stride=r), :].astype(c_ref.dtype))


def _class_proj(h2d, w, colgain, normflag, bsz, class_cols):
    t, d = h2d.shape
    n = w.shape[1]
    seq = t // bsz
    tm = _tile(seq, 1024)
    tn = math.gcd(*[width for _, _, width in class_cols])
    assert n % tn == 0 and all(st % tn == 0 and wd % tn == 0 and tm % r == 0 for r, st, wd in class_cols)
    nper = seq // tm
    class_groups = tuple((r, start // tn, width // tn) for r, start, width in class_cols)

    def class_spec(r, tile0, n_tiles):
        return pl.BlockSpec((1, r, tm // r, tn),
                            lambda i, j: (i // nper, 0, i % nper, jnp.clip(j - tile0, 0, n_tiles - 1)))

    return pl.pallas_call(
        functools.partial(_class_proj_kernel, class_groups=class_groups),
        grid=(t // tm, n // tn),
        in_specs=[
            pl.BlockSpec((tm, d), lambda i, j: (i, 0)),
            pl.BlockSpec((d, tn), lambda i, j: (0, j)),
            pl.BlockSpec((1, tn), lambda i, j: (0, j)),
            pl.BlockSpec((1, tn), lambda i, j: (0, j)),
            pl.BlockSpec((SEG_CHUNK, SEG_CHUNK), lambda i, j: (0, 0)),
        ],
        out_specs=[class_spec(*grp) for grp in class_groups],
        out_shape=[jax.ShapeDtypeStruct((bsz, r, seq // r, width), BF16) for r, _, width in class_cols],
        scratch_shapes=[pltpu.VMEM((tn // LANES, tm, LANES), F32)],
        compiler_params=_params("parallel", "arbitrary"),
        name="class_proj",
    )(h2d, w, colgain, normflag, _seg_ones())


D_ROT_LO = 48
D_ROT_HI = D_ROT_LO + LANES // 2


def _latent_lane_of_dim():
    half = D_ROPE // 2
    nope = list(range(D_ROT_LO)) + list(range(LANES // 2, LANES // 2 + D_NOPE - D_ROT_LO))
    return np.asarray(nope + list(range(D_ROT_LO, D_ROT_LO + half)) + list(range(D_ROT_HI, D_ROT_HI + half)))


def _slab_sums(x2_bf16, ones_ref):
    return jnp.concatenate(
        [jnp.dot(x2_bf16[:, c * SEG_CHUNK:(c + 1) * SEG_CHUNK], ones_ref[...], preferred_element_type=F32)
         for c in range(x2_bf16.shape[1] // SEG_CHUNK)], axis=1)


def _d_proj_kernel(h_ref, wd_ref, gcq_ref, gckv_ref, wuq_ref, wuk_ref, wuv_ref, gq_ref, gk_ref,
                   pos_ref, inv_ref, ones_ref, qt_ref, k_ref, vt_ref):
    y = jnp.dot(h_ref[...], wd_ref[...], preferred_element_type=F32)
    cq = y[:, :D_Q_RANK]
    ckv = y[:, D_Q_RANK:D_Q_RANK + D_KV_RANK]
    kr = y[:, D_Q_RANK + D_KV_RANK:]
    cqn = (cq * lax.rsqrt(jnp.mean(cq * cq, axis=-1, keepdims=True) + EPS) * gcq_ref[...]).astype(BF16)
    ckvn = (ckv * lax.rsqrt(jnp.mean(ckv * ckv, axis=-1, keepdims=True) + EPS) * gckv_ref[...]).astype(BF16)
    q = jnp.dot(cqn, wuq_ref[...], preferred_element_type=F32)
    k = jnp.dot(ckvn, wuk_ref[...], preferred_element_type=F32) + jnp.concatenate([kr] * D_HEADS, axis=1)
    v = jnp.dot(ckvn, wuv_ref[...], preferred_element_type=F32)
    inv_w = 1.0 / (D_NOPE + D_ROPE)
    q = q * lax.rsqrt(_slab_sums((q * q).astype(BF16), ones_ref) * inv_w + EPS)
    k = k * lax.rsqrt(_slab_sums((k * k).astype(BF16), ones_ref) * inv_w + EPS)

    lane = lax.broadcasted_iota(jnp.int32, (1, LANES), 1)
    ang = pos_ref[...] * inv_ref[...]
    half = D_ROPE // 2
    lo = (lane >= D_ROT_LO) & (lane < D_ROT_LO + half)
    hi = (lane >= D_ROT_HI) & (lane < D_ROT_HI + half)
    c_tab = jnp.where(lo | hi, jnp.cos(ang), 1.0)
    sn = jnp.sin(ang)
    s_tab = jnp.where(lo, -sn, jnp.where(hi, sn, 0.0))
    rows = vt_ref.shape[2]
    for hd in range(D_HEADS):
        sl = slice(hd * LANES, (hd + 1) * LANES)
        qs = q[:, sl] * gq_ref[...]
        qt_ref[0, hd] = (qs * c_tab + pltpu.roll(qs, LANES // 2, 1) * s_tab).T.astype(qt_ref.dtype)
        ks = k[:, sl] * gk_ref[...]
        k_ref[:, sl] = (ks * c_tab + pltpu.roll(ks, LANES // 2, 1) * s_tab).astype(k_ref.dtype)
        vt_ref[0, hd] = jnp.where(lane == D_V, 1.0, v[:, sl]).T[:rows].astype(vt_ref.dtype)


def _d_proj(h2d, wd, gcq, gckv, wuq, wuk, wuv, gq, gk, pos_col, inv_tab, bsz):
    vt_rows = D_VT_ROWS
    t, d = h2d.shape
    seq = t // bsz
    tm = _tile(seq, 512)
    nper = seq // tm
    nd = D_HEADS * LANES
    slab_ones = jnp.asarray(np.kron(np.eye(SEG_CHUNK // LANES), np.ones((LANES, LANES))), BF16)
    full = lambda a: pl.BlockSpec(a.shape, lambda i: (0,) * a.ndim)
    return pl.pallas_call(
        _d_proj_kernel,
        grid=(t // tm,),
        in_specs=[pl.BlockSpec((tm, d), lambda i: (i, 0)), full(wd), full(gcq), full(gckv), full(wuq),
                  full(wuk), full(wuv), full(gq), full(gk), pl.BlockSpec((tm, 1), lambda i: (i, 0)),
                  full(inv_tab), full(slab_ones)],
        out_specs=[pl.BlockSpec((1, D_HEADS, LANES, tm), lambda i: (i // nper, 0, 0, i % nper)),
                   pl.BlockSpec((tm, nd), lambda i: (i, 0)),
                   pl.BlockSpec((1, D_HEADS, vt_rows, tm), lambda i: (i // nper, 0, 0, i % nper))],
        out_shape=[jax.ShapeDtypeStruct((bsz, D_HEADS, LANES, seq), BF16),
                   jax.ShapeDtypeStruct((t, nd), BF16),
                   jax.ShapeDtypeStruct((bsz, D_HEADS, vt_rows, seq), BF16)],
        compiler_params=_params("parallel"),
        name="d_proj",
    )(h2d, wd, gcq, gckv, wuq, wuk, wuv, gq, gk, pos_col, inv_tab, slab_ones)


N_QSLAB = 4
BAND_GQ = 128


def _band_kernel(slope_ref, sink_ref, q_ref, kp_ref, kc_ref, kn_ref, vp_ref, vc_ref, vn_ref,
                 posq_ref, posw_ref, *out_refs, radius, tq, gq, length, nkv, has_sink, want_lse):
    i = pl.program_id(1)
    kw = jnp.concatenate([kp_ref[0], kc_ref[0], kn_ref[0]], axis=0)
    vw = jnp.concatenate([vp_ref[0], vc_ref[0], vn_ref[0]], axis=0)
    wg = gq + 2 * radius
    c_io = lax.broadcasted_iota(jnp.int32, (wg, gq), 0)
    r_io = lax.broadcasted_iota(jnp.int32, (wg, gq), 1)
    in_band = jnp.abs(c_io - radius - r_io) <= radius
    lane = lax.broadcasted_iota(jnp.int32, (gq, LANES), 1)
    row = lax.broadcasted_iota(jnp.int32, (LANES, gq), 0)
    for g in range(tq // gq):
        rows = slice(g * gq, g * gq + wg)
        cols = slice(g * gq, (g + 1) * gq)
        jabs = i * tq + g * gq - radius + c_io
        mask = in_band & (jabs >= 0) & (jabs < length)
        dist = jnp.abs(posw_ref[0, 0, rows, :] - posq_ref[0, :, cols])
        for t in range(N_QSLAB):
            q = q_ref[0, cols, t * LANES:(t + 1) * LANES]
            kt = t if nkv == N_QSLAB else 0
            ks = kw[rows, kt * LANES:(kt + 1) * LANES]
            vs = vw[rows, kt * LANES:(kt + 1) * LANES]
            outs, lses = [], []
            for e in range(2):
                in_half = (lane >= HEAD_DIM) if e else (lane < HEAD_DIM)
                qe = jnp.where(in_half, q, jnp.zeros_like(q))
                s = lax.dot_general(ks, qe, NT_DIMS, preferred_element_type=F32)
                s = jnp.where(mask, s - slope_ref[t, e] * dist, NEG)
                m = jnp.max(s, axis=0, keepdims=True)
                if has_sink:
                    sk = sink_ref[t, e]
                    m = jnp.maximum(m, sk)
                p = jnp.exp2(s - m)
                den = jnp.sum(p, axis=0, keepdims=True)
                if has_sink:
                    den = den + jnp.exp2(sk - m)
                ot = lax.dot_general(vs, p.astype(BF16), TN_DIMS, preferred_element_type=F32)
                outs.append(ot / den)
                lses.append((m + jnp.log2(den)) * (1.0 / LOG2E))
            sl = slice(t * LANES, (t + 1) * LANES)
            out_refs[0][0, cols, sl] = jnp.where(row < HEAD_DIM, outs[0], outs[1]).T.astype(out_refs[0].dtype)
            if want_lse:
                out_refs[1][0, cols, sl] = jnp.where(row < HEAD_DIM, lses[0], lses[1]).T


def _band_attention(src, pos, radius, slopes, sink, q_blk, k_slab0, v_slab0, nkv, want_lse):
    bn, length, _ = src.shape
    tq = _tile(length, 512)
    nb = length // tq
    rb = tq // radius
    nrb = length // radius
    w = tq + 2 * radius
    kvw = nkv * LANES
    idx = jnp.clip(jnp.arange(nb)[:, None] * tq - radius + jnp.arange(w)[None, :], 0, length - 1)
    posf = pos.astype(F32)
    posw = posf[:, idx][:, :, :, None]
    posq = posf[:, None, :]
    has_sink = sink is not None
    if sink is None:
        sink = jnp.zeros((N_QSLAB, 2), F32)

    def halo(slab0, which):
        blk = slab0 // nkv
        if which == 0:
            return pl.BlockSpec((1, radius, kvw), lambda b, i: (b, jnp.maximum(i * rb - 1, 0), blk))
        if which == 1:
            return pl.BlockSpec((1, tq, kvw), lambda b, i: (b, i, blk))
        return pl.BlockSpec((1, radius, kvw), lambda b, i: (b, jnp.minimum((i + 1) * rb, nrb - 1), blk))

    smem = pl.BlockSpec(memory_space=pltpu.SMEM)
    qw = N_QSLAB * LANES
    out_specs = [pl.BlockSpec((1, tq, qw), lambda b, i: (b, i, 0))]
    out_shape = [jax.ShapeDtypeStruct((bn, length, qw), BF16)]
    if want_lse:
        out_specs.append(pl.BlockSpec((1, tq, qw), lambda b, i: (b, i, 0)))
        out_shape.append(jax.ShapeDtypeStruct((bn, length, qw), F32))
    return pl.pallas_call(
        functools.partial(_band_kernel, radius=radius, tq=tq, gq=min(BAND_GQ, tq), length=length, nkv=nkv,
                          has_sink=has_sink, want_lse=want_lse),
        grid=(bn, nb),
        in_specs=[smem, smem,
                  pl.BlockSpec((1, tq, qw), lambda b, i: (b, i, q_blk)),
                  halo(k_slab0, 0), halo(k_slab0, 1), halo(k_slab0, 2),
                  halo(v_slab0, 0), halo(v_slab0, 1), halo(v_slab0, 2),
                  pl.BlockSpec((1, 1, tq), lambda b, i: (b, 0, i)),
                  pl.BlockSpec((1, 1, w, 1), lambda b, i: (b, i, 0, 0))],
        out_specs=out_specs,
        out_shape=out_shape,
        compiler_params=_params("parallel", "arbitrary"),
        name="band_attn",
    )(slopes, sink, src, src, src, src, src, src, src, posq, posw)


def _c_merge_kernel(*refs, dilations, tm):
    ng = len(dilations)
    o_refs, l_refs, out_ref = refs[:ng], refs[ng:2 * ng], refs[2 * ng]
    o_scr, l_scr = refs[2 * ng + 1:]

    def natural(ref, scr, r):
        if r == 1:
            return ref[0].astype(F32)
        nslab = scr.shape[0]
        for c in range(r):
            blk = ref[0, c].astype(F32)
            for s in range(nslab):
                scr[s, pl.ds(c, tm // r, stride=r), :] = blk[:, s * LANES:(s + 1) * LANES]
        return jnp.concatenate([scr[s] for s in range(nslab)], axis=1)

    ls = [natural(l_refs[g], l_scr, r) for g, r in enumerate(dilations)]
    m = functools.reduce(jnp.maximum, ls)
    ws = [jnp.exp(l - m) for l in ls]
    num = sum(w * natural(o_refs[g], o_scr, r) for g, (w, r) in enumerate(zip(ws, dilations)))
    out_ref[0] = (num / sum(ws)).astype(out_ref.dtype)


def _c_merge(outs, lses, dilations, bsz, seq):
    n = outs[0].shape[-1]
    tm = _tile(seq, 512)
    views, specs = [], []
    for arrs in (outs, lses):
        for a, r in zip(arrs, dilations):
            if r == 1:
                views.append(a.reshape(bsz, seq, n))
                specs.append(pl.BlockSpec((1, tm, n), lambda b, i: (b, i, 0)))
            else:
                views.append(a.reshape(bsz, r, seq // r, n))
                specs.append(pl.BlockSpec((1, r, tm // r, n), lambda b, i: (b, 0, i, 0)))
    return pl.pallas_call(
        functools.partial(_c_merge_kernel, dilations=tuple(dilations), tm=tm),
        grid=(bsz, seq // tm),
        in_specs=specs,
        out_specs=pl.BlockSpec((1, tm, n), lambda b, i: (b, i, 0)),
        out_shape=jax.ShapeDtypeStruct((bsz, seq, n), BF16),
        scratch_shapes=[pltpu.VMEM((n // LANES, tm, LANES), F32), pltpu.VMEM((n // LANES, tm, LANES), F32)],
        compiler_params=_params("parallel", "parallel"),
        name="c_merge",
    )(*views)


def _flash_loop(n_trips, score_fn, value_fn, m_refs, acc_refs, chunk_of=lambda trip: trip):
    def body(trip, carry):
        j = chunk_of(trip)
        scores = score_fn(j)
        for u, (s, vt_c) in enumerate(zip(scores, value_fn(j))):
            m_old = m_refs[u][...]
            m_new = jnp.maximum(m_old, jnp.max(s, axis=0, keepdims=True))
            alpha = jnp.exp2(m_old - m_new)
            p = jnp.exp2(s - m_new).astype(BF16)
            acc_refs[u][...] = alpha * acc_refs[u][...] + jnp.dot(vt_c, p, preferred_element_type=F32)
            m_refs[u][...] = m_new
        return carry

    lax.fori_loop(0, n_trips, body, 0)


def _flash_init(m_refs, acc_refs):
    for m_ref, acc_ref in zip(m_refs, acc_refs):
        m_ref[...] = jnp.full(m_ref.shape, NEG, F32)
        acc_ref[...] = jnp.zeros(acc_ref.shape, F32)


B_HEAD_GROUPS = ((2, 3), (1,), (0,))


def _flash_b_kernel(slope_ref, cnt_ref, lst_ref, qt_ref, k_ref, vt_ref, posq_ref, posk_ref, lq1_ref, lk1_ref,
                    lq2_ref, lk2_ref, subg_ref, o_ref, *scratch, tq, tk, seq, lam_init):
    n_units = 2 * B_HEADS
    n_chunks = seq // tk
    m_refs, acc_refs = scratch[:n_units], scratch[n_units:]
    _flash_init(m_refs, acc_refs)
    row = lax.broadcasted_iota(jnp.int32, (LANES, tq), 0)
    slab_of = [2 * (u % 2) + (u // 2) // 2 for u in range(n_units)]
    qs = []
    for u in range(n_units):
        qm = qt_ref[0, slab_of[u]]
        qs.append(jnp.where((row >= HEAD_DIM) if (u // 2) % 2 else (row < HEAD_DIM), qm, jnp.zeros_like(qm)))
    posq = posq_ref[0]
    tile = pl.program_id(0) * pl.num_programs(1) + pl.program_id(1)

    for g, heads in enumerate(B_HEAD_GROUPS):
        units = [2 * hd + mp for hd in heads for mp in range(2)]
        row0 = (tile * len(B_HEAD_GROUPS) + g)

        def score_fn(j, heads=heads):
            off = pl.multiple_of(j * tk, tk)
            posk = posk_ref[0, pl.ds(off, tk), :]
            dist = jnp.abs(jnp.concatenate([posk] * (tq // LANES), axis=1) - posq)
            k_c = k_ref[0, pl.ds(off, tk), :]
            scores = []
            for hd in heads:
                bias = slope_ref[hd] * dist
                for u in (2 * hd, 2 * hd + 1):
                    sl = slab_of[u]
                    scores.append(jnp.dot(k_c[:, sl * LANES:(sl + 1) * LANES], qs[u],
                                          preferred_element_type=F32) - bias)
            return scores

        def value_fn(j, heads=heads):
            off = pl.multiple_of(j * tk, tk)
            return [vt_ref[0, hd, :, pl.ds(off, tk)] for hd in heads for _ in range(2)]

        _flash_loop(cnt_ref[row0], score_fn, value_fn, [m_refs[u] for u in units], [acc_refs[u] for u in units],
                    chunk_of=lambda trip, row0=row0: lst_ref[row0 * n_chunks + trip])

    def normalised(u):
        acc = acc_refs[u][...]
        return acc[:LANES] / acc[LANES:LANES + 1]

    for hd in range(B_HEADS):
        u0, u1 = 2 * hd, 2 * hd + 1
        lam1 = jnp.sum(lq1_ref[hd:hd + 1, :] * lk1_ref[hd:hd + 1, :], axis=-1, keepdims=True)
        lam2 = jnp.sum(lq2_ref[hd:hd + 1, :] * lk2_ref[hd:hd + 1, :], axis=-1, keepdims=True)
        lam = jnp.exp(lam1) - jnp.exp(lam2) + lam_init
        a = (normalised(u0) - lam * normalised(u1)).T
        ms = jnp.mean(a * a, axis=-1, keepdims=True)
        out = a * lax.rsqrt(ms + EPS) * subg_ref[...] * (1.0 - lam_init)
        o_ref[0, :, hd * LANES:(hd + 1) * LANES] = out.astype(o_ref.dtype)


UNDERFLOW_LOG2 = 150.0


def _alibi_chunk_lists(posf, slopes, score_bound, tq, tk):
    bsz, seq = posf.shape
    qmin, qmax = posf.reshape(bsz, -1, tq).min(-1), posf.reshape(bsz, -1, tq).max(-1)
    kmin, kmax = posf.reshape(bsz, -1, tk).min(-1), posf.reshape(bsz, -1, tk).max(-1)
    dmin = jnp.maximum(jnp.maximum(qmin[:, :, None] - kmax[:, None, :], kmin[:, None, :] - qmax[:, :, None]), 0.0)
    needed = slopes[None, None, :, None] * dmin[:, :, None, :] < 2.0 * score_bound + UNDERFLOW_LOG2
    group = jnp.stack([functools.reduce(jnp.logical_or, [needed[:, :, hd] for hd in heads])
                       for heads in B_HEAD_GROUPS], axis=2)
    order = jnp.argsort(jnp.logical_not(group), axis=-1, stable=True).astype(jnp.int32)
    return group.sum(-1).astype(jnp.int32).reshape(-1), order.reshape(-1)


def _flash_b(y_main, qt, vt, pos, slopes, score_bound, lq1, lk1, lq2, lk2, subg, lam_init, k_slab0):
    bsz, seq, _ = y_main.shape
    tq = _tile(seq, 512)
    tk = _tile(seq, 512)
    nq = B_HEADS
    posf = pos.astype(F32)
    posq = posf[:, None, :]
    posk = jnp.broadcast_to(posf[:, :, None], (bsz, seq, LANES))
    counts, chunks = _alibi_chunk_lists(posf, slopes, score_bound, tq, tk)
    once = pl.Buffered(1)
    small = lambda a: pl.BlockSpec(a.shape, lambda b, i: (0,) * a.ndim)
    smem = pl.BlockSpec(memory_space=pltpu.SMEM)
    n_units = 2 * nq
    return pl.pallas_call(
        functools.partial(_flash_b_kernel, tq=tq, tk=tk, seq=seq, lam_init=lam_init),
        grid=(bsz, seq // tq),
        in_specs=[smem, smem, smem,
                  pl.BlockSpec((1, nq, LANES, tq), lambda b, i: (b, 0, 0, i)),
                  pl.BlockSpec((1, seq, nq * LANES), lambda b, i: (b, 0, k_slab0 // nq), pipeline_mode=once),
                  pl.BlockSpec((1, B_HEADS, B_VT_ROWS, seq), lambda b, i: (b, 0, 0, 0), pipeline_mode=once),
                  pl.BlockSpec((1, 1, tq), lambda b, i: (b, 0, i)),
                  pl.BlockSpec((1, seq, LANES), lambda b, i: (b, 0, 0), pipeline_mode=once),
                  small(lq1), small(lk1), small(lq2), small(lk2), small(subg)],
        out_specs=pl.BlockSpec((1, tq, B_HEADS * LANES), lambda b, i: (b, i, 0)),
        out_shape=jax.ShapeDtypeStruct((bsz, seq, B_HEADS * LANES), BF16),
        scratch_shapes=[pltpu.VMEM((1, tq), F32)] * n_units + [pltpu.VMEM((B_VT_ROWS, tq), F32)] * n_units,
        compiler_params=_params("parallel", "arbitrary"),
        name="flash_diff",
    )(slopes, counts, chunks, qt, y_main, vt, posq, posk, lq1, lk1, lq2, lk2, subg)


D_VT_ROWS = D_V + 16
D_GROUP = 8


def _flash_d_kernel(*refs, tq, tk, seq):
    n = D_GROUP
    q_refs, k_refs, vt_refs = refs[:n], refs[n:2 * n], refs[2 * n:3 * n]
    o_ref = refs[3 * n]
    scratch = refs[3 * n + 1:]
    m_refs, acc_refs = scratch[:n], scratch[n:]
    _flash_init(m_refs, acc_refs)
    qs = [q_ref[0, 0] for q_ref in q_refs]

    def score_fn(j):
        off = pl.multiple_of(j * tk, tk)
        return [jnp.dot(k_refs[e][0, pl.ds(off, tk), :], qs[e], preferred_element_type=F32) for e in range(n)]

    def value_fn(j):
        off = pl.multiple_of(j * tk, tk)
        return [vt_refs[e][0, 0, :, pl.ds(off, tk)] for e in range(n)]

    _flash_loop(seq // tk, score_fn, value_fn, m_refs, acc_refs)
    outs = []
    for e in range(n):
        acc = acc_refs[e][...]
        outs.append(acc[:D_V] / acc[D_V:D_V + 1])
    o_ref[0] = jnp.concatenate(outs, axis=0).T.astype(o_ref.dtype)


def _flash_d(qt, kd, vt):
    bsz, seq, _ = kd.shape
    tq = _tile(seq, 512)
    tk = _tile(seq, 512)
    n = D_GROUP
    once = pl.Buffered(1)
    q_spec = lambda e: pl.BlockSpec((1, 1, LANES, tq), lambda b, hg, i: (b, n * hg + e, 0, i))
    k_spec = lambda e: pl.BlockSpec((1, seq, LANES), lambda b, hg, i: (b, 0, n * hg + e), pipeline_mode=once)
    vt_rows = vt.shape[2]
    vt_spec = lambda e: pl.BlockSpec((1, 1, vt_rows, seq), lambda b, hg, i: (b, n * hg + e, 0, 0),
                                     pipeline_mode=once)
    return pl.pallas_call(
        functools.partial(_flash_d_kernel, tq=tq, tk=tk, seq=seq),
        grid=(bsz, D_HEADS // n, seq // tq),
        in_specs=[q_spec(e) for e in range(n)] + [k_spec(e) for e in range(n)]
                 + [vt_spec(e) for e in range(n)],
        out_specs=pl.BlockSpec((1, tq, n * D_V), lambda b, hg, i: (b, i, hg)),
        out_shape=jax.ShapeDtypeStruct((bsz, seq, D_HEADS * D_V), BF16),
        scratch_shapes=[pltpu.VMEM((1, tq), F32)] * n + [pltpu.VMEM((vt_rows, tq), F32)] * n,
        compiler_params=_params("parallel", "parallel", "arbitrary"),
        name="flash_latent",
    )(*([qt] * n + [kd] * n + [vt] * n))


def _merge_kernel(h_ref, x_ref, oa_ref, ob_ref, oc_ref, od_ref, wg_ref, wb_ref, wo_ref, out_ref):
    d = x_ref.shape[1]
    merged = None
    for br, o_ref in enumerate((oa_ref, ob_ref, oc_ref, od_ref)):
        gate = jax.nn.sigmoid(jnp.dot(h_ref[...], wg_ref[:, br * d:(br + 1) * d], preferred_element_type=F32))
        term = gate * jnp.dot(o_ref[...], wb_ref[br], preferred_element_type=F32)
        merged = term if merged is None else merged + term
    out_ref[...] = x_ref[...] + jnp.dot(merged.astype(BF16), wo_ref[...], preferred_element_type=F32)


def _merge(h2d, x2d, branch_outs, wg, wb, wo):
    t, d = x2d.shape
    tm = _tile(t, 512)
    bw = wb.shape[1]
    row = lambda width: pl.BlockSpec((tm, width), lambda i: (i, 0))
    resident = lambda a: pl.BlockSpec(a.shape, lambda i: (0,) * a.ndim, pipeline_mode=pl.Buffered(1))
    return pl.pallas_call(
        _merge_kernel,
        grid=(t // tm,),
        in_specs=[row(d), row(d), row(bw), row(bw), row(bw), row(bw), resident(wg), resident(wb), resident(wo)],
        out_specs=row(d),
        out_shape=jax.ShapeDtypeStruct((t, d), F32),
        compiler_params=_params("parallel"),
        name="gated_merge",
    )(h2d, x2d, *branch_outs, wg, wb, wo)


HALO = 8


def _ffn_kernel(x_ref, xp_ref, xn_ref, g_ref, wg_ref, wu_ref, cwg_ref, cwu_ref, cbg_ref, cbu_ref,
                wd_ref, out_ref, hn_scr, acc_scr, u_scr, *, tm, tiles_per_seq):
    i = pl.program_id(0)
    j = pl.program_id(1)
    rows = tm + 2 * HALO

    @pl.when(j == 0)
    def _():
        xe = jnp.concatenate([xp_ref[...], x_ref[...], xn_ref[...]], axis=0)
        ms = jnp.mean(xe * xe, axis=-1, keepdims=True)
        hn = xe * lax.rsqrt(ms + EPS) * g_ref[...]
        r = lax.broadcasted_iota(jnp.int32, (rows, 1), 0)
        first = (i % tiles_per_seq) == 0
        last = (i % tiles_per_seq) == tiles_per_seq - 1
        dead = ((r == HALO - 1) & first) | ((r == HALO + tm) & last)
        hn_scr[...] = jnp.where(dead, 0.0, hn).astype(BF16)
        acc_scr[...] = jnp.zeros(acc_scr.shape, F32)

    def conv(w_ref, cw_ref, cb_ref):
        u_scr[...] = jnp.dot(hn_scr[...], w_ref[...], preferred_element_type=F32)
        return (cw_ref[0:1, :] * u_scr[HALO - 1:HALO - 1 + tm, :] + cw_ref[1:2, :] * u_scr[HALO:HALO + tm, :]
                + cw_ref[2:3, :] * u_scr[HALO + 1:HALO + 1 + tm, :] + cb_ref[...])

    yg = conv(wg_ref, cwg_ref, cbg_ref)
    yu = conv(wu_ref, cwu_ref, cbu_ref)
    act = (jax.nn.silu(yg) * yu).astype(BF16)
    acc_scr[...] += jnp.dot(act, wd_ref[...], preferred_element_type=F32)

    @pl.when(j == pl.num_programs(1) - 1)
    def _():
        out_ref[...] = x_ref[...] + acc_scr[...]


def _ffn(x2d, seq, g, w_up, conv_w, conv_b, w_down):
    t, d = x2d.shape
    dff = w_down.shape[0]
    tm = _tile(seq, 512)
    tf = _tile(dff, 1408)
    nf = dff // tf
    nhb = t // HALO
    rb = tm // HALO
    return pl.pallas_call(
        functools.partial(_ffn_kernel, tm=tm, tiles_per_seq=seq // tm),
        grid=(t // tm, nf),
        in_specs=[pl.BlockSpec((tm, d), lambda i, j: (i, 0)),
                  pl.BlockSpec((HALO, d), lambda i, j: (jnp.maximum(i * rb - 1, 0), 0)),
                  pl.BlockSpec((HALO, d), lambda i, j: (jnp.minimum((i + 1) * rb, nhb - 1), 0)),
                  pl.BlockSpec((1, d), lambda i, j: (0, 0)),
                  pl.BlockSpec((d, tf), lambda i, j: (0, j)),
                  pl.BlockSpec((d, tf), lambda i, j: (0, nf + j)),
                  pl.BlockSpec((3, tf), lambda i, j: (0, j)),
                  pl.BlockSpec((3, tf), lambda i, j: (0, nf + j)),
                  pl.BlockSpec((1, tf), lambda i, j: (0, j)),
                  pl.BlockSpec((1, tf), lambda i, j: (0, nf + j)),
                  pl.BlockSpec((tf, d), lambda i, j: (j, 0))],
        out_specs=pl.BlockSpec((tm, d), lambda i, j: (i, 0)),
        out_shape=jax.ShapeDtypeStruct((t, d), F32),
        scratch_shapes=[pltpu.VMEM((tm + 2 * HALO, d), BF16), pltpu.VMEM((tm, d), F32),
                        pltpu.VMEM((tm + 2 * HALO, tf), F32)],
        compiler_params=_params("parallel", "arbitrary"),
        name="conv_mlp",
    )(x2d, x2d, x2d, g, w_up, w_up, conv_w, conv_w, conv_b, conv_b, w_down)


def _alibi_slopes(n):
    return 2.0 ** (-8.0 * np.arange(1, n + 1) / n)


def _pair_split_perm():
    cols = []
    for t in range(A_Q_HEADS // 2):
        for e in range(2):
            hd = t + (A_Q_HEADS // 2) * e
            cols.extend(range(hd * HEAD_DIM, (hd + 1) * HEAD_DIM))
    return np.asarray(cols, np.int32)


def _layer(x, pos, layer_idx, norm1_g, w_in, a_q_norm, a_k_norm, a_sink, b_q_norm, b_k_norm, b_lam_q1,
           b_lam_k1, b_lam_q2, b_lam_k2, b_subln_g, c_q_norm, c_k_norm, d_cq_norm, d_ckv_norm, d_w_uq,
           d_w_ukv, d_q_norm, d_k_norm, w_branch, w_o, norm2_g, ffn_w_up, ffn_conv_w, ffn_conv_b,
           ffn_w_down):
    bsz, seq, d = x.shape
    t = bsz * seq
    x2d = x.reshape(t, d)
    ng = len(C_PATTERNS)
    qscale = HEAD_DIM ** -0.5

    n_a_in = A_Q_HEADS * HEAD_DIM + 2 * LANES
    n_a = n_a_in + 2 * LANES
    n_b = 3 * 2 * B_HEADS * HEAD_DIM
    n_cg = 3 * C_HEADS * HEAD_DIM
    n_main = n_a + n_b + n_cg
    perm = _pair_split_perm()
    w_all = jnp.concatenate([w_in[:, :A_Q_HEADS * HEAD_DIM][:, perm],
                             w_in[:, A_Q_HEADS * HEAD_DIM:n_a_in],
                             jnp.zeros((d, n_a - n_a_in), w_in.dtype),
                             w_in[:, n_a_in:n_a_in + n_b + ng * n_cg]], axis=1).astype(BF16)
    ones = lambda n: jnp.ones((n,), F32)
    zeros = lambda n: jnp.zeros((n,), F32)
    rep = lambda gvec, n: jnp.tile(gvec.astype(F32), n)
    gains = [rep(a_q_norm, 8) * (qscale * LOG2E), rep(a_k_norm, 2), ones(128 + n_a - n_a_in),
             rep(b_q_norm, 8) * (qscale * LOG2E), rep(b_k_norm, 8), ones(512)]
    flags = [ones(512), ones(128), zeros(128 + n_a - n_a_in), ones(512), ones(512), zeros(512)]
    for _ in range(ng):
        gains += [rep(c_q_norm, 8) * (qscale * LOG2E), rep(c_k_norm, 8), ones(512)]
        flags += [ones(512), ones(512), zeros(512)]
    assert C_PATTERNS[0][1] == 1
    colgain, normflag = jnp.concatenate(gains)[None, :], jnp.concatenate(flags)[None, :]
    y_main, h2d, qt_b, vt_b = _in_proj(x2d, norm1_g[None, :].astype(F32), w_all[:, :n_main], colgain[:, :n_main],
                                       normflag[:, :n_main], bsz, n_a, n_a + 2 * 2 * B_HEADS * HEAD_DIM)
    class_cols = [(r, (gi - 1) * n_cg, n_cg) for gi, (_, r) in enumerate(C_PATTERNS) if gi > 0]
    y_classes = _class_proj(h2d, w_all[:, n_main:], colgain[:, n_main:], normflag[:, n_main:], bsz, class_cols)
    y3 = y_main.reshape(bsz, seq, n_main)

    sl_a = _alibi_slopes(A_Q_HEADS)
    half = A_Q_HEADS // 2
    slopes_a = jnp.asarray(np.stack([sl_a[:half], sl_a[half:]], axis=1) * LOG2E, F32)
    sink_a = jnp.stack([a_sink[:half], a_sink[half:]], axis=1).astype(F32) * LOG2E
    (oa,) = _band_attention(y3, pos, A_RADIUS, slopes_a, sink_a, 0, 4, 5, 1, False)

    lam_init = 0.8 - 0.6 * math.exp(-0.3 * layer_idx)
    slopes_b = jnp.asarray(_alibi_slopes(B_HEADS) * LOG2E, F32)
    sb = n_a // LANES
    score_bound_b = 1.05 * HEAD_DIM * qscale * LOG2E * jnp.max(jnp.abs(b_q_norm)) * jnp.max(jnp.abs(b_k_norm))
    ob = _flash_b(y3, qt_b, vt_b, pos, slopes_b, score_bound_b, b_lam_q1.astype(F32), b_lam_k1.astype(F32),
                  b_lam_q2.astype(F32), b_lam_k2.astype(F32), b_subln_g[None, :].astype(F32), lam_init, sb + 4)

    sl_c = _alibi_slopes(ng * C_HEADS).reshape(ng, C_HEADS // 2, 2)
    c_outs, c_lses, dilations = [], [], []
    for gi, (window, r) in enumerate(C_PATTERNS):
        radius = window // (2 * r)
        slopes_c = jnp.asarray(sl_c[gi] * LOG2E, F32)
        if r == 1:
            slab0 = (n_a + n_b) // LANES
            o_g, lse_g = _band_attention(y3, pos, radius, slopes_c, None, slab0 // 4, slab0 + 4, slab0 + 8, 4, True)
        else:
            src = y_classes[gi - 1].reshape(bsz * r, seq // r, n_cg)
            pos_r = pos.reshape(bsz, seq // r, r).transpose(0, 2, 1).reshape(bsz * r, seq // r)
            o_g, lse_g = _band_attention(src, pos_r, radius, slopes_c, None, 0, 4, 8, 4, True)
        c_outs.append(o_g)
        c_lses.append(lse_g)
        dilations.append(r)
    oc = _c_merge(c_outs, c_lses, dilations, bsz, seq).reshape(t, -1)

    n0 = n_a_in + n_b + ng * n_cg
    dq = D_NOPE + D_ROPE
    lane_of = _latent_lane_of_dim()

    def to_slab(a, dims):
        return jnp.zeros(a.shape[:-1] + (LANES,), a.dtype).at[..., lane_of[dims]].set(a)

    all_dims, nope_dims, rope_dims = np.arange(dq), np.arange(D_NOPE), np.arange(D_NOPE, dq)
    kr_cols = to_slab(w_in[:, n0 + D_Q_RANK + D_KV_RANK:n0 + D_Q_RANK + D_KV_RANK + D_ROPE], rope_dims)
    w_d = jnp.concatenate([w_in[:, n0:n0 + D_Q_RANK + D_KV_RANK], kr_cols], axis=1).astype(BF16)
    wuq = to_slab(d_w_uq.reshape(D_Q_RANK, D_HEADS, dq), all_dims).reshape(D_Q_RANK, -1).astype(BF16)
    ukv = d_w_ukv.reshape(D_KV_RANK, D_HEADS, D_NOPE + D_V)
    wuk = to_slab(ukv[:, :, :D_NOPE], nope_dims).reshape(D_KV_RANK, -1).astype(BF16)
    wuv = jnp.pad(ukv[:, :, D_NOPE:], ((0, 0), (0, 0), (0, LANES - D_V))).reshape(D_KV_RANK, -1).astype(BF16)
    gq = to_slab(d_q_norm.astype(F32) * (dq ** -0.5 * LOG2E), all_dims)[None, :]
    gk = to_slab(d_k_norm.astype(F32), all_dims)[None, :]
    halfr = D_ROPE // 2
    inv = ROPE_THETA ** (-np.arange(halfr, dtype=np.float32) / halfr)
    inv_tab = np.zeros((1, LANES), np.float32)
    inv_tab[0, lane_of[rope_dims]] = np.concatenate([inv, inv])
    pos_col = pos.astype(F32).reshape(t, 1)
    qt_d, kd, vt_d = _d_proj(h2d, w_d, d_cq_norm[None, :].astype(F32), d_ckv_norm[None, :].astype(F32),
                             wuq, wuk, wuv, gq, gk, pos_col, jnp.asarray(inv_tab), bsz)
    od = _flash_d(qt_d, kd.reshape(bsz, seq, -1), vt_d)

    n_gate0 = n0 + D_Q_RANK + D_KV_RANK + D_ROPE
    w_gate = w_in[:, n_gate0:].astype(BF16)
    wb = jnp.concatenate([w_branch[0][perm][None], w_branch[1:]], axis=0).astype(BF16)
    x1 = _merge(h2d, x2d, [oa.reshape(t, -1), ob.reshape(t, -1), oc, od.reshape(t, -1)],
                w_gate, wb, w_o.astype(BF16))

    x2 = _ffn(x1, seq, norm2_g[None, :].astype(F32), ffn_w_up.astype(BF16), ffn_conv_w.astype(F32),
              ffn_conv_b[None, :].astype(F32), ffn_w_down.astype(BF16))
    return x2.reshape(bsz, seq, d)


def kernel(x, positions, norm1_g, w_in, a_q_norm, a_k_norm, a_sink, b_q_norm, b_k_norm, b_lam_q1,
           b_lam_k1, b_lam_q2, b_lam_k2, b_subln_g, c_q_norm, c_k_norm, d_cq_norm, d_ckv_norm, d_w_uq,
           d_w_ukv, d_q_norm, d_k_norm, w_branch, w_o, norm2_g, ffn_w_up, ffn_conv_w, ffn_conv_b,
           ffn_w_down):
    depth = w_in.shape[0]
    per_layer = (norm1_g, w_in, a_q_norm, a_k_norm, a_sink, b_q_norm, b_k_norm, b_lam_q1, b_lam_k1,
                 b_lam_q2, b_lam_k2, b_subln_g, c_q_norm, c_k_norm, d_cq_norm, d_ckv_norm, d_w_uq,
                 d_w_ukv, d_q_norm, d_k_norm, w_branch, w_o, norm2_g, ffn_w_up, ffn_conv_w, ffn_conv_b,
                 ffn_w_down)
    for layer in range(depth):
        x = _layer(x, positions, layer, *[p[layer] for p in per_layer])
    return x
```
